```python
import math
import jax, jax.numpy as jnp
from jax import lax
import numpy as np

D_MODEL = 2048
BATCH = 1
SEQ = 8192
DEPTH = 4

N_EVEN = (DEPTH + 1) // 2
N_ODD = DEPTH // 2
EPS = 1e-6
ROPE_THETA = 10000.0
Q_BLOCK = 128
A_HEADS = 8
A_HALF_DIM = 64
A_V_DIM = 2 * A_HALF_DIM
A_WIDTH = A_HEADS * A_V_DIM
B_HEADS = 8
B_HEAD_DIM = 128
B_WIDTH = B_HEADS * B_HEAD_DIM
AB_IN = 3 * A_WIDTH + 3 * B_WIDTH
AB_OUT = A_WIDTH + B_WIDTH
C_Q_HEADS = 32
C_KV_HEADS = 4
C_HEAD_DIM = 64
WINDOW = 128
C_KV_WIDTH = C_KV_HEADS * C_HEAD_DIM
C_OUT = C_Q_HEADS * C_HEAD_DIM
C_IN = C_OUT + 2 * C_KV_WIDTH
D_FF = 4 * D_MODEL

kernel_name = 'hybrid_diffattn_stickbreak_swa_sink_trunk'


def rms_norm(x, g):
    xf = x.astype(jnp.float32)
    y = xf * lax.rsqrt(jnp.mean(xf * xf, axis=-1, keepdims=True) + EPS)
    return (y * g.astype(jnp.float32)).astype(x.dtype)


def rope(x, positions):
    d = x.shape[-1]
    inv_freq = ROPE_THETA ** (-jnp.arange(0, d, 2, dtype=jnp.float32) / d)
    ang = positions.astype(jnp.float32)[..., None] * inv_freq
    cos = jnp.cos(ang)[:, :, None, :]
    sin = jnp.sin(ang)[:, :, None, :]
    xf = x.astype(jnp.float32)
    x1, x2 = xf[..., : d // 2], xf[..., d // 2:]
    out = jnp.concatenate([x1 * cos - x2 * sin, x2 * cos + x1 * sin], axis=-1)
    return out.astype(x.dtype)


def lambda_init_fn(layer):
    return 0.8 - 0.6 * math.exp(-0.3 * layer)


def differential_attention(q1, q2, k1, k2, v, lam):
    B, S, H, dh = q1.shape
    nb = S // Q_BLOCK
    scale = dh ** -0.5
    qs = jnp.stack([q1, q2], axis=2)
    ks = jnp.stack([k1, k2], axis=2)
    qblocks = jnp.moveaxis(qs.reshape(B, nb, Q_BLOCK, 2, H, dh), 1, 0)
    kpos = jnp.arange(S)

    def block(args):
        qblk, i = args
        qpos = i * Q_BLOCK + jnp.arange(Q_BLOCK)
        s = jnp.einsum('bqmhd,bkmhd->bmhqk', qblk, ks).astype(jnp.float32) * scale
        causal = kpos[None, :] <= qpos[:, None]
        s = jnp.where(causal, s, -jnp.inf)
        p = jax.nn.softmax(s, axis=-1)
        p = p[:, 0] - lam * p[:, 1]
        return jnp.einsum('bhqk,bkhe->bqhe', p.astype(v.dtype), v)

    o = lax.map(block, (qblocks, jnp.arange(nb)))
    return jnp.moveaxis(o, 0, 1).reshape(B, S, H, 2 * dh)


def stick_breaking_attention(q, k, v):
    B, S, H, d = q.shape
    nb = S // Q_BLOCK
    scale = d ** -0.5
    qblocks = jnp.moveaxis(q.reshape(B, nb, Q_BLOCK, H, d), 1, 0)
    kpos = jnp.arange(S)

    def block(args):
        qblk, i = args
        qpos = i * Q_BLOCK + jnp.arange(Q_BLOCK)
        z = jnp.einsum('bqhd,bkhd->bhqk', qblk, k).astype(jnp.float32) * scale
        strict = kpos[None, :] < qpos[:, None]
        log1m_beta = jnp.where(strict, jax.nn.log_sigmoid(-z), 0.0)
        after = lax.cumsum(log1m_beta, axis=3, reverse=True) - log1m_beta
        w = jnp.where(strict, jnp.exp(jax.nn.log_sigmoid(z) + after), 0.0)
        return jnp.einsum('bhqk,bkhd->bqhd', w.astype(v.dtype), v)

    o = lax.map(block, (qblocks, jnp.arange(nb)))
    return jnp.moveaxis(o, 0, 1).reshape(B, S, H, d)


def sliding_window_sink_attention(q, k, v, sinks):
    B, S, HQ, dh = q.shape
    HKV = k.shape[2]
    G = HQ // HKV
    nb = S // WINDOW
    scale = dh ** -0.5
    qb = q.reshape(B, nb, WINDOW, HKV, G, dh)
    kb = k.reshape(B, nb, WINDOW, HKV, dh)
    vb = v.reshape(B, nb, WINDOW, HKV, dh)
    pad = ((0, 0), (1, 0), (0, 0), (0, 0), (0, 0))
    kband = jnp.concatenate([jnp.pad(kb, pad)[:, :-1], kb], axis=2)
    vband = jnp.concatenate([jnp.pad(vb, pad)[:, :-1], vb], axis=2)
    s = jnp.einsum('bnqhgd,bnkhd->bnhgqk', qb, kband).astype(jnp.float32) * scale
    qi = jnp.arange(WINDOW)[:, None]
    ki = jnp.arange(2 * WINDOW)[None, :]
    dist = qi + WINDOW - ki
    in_window = (dist >= 0) & (dist < WINDOW)
    real_key = (jnp.arange(nb)[:, None, None] > 0) | (ki[None] >= WINDOW)
    mask = in_window[None] & real_key
    s = jnp.where(mask[None, :, None, None], s, -jnp.inf)
    sink = jnp.broadcast_to(sinks.astype(jnp.float32).reshape(1, 1, HKV, G, 1, 1),
                            s.shape[:-1] + (1,))
    p = jax.nn.softmax(jnp.concatenate([s, sink], axis=-1), axis=-1)[..., :-1]
    o = jnp.einsum('bnhgqk,bnkhd->bnqhgd', p.astype(v.dtype), vband)
    return o.reshape(B, S, HQ, dh)


def even_mixer(h, positions, w_in, w_out, lq1, lk1, lq2, lk2, subln, layer):
    B, S, _ = h.shape
    proj = h @ w_in
    splits = [A_WIDTH, 2 * A_WIDTH, 3 * A_WIDTH, 3 * A_WIDTH + B_WIDTH, 3 * A_WIDTH + 2 * B_WIDTH]
    qa, ka, va, qs, ks, vs = jnp.split(proj, splits, axis=-1)
    qa = qa.reshape(B, S, A_HEADS, 2, A_HALF_DIM)
    ka = ka.reshape(B, S, A_HEADS, 2, A_HALF_DIM)
    q1, q2 = rope(qa[..., 0, :], positions), rope(qa[..., 1, :], positions)
    k1, k2 = rope(ka[..., 0, :], positions), rope(ka[..., 1, :], positions)
    lam_init = lambda_init_fn(layer)
    f32 = jnp.float32
    lam = (jnp.exp(jnp.sum(lq1.astype(f32) * lk1.astype(f32)))
           - jnp.exp(jnp.sum(lq2.astype(f32) * lk2.astype(f32))) + lam_init)
    oa = differential_attention(q1, q2, k1, k2, va.reshape(B, S, A_HEADS, A_V_DIM), lam)
    oa = rms_norm(oa, subln) * (1.0 - lam_init)
    ob = stick_breaking_attention(qs.reshape(B, S, B_HEADS, B_HEAD_DIM),
                                  ks.reshape(B, S, B_HEADS, B_HEAD_DIM),
                                  vs.reshape(B, S, B_HEADS, B_HEAD_DIM))
    o = jnp.concatenate([oa.reshape(B, S, A_WIDTH), ob.reshape(B, S, B_WIDTH)], axis=-1)
    return o @ w_out


def odd_mixer(h, positions, w_in, w_out, sinks):
    B, S, _ = h.shape
    proj = h @ w_in
    q, k, v = jnp.split(proj, [C_OUT, C_OUT + C_KV_WIDTH], axis=-1)
    q = rope(q.reshape(B, S, C_Q_HEADS, C_HEAD_DIM), positions)
    k = rope(k.reshape(B, S, C_KV_HEADS, C_HEAD_DIM), positions)
    v = v.reshape(B, S, C_KV_HEADS, C_HEAD_DIM)
    o = sliding_window_sink_attention(q, k, v, sinks)
    return o.reshape(B, S, C_OUT) @ w_out


def squared_relu_mlp(h, w_up, w_down):
    return jnp.square(jax.nn.relu(h @ w_up)) @ w_down


def setup_inputs(seed: int = 0) -> dict:
    key = jax.random.key(seed)
    ks = jax.random.split(key, 16)
    f32 = jnp.float32

    def nrm(k, shape, scale):
        return jax.random.normal(k, shape, f32) * scale

    return {
        'x': nrm(ks[0], (BATCH, SEQ, D_MODEL), 1.0),
        'positions': jnp.tile(jnp.arange(SEQ, dtype=jnp.int32)[None, :], (BATCH, 1)),
        'norm_mix': 1.0 + nrm(ks[1], (DEPTH, D_MODEL), 0.02),
        'norm_mlp': 1.0 + nrm(ks[2], (DEPTH, D_MODEL), 0.02),
        'norm_final': 1.0 + nrm(ks[3], (D_MODEL,), 0.02),
        'w_in_ab': nrm(ks[4], (N_EVEN, D_MODEL, AB_IN), D_MODEL ** -0.5),
        'w_out_ab': nrm(ks[5], (N_EVEN, AB_OUT, D_MODEL), AB_OUT ** -0.5),
        'lambda_q1': nrm(ks[6], (N_EVEN, A_HALF_DIM), 0.1),
        'lambda_k1': nrm(ks[7], (N_EVEN, A_HALF_DIM), 0.1),
        'lambda_q2': nrm(ks[8], (N_EVEN, A_HALF_DIM), 0.1),
        'lambda_k2': nrm(ks[9], (N_EVEN, A_HALF_DIM), 0.1),
        'diff_subln': 1.0 + nrm(ks[10], (N_EVEN, A_V_DIM), 0.02),
        'w_in_c': nrm(ks[11], (N_ODD, D_MODEL, C_IN), D_MODEL ** -0.5),
        'w_out_c': nrm(ks[12], (N_ODD, C_OUT, D_MODEL), C_OUT ** -0.5),
        'sinks': nrm(ks[13], (N_ODD, C_Q_HEADS), 1.0),
        'w_up': nrm(ks[14], (DEPTH, D_MODEL, D_FF), D_MODEL ** -0.5),
        'w_down': nrm(ks[15], (DEPTH, D_FF, D_MODEL), D_FF ** -0.5),
    }


def reference(x, positions, norm_mix, norm_mlp, norm_final, w_in_ab, w_out_ab,
              lambda_q1, lambda_k1, lambda_q2, lambda_k2, diff_subln,
              w_in_c, w_out_c, sinks, w_up, w_down):
    for layer in range(DEPTH):
        h = rms_norm(x, norm_mix[layer])
        j = layer // 2
        if layer % 2 == 0:
            x = x + even_mixer(h, positions, w_in_ab[j], w_out_ab[j],
                               lambda_q1[j], lambda_k1[j], lambda_q2[j], lambda_k2[j],
                               diff_subln[j], layer)
        else:
            x = x + odd_mixer(h, positions, w_in_c[j], w_out_c[j], sinks[j])
        x = x + squared_relu_mlp(rms_norm(x, norm_mlp[layer]), w_up[layer], w_down[layer])
    return rms_norm(x, norm_final)
```

```python
import functools
import math

import jax
import jax.numpy as jnp
from jax import lax
from jax.experimental import pallas as pl
from jax.experimental.pallas import tpu as pltpu

F32 = jnp.float32
BF16 = jnp.bfloat16

EPS = 1e-6
ROPE_THETA = 10000.0
LANES = 128
ROPE_DIM = 64
A_HEADS = 8
B_HEADS = 8
B_HEAD_DIM = 128
C_Q_HEADS = 32
C_KV_HEADS = 4
C_GROUP = C_Q_HEADS // C_KV_HEADS
C_HEAD_DIM = 64
WINDOW = 128
MASK_VALUE = -1e30
MIB = 1024 * 1024

NT_DIMS = (((1,), (1,)), ((), ()))


def _params(semantics, vmem_mib):
    return pltpu.CompilerParams(dimension_semantics=semantics,
                                vmem_limit_bytes=vmem_mib * MIB)


def _rms(x, g):
    ms = jnp.mean(x * x, axis=-1, keepdims=True)
    return x * lax.rsqrt(ms + EPS) * g


def _lane_tile(a, width):
    reps = width // LANES
    return a if reps == 1 else jnp.concatenate([a] * reps, axis=1)


def _rope_table_kernel(pos_ref, invf_ref, cos_ref, sin_ref):
    ang = pos_ref[...].astype(F32) * invf_ref[...]
    lane = lax.broadcasted_iota(jnp.int32, ang.shape, 1)
    sign = jnp.where((lane & (ROPE_DIM // 2)) == 0, -1.0, 1.0)
    cos_ref[...] = jnp.cos(ang)
    sin_ref[...] = jnp.sin(ang) * sign


def _rope_tables(positions):
    s = positions.shape[-1]
    tm = 1024
    inv_freq = ROPE_THETA ** (-jnp.arange(0, ROPE_DIM, 2, dtype=F32) / ROPE_DIM)
    invf = jnp.tile(inv_freq, LANES // (ROPE_DIM // 2)).reshape(1, LANES)
    pos = positions.reshape(s, 1)
    return pl.pallas_call(
        _rope_table_kernel,
        grid=(s // tm,),
        in_specs=[pl.BlockSpec((tm, 1), lambda i: (i, 0)),
                  pl.BlockSpec((1, LANES), lambda i: (0, 0))],
        out_specs=[pl.BlockSpec((tm, LANES), lambda i: (i, 0))] * 2,
        out_shape=[jax.ShapeDtypeStruct((s, LANES), F32)] * 2,
        compiler_params=_params(("arbitrary",), 16),
        name="rope_tables",
    )(pos, invf)


def _rope(x, cos, ss):
    half = ROPE_DIM // 2
    lane = lax.broadcasted_iota(jnp.int32, cos.shape, 1)
    low = (lane & half) == 0
    outs = []
    for c in range(x.shape[1] // LANES):
        xc = x[:, c * LANES:(c + 1) * LANES]
        partner = jnp.where(low, pltpu.roll(xc, LANES - half, 1), pltpu.roll(xc, half, 1))
        outs.append(xc * cos + partner * ss)
    return outs[0] if len(outs) == 1 else jnp.concatenate(outs, axis=1)


def _norm_proj_kernel(x_ref, g_ref, w_ref, cos_ref, sin_ref, o_ref, xn_ref, *,
                      n_rope, n_scaled, scale):
    j = pl.program_id(1)

    @pl.when(j == 0)
    def _():
        xn_ref[...] = _rms(x_ref[...], g_ref[...]).astype(BF16)

    acc = jnp.dot(xn_ref[...], w_ref[...], preferred_element_type=F32)

    @pl.when(j < n_rope)
    def _():
        r = _rope(acc, cos_ref[...], sin_ref[...])
        r = r * jnp.where(j < n_scaled, scale, 1.0)
        o_ref[...] = r.astype(o_ref.dtype)

    @pl.when(j >= n_rope)
    def _():
        o_ref[...] = acc.astype(o_ref.dtype)


def _norm_proj(x, g, w, cos, sin, *, n_rope, n_scaled, scale, tm=1024, tn=512):
    s, d = x.shape
    n = w.shape[1]
    kern = functools.partial(_norm_proj_kernel, n_rope=n_rope, n_scaled=n_scaled, scale=scale)
    return pl.pallas_call(
        kern,
        grid=(s // tm, n // tn),
        in_specs=[pl.BlockSpec((tm, d), lambda i, j: (i, 0)),
                  pl.BlockSpec((1, d), lambda i, j: (0, 0)),
                  pl.BlockSpec((d, tn), lambda i, j: (0, j)),
                  pl.BlockSpec((tm, LANES), lambda i, j: (i, 0)),
                  pl.BlockSpec((tm, LANES), lambda i, j: (i, 0))],
        out_specs=pl.BlockSpec((tm, tn), lambda i, j: (i, j)),
        out_shape=jax.ShapeDtypeStruct((s, n), BF16),
        scratch_shapes=[pltpu.VMEM((tm, d), BF16)],
        compiler_params=_params(("arbitrary", "arbitrary"), 48),
        name="norm_proj",
    )(x, g.reshape(1, d), w, cos, sin)


def _diff_attn_kernel(lam_ref, subln_ref, q_ref, k_ref, v_ref, o_ref,
                      qs_ref, m_ref, l_ref, acc_ref, *, tq, lam_init):
    i = pl.program_id(1)
    rows = 2 * tq
    half = LANES // 2

    q = q_ref[...].astype(F32)
    lane = lax.broadcasted_iota(jnp.int32, q.shape, 1)
    qs_ref[0:tq, :] = jnp.where(lane < half, q, 0.0).astype(BF16)
    qs_ref[tq:rows, :] = jnp.where(lane >= half, q, 0.0).astype(BF16)
    m_ref[...] = jnp.full(m_ref.shape, MASK_VALUE, F32)
    l_ref[...] = jnp.zeros(l_ref.shape, F32)
    acc_ref[...] = jnp.zeros(acc_ref.shape, F32)

    def step(j, masked):
        start = pl.multiple_of(j * tq, tq)
        kb = k_ref[pl.ds(start, tq), :]
        vb = v_ref[pl.ds(start, tq), :]
        s = lax.dot_general(qs_ref[...], kb, NT_DIMS, preferred_element_type=F32)
        if masked:
            row = lax.broadcasted_iota(jnp.int32, s.shape, 0)
            col = lax.broadcasted_iota(jnp.int32, s.shape, 1)
            row = jnp.where(row >= tq, row - tq, row)
            s = jnp.where(col <= row, s, MASK_VALUE)
        m_prev = m_ref[...]
        m_new = jnp.maximum(m_prev, jnp.max(s, axis=1, keepdims=True))
        alpha = jnp.exp(m_prev - m_new)
        p = jnp.exp(s - _lane_tile(m_new, tq))
        l_ref[...] = alpha * l_ref[...] + jnp.sum(p, axis=1, keepdims=True)
        acc_ref[...] = alpha * acc_ref[...] + jnp.dot(p.astype(BF16), vb,
                                                      preferred_element_type=F32)
        m_ref[...] = m_new

    def body(j, carry):
        step(j, False)
        return carry

    lax.fori_loop(0, i, body, 0)
    step(i, True)

    lp = lam_ref[...]
    lam = (jnp.exp(jnp.sum(lp[0:1] * lp[1:2], axis=1, keepdims=True))
           - jnp.exp(jnp.sum(lp[2:3] * lp[3:4], axis=1, keepdims=True)) + lam_init)
    o = acc_ref[...] / l_ref[...]
    d = o[0:tq] - lam * o[tq:rows]
    y = _rms(d, subln_ref[...]) * (1.0 - lam_init)
    o_ref[...] = y.astype(o_ref.dtype)


def _diff_attn(proj, lam_params, subln, lam_init, *, tq=256):
    s = proj.shape[0]
    width = A_HEADS * LANES
    kern = functools.partial(_diff_attn_kernel, tq=tq, lam_init=lam_init)
    return pl.pallas_call(
        kern,
        grid=(A_HEADS, s // tq),
        in_specs=[pl.BlockSpec((4, ROPE_DIM), lambda h, i: (0, 0)),
                  pl.BlockSpec((1, LANES), lambda h, i: (0, 0)),
                  pl.BlockSpec((tq, LANES), lambda h, i: (i, h)),
                  pl.BlockSpec((s, LANES), lambda h, i: (0, A_HEADS + h)),
                  pl.BlockSpec((s, LANES), lambda h, i: (0, 2 * A_HEADS + h))],
        out_specs=pl.BlockSpec((tq, LANES), lambda h, i: (i, h)),
        out_shape=jax.ShapeDtypeStruct((s, width), BF16),
        scratch_shapes=[pltpu.VMEM((2 * tq, LANES), BF16),
                        pltpu.VMEM((2 * tq, LANES), F32),
                        pltpu.VMEM((2 * tq, LANES), F32),
                        pltpu.VMEM((2 * tq, LANES), F32)],
        compiler_params=_params(("arbitrary", "arbitrary"), 32),
        name="diff_attn",
    )(lam_params, subln.reshape(1, LANES), proj, proj, proj)


def _stickbreak_kernel(q_ref, k_ref, v_ref, o_ref, tri_ref, carry_ref, acc_ref, *, tq, scale):
    h = pl.program_id(0)
    i = pl.program_id(1)

    @pl.when((h == 0) & (i == 0))
    def _():
        r = lax.broadcasted_iota(jnp.int32, tri_ref.shape, 0)
        c = lax.broadcasted_iota(jnp.int32, tri_ref.shape, 1)
        tri_ref[...] = jnp.where(r > c, 1.0, 0.0).astype(BF16)

    carry_ref[...] = jnp.zeros(carry_ref.shape, F32)
    acc_ref[...] = jnp.zeros(acc_ref.shape, F32)

    def step(j, masked):
        start = pl.multiple_of(j * tq, tq)
        kb = k_ref[pl.ds(start, tq), :]
        vb = v_ref[pl.ds(start, tq), :]
        z = lax.dot_general(q_ref[...], kb, NT_DIMS, preferred_element_type=F32) * scale
        lb = jnp.minimum(-z, 0.0) - jnp.log1p(jnp.exp(-jnp.abs(z)))
        if masked:
            row = lax.broadcasted_iota(jnp.int32, z.shape, 0)
            col = lax.broadcasted_iota(jnp.int32, z.shape, 1)
            strict = col < row
            lb = jnp.where(strict, lb, 0.0)
        later = jnp.dot(lb.astype(BF16), tri_ref[...], preferred_element_type=F32)
        e = z + lb + later + _lane_tile(carry_ref[...], tq)
        if masked:
            e = jnp.where(strict, e, MASK_VALUE)
        w = jnp.exp(e)
        acc_ref[...] += jnp.dot(w.astype(BF16), vb, preferred_element_type=F32)
        carry_ref[...] += jnp.sum(lb, axis=1, keepdims=True)

    step(i, True)

    def body(t, carry):
        step(i - 1 - t, False)
        return carry

    lax.fori_loop(0, i, body, 0)
    o_ref[...] = acc_ref[...].astype(o_ref.dtype)


def _stickbreak(proj, *, first_col, tq=256):
    s = proj.shape[0]
    width = B_HEADS * B_HEAD_DIM
    c0 = first_col // LANES
    kern = functools.partial(_stickbreak_kernel, tq=tq, scale=B_HEAD_DIM ** -0.5)
    return pl.pallas_call(
        kern,
        grid=(B_HEADS, s // tq),
        in_specs=[pl.BlockSpec((tq, LANES), lambda h, i: (i, c0 + h)),
                  pl.BlockSpec((s, LANES), lambda h, i: (0, c0 + B_HEADS + h)),
                  pl.BlockSpec((s, LANES), lambda h, i: (0, c0 + 2 * B_HEADS + h))],
        out_specs=pl.BlockSpec((tq, LANES), lambda h, i: (i, h)),
        out_shape=jax.ShapeDtypeStruct((s, width), BF16),
        scratch_shapes=[pltpu.VMEM((tq, tq), BF16),
                        pltpu.VMEM((tq, LANES), F32),
                        pltpu.VMEM((tq, LANES), F32)],
        compiler_params=_params(("arbitrary", "arbitrary"), 32),
        name="stickbreak",
    )(proj, proj, proj)


def _swa_kernel(sinks_ref, q_ref, kp_ref, kc_ref, vp_ref, vc_ref, o_ref):
    hk = pl.program_id(0)
    n = pl.program_id(1)
    w = WINDOW
    half = LANES // 2

    def block_diag(prev_ref, cur_ref):
        both = jnp.concatenate([prev_ref[...], cur_ref[...]], axis=0).astype(F32)
        lane = lax.broadcasted_iota(jnp.int32, both.shape, 1)
        low = lane < half
        return jnp.concatenate([jnp.where(low, both, 0.0), jnp.where(low, 0.0, both)],
                               axis=0).astype(BF16)

    kbd = block_diag(kp_ref, kc_ref)
    vbd = block_diag(vp_ref, vc_ref)

    qi = lax.broadcasted_iota(jnp.int32, (w, 2 * w), 0)
    ki = lax.broadcasted_iota(jnp.int32, (w, 2 * w), 1)
    first_real = jnp.where(n > 0, 0, w)
    mask = (ki > qi) & (ki <= qi + w) & (ki >= first_real)
    lane_o = lax.broadcasted_iota(jnp.int32, (w, LANES), 1)

    outs = []
    for p in range(C_GROUP // 2):
        qp = q_ref[:, p * LANES:(p + 1) * LANES]
        s = lax.dot_general(qp, kbd, NT_DIMS, preferred_element_type=F32)
        probs, inv_den = [], []
        for t in range(2):
            st = jnp.where(mask, s[:, t * 2 * w:(t + 1) * 2 * w], MASK_VALUE)
            sink = sinks_ref[hk * C_GROUP + 2 * p + t]
            m = jnp.maximum(jnp.max(st, axis=1, keepdims=True), sink)
            pt = jnp.exp(st - m)
            den = jnp.sum(pt, axis=1, keepdims=True) + jnp.exp(sink - m)
            probs.append(pt)
            inv_den.append(1.0 / den)
        pp = jnp.concatenate(probs, axis=1).astype(BF16)
        o = jnp.dot(pp, vbd, preferred_element_type=F32)
        outs.append(o * jnp.where(lane_o < half, inv_den[0], inv_den[1]))
    o_ref[...] = jnp.concatenate(outs, axis=1).astype(o_ref.dtype)


def _swa(proj, sinks):
    s = proj.shape[0]
    w = WINDOW
    qw = C_GROUP * C_HEAD_DIM
    k0 = C_Q_HEADS * C_HEAD_DIM // LANES
    v0 = k0 + C_KV_HEADS
    prev = lambda n: jnp.maximum(n - 1, 0)
    return pl.pallas_call(
        _swa_kernel,
        grid=(C_KV_HEADS, s // w),
        in_specs=[pl.BlockSpec(memory_space=pltpu.SMEM),
                  pl.BlockSpec((w, qw), lambda h, n: (n, h)),
                  pl.BlockSpec((w, LANES), lambda h, n: (prev(n), k0 + h)),
                  pl.BlockSpec((w, LANES), lambda h, n: (n, k0 + h)),
                  pl.BlockSpec((w, LANES), lambda h, n: (prev(n), v0 + h)),
                  pl.BlockSpec((w, LANES), lambda h, n: (n, v0 + h))],
        out_specs=pl.BlockSpec((w, qw), lambda h, n: (n, h)),
        out_shape=jax.ShapeDtypeStruct((s, C_Q_HEADS * C_HEAD_DIM), BF16),
        compiler_params=_params(("arbitrary", "arbitrary"), 16),
        name="swa_sink",
    )(sinks, proj, proj, proj, proj, proj)


def _out_proj_kernel(*refs, n_in):
    a_refs, w_refs = refs[:n_in], refs[n_in:2 * n_in]
    x_ref, o_ref = refs[2 * n_in], refs[2 * n_in + 1]
    acc = x_ref[...]
    for a_ref, w_ref in zip(a_refs, w_refs):
        acc = acc + jnp.dot(a_ref[...], w_ref[...], preferred_element_type=F32)
    o_ref[...] = acc


def _out_proj(acts, w, x, *, tm=1024, tn=512):
    s, d = x.shape
    n_in = len(acts)
    ka = acts[0].shape[1]
    a_specs = [pl.BlockSpec((tm, ka), lambda i, j: (i, 0)) for _ in acts]
    w_specs = [pl.BlockSpec((ka, tn), lambda i, j, t=t: (t, j)) for t in range(n_in)]
    return pl.pallas_call(
        functools.partial(_out_proj_kernel, n_in=n_in),
        grid=(s // tm, d // tn),
        in_specs=a_specs + w_specs + [pl.BlockSpec((tm, tn), lambda i, j: (i, j))],
        out_specs=pl.BlockSpec((tm, tn), lambda i, j: (i, j)),
        out_shape=jax.ShapeDtypeStruct((s, d), F32),
        compiler_params=_params(("arbitrary", "arbitrary"), 48),
        name="out_proj",
    )(*acts, *([w] * n_in), x)


def _mlp_kernel(x_ref, g_ref, wu_ref, wd_ref, gf_ref, o_ref, xn_ref, *, final_norm):
    f = pl.program_id(1)

    @pl.when(f == 0)
    def _():
        x = x_ref[...]
        xn_ref[...] = _rms(x, g_ref[...]).astype(BF16)
        o_ref[...] = x

    hid = jnp.dot(xn_ref[...], wu_ref[...], preferred_element_type=F32)
    hid = jnp.square(jnp.maximum(hid, 0.0)).astype(BF16)
    o_ref[...] += jnp.dot(hid, wd_ref[...], preferred_element_type=F32)

    if final_norm:
        @pl.when(f == pl.num_programs(1) - 1)
        def _():
            o_ref[...] = _rms(o_ref[...], gf_ref[...])


def _mlp(x, g, w_up, w_down, g_final, *, final_norm, tm=512, tf=512):
    s, d = x.shape
    dff = w_up.shape[1]
    return pl.pallas_call(
        functools.partial(_mlp_kernel, final_norm=final_norm),
        grid=(s // tm, dff // tf),
        in_specs=[pl.BlockSpec((tm, d), lambda i, f: (i, 0)),
                  pl.BlockSpec((1, d), lambda i, f: (0, 0)),
                  pl.BlockSpec((d, tf), lambda i, f: (0, f)),
                  pl.BlockSpec((tf, d), lambda i, f: (f, 0)),
                  pl.BlockSpec((1, d), lambda i, f: (0, 0))],
        out_specs=pl.BlockSpec((tm, d), lambda i, f: (i, 0)),
        out_shape=jax.ShapeDtypeStruct((s, d), F32),
        scratch_shapes=[pltpu.VMEM((tm, d), BF16)],
        compiler_params=_params(("arbitrary", "arbitrary"), 48),
        name="mlp",
    )(x, g.reshape(1, d), w_up, w_down, g_final.reshape(1, d))


def _dup_heads(w, n_heads, dim):
    d = w.shape[0]
    w = w.reshape(d, n_heads, 1, dim)
    return jnp.broadcast_to(w, (d, n_heads, 2, dim)).reshape(d, n_heads * 2 * dim)


def kernel(x, positions, norm_mix, norm_mlp, norm_final, w_in_ab, w_out_ab, lambda_q1, lambda_k1,
           lambda_q2, lambda_k2, diff_subln, w_in_c, w_out_c, sinks, w_up, w_down):
    b, s, d = x.shape
    assert b == 1
    depth = norm_mix.shape[0]
    h = x.reshape(s, d)
    cos, sin = _rope_tables(positions)

    a_width = A_HEADS * LANES
    c_q = C_Q_HEADS * C_HEAD_DIM
    c_kv = C_KV_HEADS * C_HEAD_DIM
    tn = 512
    for layer in range(depth):
        j = layer // 2
        if layer % 2 == 0:
            proj = _norm_proj(h, norm_mix[layer], w_in_ab[j].astype(BF16), cos, sin,
                              n_rope=2 * a_width // tn, n_scaled=a_width // tn,
                              scale=ROPE_DIM ** -0.5, tn=tn)
            lam_init = 0.8 - 0.6 * math.exp(-0.3 * layer)
            lam_params = jnp.stack([lambda_q1[j], lambda_k1[j], lambda_q2[j], lambda_k2[j]])
            oa = _diff_attn(proj, lam_params, diff_subln[j], lam_init)
            ob = _stickbreak(proj, first_col=3 * a_width)
            h = _out_proj([oa, ob], w_out_ab[j].astype(BF16), h)
        else:
            wc = w_in_c[j]
            wq, wk, wv = wc[:, :c_q], wc[:, c_q:c_q + c_kv], wc[:, c_q + c_kv:]
            wcat = jnp.concatenate([wq, _dup_heads(wk, C_KV_HEADS, C_HEAD_DIM),
                                    _dup_heads(wv, C_KV_HEADS, C_HEAD_DIM)], axis=1).astype(BF16)
            proj = _norm_proj(h, norm_mix[layer], wcat, cos, sin,
                              n_rope=(c_q + 2 * c_kv) // tn, n_scaled=c_q // tn,
                              scale=C_HEAD_DIM ** -0.5, tn=tn)
            oc = _swa(proj, sinks[j])
            h = _out_proj([oc], w_out_c[j].astype(BF16), h)
        h = _mlp(h, norm_mlp[layer], w_up[layer].astype(BF16), w_down[layer].astype(BF16),
                 norm_final, final_norm=(layer == depth - 1))
    return h.reshape(b, s, d)
```

```python
import functools
import math

import jax
import jax.numpy as jnp
from jax import lax
from jax.experimental import pallas as pl
from jax.experimental.pallas import tpu as pltpu

F32 = jnp.float32
BF16 = jnp.bfloat16

EPS = 1e-6
ROPE_THETA = 10000.0
LANES = 128
ROPE_DIM = 64
A_HEADS = 8
B_HEADS = 8
B_HEAD_DIM = 128
C_Q_HEADS = 32
C_KV_HEADS = 4
C_GROUP = C_Q_HEADS // C_KV_HEADS
C_HEAD_DIM = 64
WINDOW = 128
MASK_VALUE = -1e30
MIB = 1024 * 1024

NT_DIMS = (((1,), (1,)), ((), ()))


def _params(semantics, vmem_mib):
    return pltpu.CompilerParams(dimension_semantics=semantics,
                                vmem_limit_bytes=vmem_mib * MIB)


def _rms(x, g):
    ms = jnp.mean(x * x, axis=-1, keepdims=True)
    return x * lax.rsqrt(ms + EPS) * g


def _lane_tile(a, width):
    reps = width // LANES
    return a if reps == 1 else jnp.concatenate([a] * reps, axis=1)


def _head(ref, g, rows=slice(None)):
    return ref[rows, g * LANES:(g + 1) * LANES]


def _rope_table_kernel(pos_ref, invf_ref, cos_ref, sin_ref):
    ang = pos_ref[...].astype(F32) * invf_ref[...]
    lane = lax.broadcasted_iota(jnp.int32, ang.shape, 1)
    sign = jnp.where((lane & (ROPE_DIM // 2)) == 0, -1.0, 1.0)
    cos_ref[...] = jnp.cos(ang)
    sin_ref[...] = jnp.sin(ang) * sign


def _rope_tables(positions):
    s = positions.shape[-1]
    tm = 1024
    inv_freq = ROPE_THETA ** (-jnp.arange(0, ROPE_DIM, 2, dtype=F32) / ROPE_DIM)
    invf = jnp.tile(inv_freq, LANES // (ROPE_DIM // 2)).reshape(1, LANES)
    pos = positions.reshape(s, 1)
    return pl.pallas_call(
        _rope_table_kernel,
        grid=(s // tm,),
        in_specs=[pl.BlockSpec((tm, 1), lambda i: (i, 0)),
                  pl.BlockSpec((1, LANES), lambda i: (0, 0))],
        out_specs=[pl.BlockSpec((tm, LANES), lambda i: (i, 0))] * 2,
        out_shape=[jax.ShapeDtypeStruct((s, LANES), F32)] * 2,
        compiler_params=_params(("arbitrary",), 16),
        name="rope_tables",
    )(pos, invf)


def _rope(x, cos, ss):
    half = ROPE_DIM // 2
    lane = lax.broadcasted_iota(jnp.int32, cos.shape, 1)
    low = (lane & half) == 0
    outs = []
    for c in range(x.shape[1] // LANES):
        xc = x[:, c * LANES:(c + 1) * LANES]
        partner = jnp.where(low, pltpu.roll(xc, LANES - half, 1), pltpu.roll(xc, half, 1))
        outs.append(xc * cos + partner * ss)
    return outs[0] if len(outs) == 1 else jnp.concatenate(outs, axis=1)


def _norm_proj_kernel(x_ref, g_ref, w_ref, cos_ref, sin_ref, o_ref, xn_ref, *,
                      n_rope, n_scaled, scale):
    j = pl.program_id(1)

    @pl.when(j == 0)
    def _():
        xn_ref[...] = _rms(x_ref[...], g_ref[...]).astype(BF16)

    acc = jnp.dot(xn_ref[...], w_ref[...], preferred_element_type=F32)

    @pl.when(j < n_rope)
    def _():
        r = _rope(acc, cos_ref[...], sin_ref[...])
        r = r * jnp.where(j < n_scaled, scale, 1.0)
        o_ref[...] = r.astype(o_ref.dtype)

    @pl.when(j >= n_rope)
    def _():
        o_ref[...] = acc.astype(o_ref.dtype)


def _norm_proj(x, g, w, cos, sin, *, n_rope, n_scaled, scale, tm=1024, tn=512):
    s, d = x.shape
    n = w.shape[1]
    kern = functools.partial(_norm_proj_kernel, n_rope=n_rope, n_scaled=n_scaled, scale=scale)
    return pl.pallas_call(
        kern,
        grid=(s // tm, n // tn),
        in_specs=[pl.BlockSpec((tm, d), lambda i, j: (i, 0)),
                  pl.BlockSpec((1, d), lambda i, j: (0, 0)),
                  pl.BlockSpec((d, tn), lambda i, j: (0, j)),
                  pl.BlockSpec((tm, LANES), lambda i, j: (i, 0)),
                  pl.BlockSpec((tm, LANES), lambda i, j: (i, 0))],
        out_specs=pl.BlockSpec((tm, tn), lambda i, j: (i, j)),
        out_shape=jax.ShapeDtypeStruct((s, n), BF16),
        scratch_shapes=[pltpu.VMEM((tm, d), BF16)],
        compiler_params=_params(("arbitrary", "arbitrary"), 48),
        name="norm_proj",
    )(x, g.reshape(1, d), w, cos, sin)


def _diff_attn_kernel(lam_ref, subln_ref, q_ref, k_ref, v_ref, o_ref,
                      qs_ref, m_ref, l_ref, acc_ref, *, tq, tk, heads, lam_init):
    i = pl.program_id(1)
    rows = 2 * tq
    half = LANES // 2
    sub = tq // tk

    for g in range(heads):
        q = _head(q_ref, g).astype(F32)
        lane = lax.broadcasted_iota(jnp.int32, q.shape, 1)
        qs_ref[g, 0:tq, :] = jnp.where(lane < half, q, 0.0).astype(BF16)
        qs_ref[g, tq:rows, :] = jnp.where(lane >= half, q, 0.0).astype(BF16)
    m_ref[...] = jnp.full(m_ref.shape, MASK_VALUE, F32)
    l_ref[...] = jnp.zeros(l_ref.shape, F32)
    acc_ref[...] = jnp.zeros(acc_ref.shape, F32)

    def step(j, diag_block):
        start = pl.multiple_of(j * tk, tk)
        for g in range(heads):
            kb = _head(k_ref, g, pl.ds(start, tk))
            vb = _head(v_ref, g, pl.ds(start, tk))
            s = lax.dot_general(qs_ref[g], kb, NT_DIMS, preferred_element_type=F32)
            if diag_block is not None:
                row = lax.broadcasted_iota(jnp.int32, s.shape, 0)
                col = lax.broadcasted_iota(jnp.int32, s.shape, 1) + diag_block * tk
                row = jnp.where(row >= tq, row - tq, row)
                s = jnp.where(col <= row, s, MASK_VALUE)
            m_prev = m_ref[g]
            m_new = jnp.maximum(m_prev, jnp.max(s, axis=1, keepdims=True))
            alpha = jnp.exp2(m_prev - m_new)
            p = jnp.exp2(s - _lane_tile(m_new, tk))
            l_ref[g] = alpha * l_ref[g] + jnp.sum(p, axis=1, keepdims=True)
            acc_ref[g] = alpha * acc_ref[g] + jnp.dot(p.astype(BF16), vb,
                                                      preferred_element_type=F32)
            m_ref[g] = m_new

    def body(t, carry):
        for u in range(sub):
            step(t * sub + u, None)
        return carry

    lax.fori_loop(0, i, body, 0)
    for u in range(sub):
        step(i * sub + u, u)

    lp = lam_ref[...]
    lam = (jnp.exp(jnp.sum(lp[0:1] * lp[1:2], axis=1, keepdims=True))
           - jnp.exp(jnp.sum(lp[2:3] * lp[3:4], axis=1, keepdims=True)) + lam_init)
    for g in range(heads):
        o = acc_ref[g] / l_ref[g]
        d = o[0:tq] - lam * o[tq:rows]
        y = _rms(d, subln_ref[...]) * (1.0 - lam_init)
        o_ref[:, g * LANES:(g + 1) * LANES] = y.astype(o_ref.dtype)


def _diff_attn(proj, lam_params, subln, lam_init, *, tq=512, tk=512, heads=2):
    s = proj.shape[0]
    width = A_HEADS * LANES
    hw = heads * LANES
    groups = A_HEADS // heads
    kern = functools.partial(_diff_attn_kernel, tq=tq, tk=tk, heads=heads, lam_init=lam_init)
    return pl.pallas_call(
        kern,
        grid=(groups, s // tq),
        in_specs=[pl.BlockSpec((4, ROPE_DIM), lambda h, i: (0, 0)),
                  pl.BlockSpec((1, LANES), lambda h, i: (0, 0)),
                  pl.BlockSpec((tq, hw), lambda h, i: (i, h)),
                  pl.BlockSpec((s, hw), lambda h, i: (0, groups + h)),
                  pl.BlockSpec((s, hw), lambda h, i: (0, 2 * groups + h))],
        out_specs=pl.BlockSpec((tq, hw), lambda h, i: (i, h)),
        out_shape=jax.ShapeDtypeStruct((s, width), BF16),
        scratch_shapes=[pltpu.VMEM((heads, 2 * tq, LANES), BF16),
                        pltpu.VMEM((heads, 2 * tq, LANES), F32),
                        pltpu.VMEM((heads, 2 * tq, LANES), F32),
                        pltpu.VMEM((heads, 2 * tq, LANES), F32)],
        compiler_params=_params(("arbitrary", "arbitrary"), 48),
        name="diff_attn",
    )(lam_params, subln.reshape(1, LANES), proj, proj, proj)


def _stickbreak_kernel(q_ref, k_ref, v_ref, o_ref, tri_ref, carry_ref, acc_ref, *,
                       tq, tk, heads, scale):
    h = pl.program_id(0)
    i = pl.program_id(1)
    sub = tq // tk

    @pl.when((h == 0) & (i == 0))
    def _():
        r = lax.broadcasted_iota(jnp.int32, tri_ref.shape, 0)
        c = lax.broadcasted_iota(jnp.int32, tri_ref.shape, 1)
        tri_ref[...] = jnp.where(r > c, 1.0, 0.0).astype(BF16)

    carry_ref[...] = jnp.zeros(carry_ref.shape, F32)
    acc_ref[...] = jnp.zeros(acc_ref.shape, F32)

    def step(j, diag_block):
        start = pl.multiple_of(j * tk, tk)
        for g in range(heads):
            kb = _head(k_ref, g, pl.ds(start, tk))
            vb = _head(v_ref, g, pl.ds(start, tk))
            z = lax.dot_general(_head(q_ref, g), kb, NT_DIMS,
                                preferred_element_type=F32) * scale
            lb = jnp.minimum(-z, 0.0) - jnp.log(1.0 + jnp.exp(-jnp.abs(z)))
            if diag_block is not None:
                row = lax.broadcasted_iota(jnp.int32, z.shape, 0)
                col = lax.broadcasted_iota(jnp.int32, z.shape, 1) + diag_block * tk
                strict = col < row
                lb = jnp.where(strict, lb, 0.0)
            later = jnp.dot(lb.astype(BF16), tri_ref[...], preferred_element_type=F32)
            e = z + lb + later + _lane_tile(carry_ref[g], tk)
            if diag_block is not None:
                e = jnp.where(strict, e, MASK_VALUE)
            w = jnp.exp(e)
            acc_ref[g] += jnp.dot(w.astype(BF16), vb, preferred_element_type=F32)
            carry_ref[g] += jnp.sum(lb, axis=1, keepdims=True)

    for u in reversed(range(sub)):
        step(i * sub + u, u)

    def body(t, carry):
        for u in reversed(range(sub)):
            step((i - 1 - t) * sub + u, None)
        return carry

    lax.fori_loop(0, i, body, 0)
    for g in range(heads):
        o_ref[:, g * LANES:(g + 1) * LANES] = acc_ref[g].astype(o_ref.dtype)


def _stickbreak(proj, *, first_col, tq=512, tk=256, heads=2):
    s = proj.shape[0]
    width = B_HEADS * B_HEAD_DIM
    hw = heads * LANES
    groups = B_HEADS // heads
    c0 = first_col // hw
    kern = functools.partial(_stickbreak_kernel, tq=tq, tk=tk, heads=heads,
                             scale=B_HEAD_DIM ** -0.5)
    return pl.pallas_call(
        kern,
        grid=(groups, s // tq),
        in_specs=[pl.BlockSpec((tq, hw), lambda h, i: (i, c0 + h)),
                  pl.BlockSpec((s, hw), lambda h, i: (0, c0 + groups + h)),
                  pl.BlockSpec((s, hw), lambda h, i: (0, c0 + 2 * groups + h))],
        out_specs=pl.BlockSpec((tq, hw), lambda h, i: (i, h)),
        out_shape=jax.ShapeDtypeStruct((s, width), BF16),
        scratch_shapes=[pltpu.VMEM((tk, tk), BF16),
                        pltpu.VMEM((heads, tq, LANES), F32),
                        pltpu.VMEM((heads, tq, LANES), F32)],
        compiler_params=_params(("arbitrary", "arbitrary"), 48),
        name="stickbreak",
    )(proj, proj, proj)


def _swa_kernel(sinks_ref, q_ref, kp_ref, kc_ref, vp_ref, vc_ref, o_ref):
    hk = pl.program_id(0)
    n = pl.program_id(1)
    w = WINDOW
    half = LANES // 2

    def block_diag(prev_ref, cur_ref):
        both = jnp.concatenate([prev_ref[...], cur_ref[...]], axis=0).astype(F32)
        lane = lax.broadcasted_iota(jnp.int32, both.shape, 1)
        low = lane < half
        return jnp.concatenate([jnp.where(low, both, 0.0), jnp.where(low, 0.0, both)],
                               axis=0).astype(BF16)

    kbd = block_diag(kp_ref, kc_ref)
    vbd = block_diag(vp_ref, vc_ref)

    qi = lax.broadcasted_iota(jnp.int32, (w, 2 * w), 0)
    ki = lax.broadcasted_iota(jnp.int32, (w, 2 * w), 1)
    first_real = jnp.where(n > 0, 0, w)
    mask = (ki > qi) & (ki <= qi + w) & (ki >= first_real)
    lane_o = lax.broadcasted_iota(jnp.int32, (w, LANES), 1)

    outs = []
    for p in range(C_GROUP // 2):
        qp = q_ref[:, p * LANES:(p + 1) * LANES]
        s = lax.dot_general(qp, kbd, NT_DIMS, preferred_element_type=F32)
        probs, inv_den = [], []
        for t in range(2):
            st = jnp.where(mask, s[:, t * 2 * w:(t + 1) * 2 * w], MASK_VALUE)
            sink = sinks_ref[hk * C_GROUP + 2 * p + t]
            m = jnp.maximum(jnp.max(st, axis=1, keepdims=True), sink)
            pt = jnp.exp(st - m)
            den = jnp.sum(pt, axis=1, keepdims=True) + jnp.exp(sink - m)
            probs.append(pt)
            inv_den.append(1.0 / den)
        pp = jnp.concatenate(probs, axis=1).astype(BF16)
        o = jnp.dot(pp, vbd, preferred_element_type=F32)
        outs.append(o * jnp.where(lane_o < half, inv_den[0], inv_den[1]))
    o_ref[...] = jnp.concatenate(outs, axis=1).astype(o_ref.dtype)


def _swa(proj, sinks):
    s = proj.shape[0]
    w = WINDOW
    qw = C_GROUP * C_HEAD_DIM
    k0 = C_Q_HEADS * C_HEAD_DIM // LANES
    v0 = k0 + C_KV_HEADS
    prev = lambda n: jnp.maximum(n - 1, 0)
    return pl.pallas_call(
        _swa_kernel,
        grid=(C_KV_HEADS, s // w),
        in_specs=[pl.BlockSpec(memory_space=pltpu.SMEM),
                  pl.BlockSpec((w, qw), lambda h, n: (n, h)),
                  pl.BlockSpec((w, LANES), lambda h, n: (prev(n), k0 + h)),
                  pl.BlockSpec((w, LANES), lambda h, n: (n, k0 + h)),
                  pl.BlockSpec((w, LANES), lambda h, n: (prev(n), v0 + h)),
                  pl.BlockSpec((w, LANES), lambda h, n: (n, v0 + h))],
        out_specs=pl.BlockSpec((w, qw), lambda h, n: (n, h)),
        out_shape=jax.ShapeDtypeStruct((s, C_Q_HEADS * C_HEAD_DIM), BF16),
        compiler_params=_params(("arbitrary", "arbitrary"), 16),
        name="swa_sink",
    )(sinks, proj, proj, proj, proj, proj)


def _out_proj_kernel(*refs, n_in):
    a_refs, w_refs = refs[:n_in], refs[n_in:2 * n_in]
    x_ref, o_ref = refs[2 * n_in], refs[2 * n_in + 1]
    acc = x_ref[...]
    for a_ref, w_ref in zip(a_refs, w_refs):
        acc = acc + jnp.dot(a_ref[...], w_ref[...], preferred_element_type=F32)
    o_ref[...] = acc


def _out_proj(acts, w, x, *, tm=1024, tn=512):
    s, d = x.shape
    n_in = len(acts)
    ka = acts[0].shape[1]
    a_specs = [pl.BlockSpec((tm, ka), lambda i, j: (i, 0)) for _ in acts]
    w_specs = [pl.BlockSpec((ka, tn), lambda i, j, t=t: (t, j)) for t in range(n_in)]
    return pl.pallas_call(
        functools.partial(_out_proj_kernel, n_in=n_in),
        grid=(s // tm, d // tn),
        in_specs=a_specs + w_specs + [pl.BlockSpec((tm, tn), lambda i, j: (i, j))],
        out_specs=pl.BlockSpec((tm, tn), lambda i, j: (i, j)),
        out_shape=jax.ShapeDtypeStruct((s, d), F32),
        compiler_params=_params(("arbitrary", "arbitrary"), 48),
        name="out_proj",
    )(*acts, *([w] * n_in), x)


def _mlp_kernel(x_ref, g_ref, wu_ref, wd_ref, gf_ref, o_ref, xn_ref, *, final_norm):
    f = pl.program_id(1)

    @pl.when(f == 0)
    def _():
        x = x_ref[...]
        xn_ref[...] = _rms(x, g_ref[...]).astype(BF16)
        o_ref[...] = x

    hid = jnp.dot(xn_ref[...], wu_ref[...], preferred_element_type=F32)
    hid = jnp.square(jnp.maximum(hid, 0.0)).astype(BF16)
    o_ref[...] += jnp.dot(hid, wd_ref[...], preferred_element_type=F32)

    if final_norm:
        @pl.when(f == pl.num_programs(1) - 1)
        def _():
            o_ref[...] = _rms(o_ref[...], gf_ref[...])


def _mlp(x, g, w_up, w_down, g_final, *, final_norm, tm=512, tf=512):
    s, d = x.shape
    dff = w_up.shape[1]
    return pl.pallas_call(
        functools.partial(_mlp_kernel, final_norm=final_norm),
        grid=(s // tm, dff // tf),
        in_specs=[pl.BlockSpec((tm, d), lambda i, f: (i, 0)),
                  pl.BlockSpec((1, d), lambda i, f: (0, 0)),
                  pl.BlockSpec((d, tf), lambda i, f: (0, f)),
                  pl.BlockSpec((tf, d), lambda i, f: (f, 0)),
                  pl.BlockSpec((1, d), lambda i, f: (0, 0))],
        out_specs=pl.BlockSpec((tm, d), lambda i, f: (i, 0)),
        out_shape=jax.ShapeDtypeStruct((s, d), F32),
        scratch_shapes=[pltpu.VMEM((tm, d), BF16)],
        compiler_params=_params(("arbitrary", "arbitrary"), 48),
        name="mlp",
    )(x, g.reshape(1, d), w_up, w_down, g_final.reshape(1, d))


def _dup_heads(w, n_heads, dim):
    d = w.shape[0]
    w = w.reshape(d, n_heads, 1, dim)
    return jnp.broadcast_to(w, (d, n_heads, 2, dim)).reshape(d, n_heads * 2 * dim)


def kernel(x, positions, norm_mix, norm_mlp, norm_final, w_in_ab, w_out_ab, lambda_q1, lambda_k1,
           lambda_q2, lambda_k2, diff_subln, w_in_c, w_out_c, sinks, w_up, w_down):
    b, s, d = x.shape
    assert b == 1
    depth = norm_mix.shape[0]
    h = x.reshape(s, d)
    cos, sin = _rope_tables(positions)

    a_width = A_HEADS * LANES
    c_q = C_Q_HEADS * C_HEAD_DIM
    c_kv = C_KV_HEADS * C_HEAD_DIM
    tn = 512
    for layer in range(depth):
        j = layer // 2
        if layer % 2 == 0:
            proj = _norm_proj(h, norm_mix[layer], w_in_ab[j].astype(BF16), cos, sin,
                              n_rope=2 * a_width // tn, n_scaled=a_width // tn,
                              scale=ROPE_DIM ** -0.5 * math.log2(math.e), tn=tn)
            lam_init = 0.8 - 0.6 * math.exp(-0.3 * layer)
            lam_params = jnp.stack([lambda_q1[j], lambda_k1[j], lambda_q2[j], lambda_k2[j]])
            oa = _diff_attn(proj, lam_params, diff_subln[j], lam_init)
            ob = _stickbreak(proj, first_col=3 * a_width)
            h = _out_proj([oa, ob], w_out_ab[j].astype(BF16), h)
        else:
            wc = w_in_c[j]
            wq, wk, wv = wc[:, :c_q], wc[:, c_q:c_q + c_kv], wc[:, c_q + c_kv:]
            wcat = jnp.concatenate([wq, _dup_heads(wk, C_KV_HEADS, C_HEAD_DIM),
                                    _dup_heads(wv, C_KV_HEADS, C_HEAD_DIM)], axis=1).astype(BF16)
            proj = _norm_proj(h, norm_mix[layer], wcat, cos, sin,
                              n_rope=(c_q + 2 * c_kv) // tn, n_scaled=c_q // tn,
                              scale=C_HEAD_DIM ** -0.5, tn=tn)
            oc = _swa(proj, sinks[j])
            h = _out_proj([oc], w_out_c[j].astype(BF16), h)
        h = _mlp(h, norm_mlp[layer], w_up[layer].astype(BF16), w_down[layer].astype(BF16),
                 norm_final, final_norm=(layer == depth - 1))
    return h.reshape(b, s, d)
```

```python
import functools
import math

import jax
import jax.numpy as jnp
from jax import lax
from jax.experimental import pallas as pl
from jax.experimental.pallas import tpu as pltpu

F32 = jnp.float32
BF16 = jnp.bfloat16

EPS = 1e-6
ROPE_THETA = 10000.0
LANES = 128
ROPE_DIM = 64
A_HEADS = 8
B_HEADS = 8
B_HEAD_DIM = 128
C_Q_HEADS = 32
C_KV_HEADS = 4
C_GROUP = C_Q_HEADS // C_KV_HEADS
C_HEAD_DIM = 64
WINDOW = 128
MASK_VALUE = -1e30
MIB = 1024 * 1024

NT_DIMS = (((1,), (1,)), ((), ()))


def _params(semantics, vmem_mib):
    return pltpu.CompilerParams(dimension_semantics=semantics,
                                vmem_limit_bytes=vmem_mib * MIB)


def _rms(x, g):
    ms = jnp.mean(x * x, axis=-1, keepdims=True)
    return x * lax.rsqrt(ms + EPS) * g


def _lane_tile(a, width):
    reps = width // LANES
    return a if reps == 1 else jnp.concatenate([a] * reps, axis=1)


def _head(ref, g, rows=slice(None)):
    return ref[rows, g * LANES:(g + 1) * LANES]


def _rope_table_kernel(pos_ref, invf_ref, cos_ref, sin_ref):
    ang = pos_ref[...].astype(F32) * invf_ref[...]
    lane = lax.broadcasted_iota(jnp.int32, ang.shape, 1)
    sign = jnp.where((lane & (ROPE_DIM // 2)) == 0, -1.0, 1.0)
    cos_ref[...] = jnp.cos(ang)
    sin_ref[...] = jnp.sin(ang) * sign


def _rope_tables(positions):
    s = positions.shape[-1]
    tm = 1024
    inv_freq = ROPE_THETA ** (-jnp.arange(0, ROPE_DIM, 2, dtype=F32) / ROPE_DIM)
    invf = jnp.tile(inv_freq, LANES // (ROPE_DIM // 2)).reshape(1, LANES)
    pos = positions.reshape(s, 1)
    return pl.pallas_call(
        _rope_table_kernel,
        grid=(s // tm,),
        in_specs=[pl.BlockSpec((tm, 1), lambda i: (i, 0)),
                  pl.BlockSpec((1, LANES), lambda i: (0, 0))],
        out_specs=[pl.BlockSpec((tm, LANES), lambda i: (i, 0))] * 2,
        out_shape=[jax.ShapeDtypeStruct((s, LANES), F32)] * 2,
        compiler_params=_params(("arbitrary",), 16),
        name="rope_tables",
    )(pos, invf)


def _rope(x, cos, ss):
    half = ROPE_DIM // 2
    lane = lax.broadcasted_iota(jnp.int32, cos.shape, 1)
    low = (lane & half) == 0
    outs = []
    for c in range(x.shape[1] // LANES):
        xc = x[:, c * LANES:(c + 1) * LANES]
        partner = jnp.where(low, pltpu.roll(xc, LANES - half, 1), pltpu.roll(xc, half, 1))
        outs.append(xc * cos + partner * ss)
    return outs[0] if len(outs) == 1 else jnp.concatenate(outs, axis=1)


def _norm_proj_kernel(x_ref, g_ref, w_ref, cos_ref, sin_ref, o_ref, xn_ref, *,
                      n_rope, n_scaled, scale):
    j = pl.program_id(1)

    @pl.when(j == 0)
    def _():
        xn_ref[...] = _rms(x_ref[...], g_ref[...]).astype(BF16)

    acc = jnp.dot(xn_ref[...], w_ref[...].astype(BF16), preferred_element_type=F32)

    @pl.when(j < n_rope)
    def _():
        r = _rope(acc, cos_ref[...], sin_ref[...])
        r = r * jnp.where(j < n_scaled, scale, 1.0)
        o_ref[...] = r.astype(o_ref.dtype)

    @pl.when(j >= n_rope)
    def _():
        o_ref[...] = acc.astype(o_ref.dtype)


def _norm_proj(x, g, w, layer, cos, sin, *, n_rope, n_scaled, scale, tm=1024, tn=512):
    s, d = x.shape
    n = w.shape[2]
    kern = functools.partial(_norm_proj_kernel, n_rope=n_rope, n_scaled=n_scaled, scale=scale)
    return pl.pallas_call(
        kern,
        grid=(s // tm, n // tn),
        in_specs=[pl.BlockSpec((tm, d), lambda i, j: (i, 0)),
                  pl.BlockSpec((1, d), lambda i, j: (0, 0)),
                  pl.BlockSpec((None, d, tn), lambda i, j: (layer, 0, j)),
                  pl.BlockSpec((tm, LANES), lambda i, j: (i, 0)),
                  pl.BlockSpec((tm, LANES), lambda i, j: (i, 0))],
        out_specs=pl.BlockSpec((tm, tn), lambda i, j: (i, j)),
        out_shape=jax.ShapeDtypeStruct((s, n), BF16),
        scratch_shapes=[pltpu.VMEM((tm, d), BF16)],
        compiler_params=_params(("arbitrary", "arbitrary"), 48),
        name="norm_proj",
    )(x, g.reshape(1, d), w, cos, sin)


def _diff_attn_kernel(lam_ref, subln_ref, q_ref, k_ref, v_ref, o_ref,
                      qs_ref, m_ref, l_ref, acc_ref, *, tq, tk, heads, lam_init):
    i = pl.program_id(1)
    rows = 2 * tq
    half = LANES // 2
    sub = tq // tk

    for g in range(heads):
        q = _head(q_ref, g).astype(F32)
        lane = lax.broadcasted_iota(jnp.int32, q.shape, 1)
        qs_ref[g, 0:tq, :] = jnp.where(lane < half, q, 0.0).astype(BF16)
        qs_ref[g, tq:rows, :] = jnp.where(lane >= half, q, 0.0).astype(BF16)
    m_ref[...] = jnp.full(m_ref.shape, MASK_VALUE, F32)
    l_ref[...] = jnp.zeros(l_ref.shape, F32)
    acc_ref[...] = jnp.zeros(acc_ref.shape, F32)

    def step(j, diag_block):
        start = pl.multiple_of(j * tk, tk)
        for g in range(heads):
            kb = _head(k_ref, g, pl.ds(start, tk))
            vb = _head(v_ref, g, pl.ds(start, tk))
            s = lax.dot_general(qs_ref[g], kb, NT_DIMS, preferred_element_type=F32)
            if diag_block is not None:
                row = lax.broadcasted_iota(jnp.int32, s.shape, 0)
                col = lax.broadcasted_iota(jnp.int32, s.shape, 1) + diag_block * tk
                row = jnp.where(row >= tq, row - tq, row)
                s = jnp.where(col <= row, s, MASK_VALUE)
            m_prev = m_ref[g]
            m_new = jnp.maximum(m_prev, jnp.max(s, axis=1, keepdims=True))
            alpha = jnp.exp2(m_prev - m_new)
            p = jnp.exp2(s - _lane_tile(m_new, tk))
            l_ref[g] = alpha * l_ref[g] + jnp.sum(p, axis=1, keepdims=True)
            acc_ref[g] = alpha * acc_ref[g] + jnp.dot(p.astype(BF16), vb,
                                                      preferred_element_type=F32)
            m_ref[g] = m_new

    def body(t, carry):
        for u in range(sub):
            step(t * sub + u, None)
        return carry

    lax.fori_loop(0, i, body, 0)
    for u in range(sub):
        step(i * sub + u, u)

    lp = lam_ref[...]
    lam = (jnp.exp(jnp.sum(lp[0:1] * lp[1:2], axis=1, keepdims=True))
           - jnp.exp(jnp.sum(lp[2:3] * lp[3:4], axis=1, keepdims=True)) + lam_init)
    for g in range(heads):
        o = acc_ref[g] / l_ref[g]
        d = o[0:tq] - lam * o[tq:rows]
        y = _rms(d, subln_ref[...]) * (1.0 - lam_init)
        o_ref[:, g * LANES:(g + 1) * LANES] = y.astype(o_ref.dtype)


def _diff_attn(proj, lam_params, subln, lam_init, *, tq=512, tk=512, heads=2):
    s = proj.shape[0]
    width = A_HEADS * LANES
    hw = heads * LANES
    groups = A_HEADS // heads
    kern = functools.partial(_diff_attn_kernel, tq=tq, tk=tk, heads=heads, lam_init=lam_init)
    return pl.pallas_call(
        kern,
        grid=(groups, s // tq),
        in_specs=[pl.BlockSpec((4, ROPE_DIM), lambda h, i: (0, 0)),
                  pl.BlockSpec((1, LANES), lambda h, i: (0, 0)),
                  pl.BlockSpec((tq, hw), lambda h, i: (i, h)),
                  pl.BlockSpec((s, hw), lambda h, i: (0, groups + h)),
                  pl.BlockSpec((s, hw), lambda h, i: (0, 2 * groups + h))],
        out_specs=pl.BlockSpec((tq, hw), lambda h, i: (i, h)),
        out_shape=jax.ShapeDtypeStruct((s, width), BF16),
        scratch_shapes=[pltpu.VMEM((heads, 2 * tq, LANES), BF16),
                        pltpu.VMEM((heads, 2 * tq, LANES), F32),
                        pltpu.VMEM((heads, 2 * tq, LANES), F32),
                        pltpu.VMEM((heads, 2 * tq, LANES), F32)],
        compiler_params=_params(("arbitrary", "arbitrary"), 48),
        name="diff_attn",
    )(lam_params, subln.reshape(1, LANES), proj, proj, proj)


def _stickbreak_kernel(q_ref, k_ref, v_ref, o_ref, tri_ref, carry_ref, acc_ref, *,
                       tq, tk, heads, scale):
    h = pl.program_id(0)
    i = pl.program_id(1)
    sub = tq // tk

    @pl.when((h == 0) & (i == 0))
    def _():
        r = lax.broadcasted_iota(jnp.int32, tri_ref.shape, 0)
        c = lax.broadcasted_iota(jnp.int32, tri_ref.shape, 1)
        tri_ref[...] = jnp.where(r > c, 1.0, 0.0).astype(BF16)

    carry_ref[...] = jnp.zeros(carry_ref.shape, F32)
    acc_ref[...] = jnp.zeros(acc_ref.shape, F32)

    def step(j, diag_block):
        start = pl.multiple_of(j * tk, tk)
        for g in range(heads):
            kb = _head(k_ref, g, pl.ds(start, tk))
            vb = _head(v_ref, g, pl.ds(start, tk))
            z = lax.dot_general(_head(q_ref, g), kb, NT_DIMS,
                                preferred_element_type=F32) * scale
            lb = jnp.minimum(-z, 0.0) - jnp.log(1.0 + jnp.exp(-jnp.abs(z)))
            if diag_block is not None:
                row = lax.broadcasted_iota(jnp.int32, z.shape, 0)
                col = lax.broadcasted_iota(jnp.int32, z.shape, 1) + diag_block * tk
                strict = col < row
                lb = jnp.where(strict, lb, 0.0)
            later = jnp.dot(lb.astype(BF16), tri_ref[...], preferred_element_type=F32)
            e = z + lb + later + _lane_tile(carry_ref[g], tk)
            if diag_block is not None:
                e = jnp.where(strict, e, MASK_VALUE)
            w = jnp.exp(e)
            acc_ref[g] += jnp.dot(w.astype(BF16), vb, preferred_element_type=F32)
            carry_ref[g] += jnp.sum(lb, axis=1, keepdims=True)

    for u in reversed(range(sub)):
        step(i * sub + u, u)

    def body(t, carry):
        for u in reversed(range(sub)):
            step((i - 1 - t) * sub + u, None)
        return carry

    lax.fori_loop(0, i, body, 0)
    for g in range(heads):
        o_ref[:, g * LANES:(g + 1) * LANES] = acc_ref[g].astype(o_ref.dtype)


def _stickbreak(proj, *, first_col, tq=512, tk=256, heads=2):
    s = proj.shape[0]
    width = B_HEADS * B_HEAD_DIM
    hw = heads * LANES
    groups = B_HEADS // heads
    c0 = first_col // hw
    kern = functools.partial(_stickbreak_kernel, tq=tq, tk=tk, heads=heads,
                             scale=B_HEAD_DIM ** -0.5)
    return pl.pallas_call(
        kern,
        grid=(groups, s // tq),
        in_specs=[pl.BlockSpec((tq, hw), lambda h, i: (i, c0 + h)),
                  pl.BlockSpec((s, hw), lambda h, i: (0, c0 + groups + h)),
                  pl.BlockSpec((s, hw), lambda h, i: (0, c0 + 2 * groups + h))],
        out_specs=pl.BlockSpec((tq, hw), lambda h, i: (i, h)),
        out_shape=jax.ShapeDtypeStruct((s, width), BF16),
        scratch_shapes=[pltpu.VMEM((tk, tk), BF16),
                        pltpu.VMEM((heads, tq, LANES), F32),
                        pltpu.VMEM((heads, tq, LANES), F32)],
        compiler_params=_params(("arbitrary", "arbitrary"), 48),
        name="stickbreak",
    )(proj, proj, proj)


def _swa_kernel(sinks_ref, q_ref, kp_ref, kc_ref, vp_ref, vc_ref, o_ref):
    hk = pl.program_id(0)
    n = pl.program_id(1)
    w = WINDOW
    half = LANES // 2

    def block_diag(prev_ref, cur_ref):
        both = jnp.concatenate([prev_ref[...], cur_ref[...]], axis=0).astype(F32)
        lane = lax.broadcasted_iota(jnp.int32, both.shape, 1)
        low = lane < half
        return jnp.concatenate([jnp.where(low, both, 0.0), jnp.where(low, 0.0, both)],
                               axis=0).astype(BF16)

    kbd = block_diag(kp_ref, kc_ref)
    vbd = block_diag(vp_ref, vc_ref)

    qi = lax.broadcasted_iota(jnp.int32, (w, 2 * w), 0)
    ki = lax.broadcasted_iota(jnp.int32, (w, 2 * w), 1)
    first_real = jnp.where(n > 0, 0, w)
    mask = (ki > qi) & (ki <= qi + w) & (ki >= first_real)
    lane_o = lax.broadcasted_iota(jnp.int32, (w, LANES), 1)

    outs = []
    for p in range(C_GROUP // 2):
        qp = q_ref[:, p * LANES:(p + 1) * LANES]
        s = lax.dot_general(qp, kbd, NT_DIMS, preferred_element_type=F32)
        probs, inv_den = [], []
        for t in range(2):
            st = jnp.where(mask, s[:, t * 2 * w:(t + 1) * 2 * w], MASK_VALUE)
            sink = sinks_ref[hk * C_GROUP + 2 * p + t]
            m = jnp.maximum(jnp.max(st, axis=1, keepdims=True), sink)
            pt = jnp.exp(st - m)
            den = jnp.sum(pt, axis=1, keepdims=True) + jnp.exp(sink - m)
            probs.append(pt)
            inv_den.append(1.0 / den)
        pp = jnp.concatenate(probs, axis=1).astype(BF16)
        o = jnp.dot(pp, vbd, preferred_element_type=F32)
        outs.append(o * jnp.where(lane_o < half, inv_den[0], inv_den[1]))
    o_ref[...] = jnp.concatenate(outs, axis=1).astype(o_ref.dtype)


def _swa(proj, sinks):
    s = proj.shape[0]
    w = WINDOW
    qw = C_GROUP * C_HEAD_DIM
    k0 = C_Q_HEADS * C_HEAD_DIM // LANES
    v0 = k0 + C_KV_HEADS
    prev = lambda n: jnp.maximum(n - 1, 0)
    return pl.pallas_call(
        _swa_kernel,
        grid=(C_KV_HEADS, s // w),
        in_specs=[pl.BlockSpec(memory_space=pltpu.SMEM),
                  pl.BlockSpec((w, qw), lambda h, n: (n, h)),
                  pl.BlockSpec((w, LANES), lambda h, n: (prev(n), k0 + h)),
                  pl.BlockSpec((w, LANES), lambda h, n: (n, k0 + h)),
                  pl.BlockSpec((w, LANES), lambda h, n: (prev(n), v0 + h)),
                  pl.BlockSpec((w, LANES), lambda h, n: (n, v0 + h))],
        out_specs=pl.BlockSpec((w, qw), lambda h, n: (n, h)),
        out_shape=jax.ShapeDtypeStruct((s, C_Q_HEADS * C_HEAD_DIM), BF16),
        compiler_params=_params(("arbitrary", "arbitrary"), 16),
        name="swa_sink",
    )(sinks, proj, proj, proj, proj, proj)


def _out_proj_kernel(*refs, n_in):
    a_refs = refs[:n_in]
    w_ref, x_ref, o_ref, wb_ref = refs[n_in:]

    @pl.when(pl.program_id(0) == 0)
    def _():
        wb_ref[...] = w_ref[...].astype(BF16)

    acc = x_ref[...]
    row = 0
    for a_ref in a_refs:
        ka = a_ref.shape[1]
        acc = acc + jnp.dot(a_ref[...], wb_ref[row:row + ka, :], preferred_element_type=F32)
        row += ka
    o_ref[...] = acc


def _out_proj(acts, w, layer, x, *, tm=512):
    s, d = x.shape
    n_in = len(acts)
    kw = w.shape[1]
    a_specs = [pl.BlockSpec((tm, a.shape[1]), lambda i: (i, 0)) for a in acts]
    w_spec = pl.BlockSpec((None, kw, d), lambda i: (layer, 0, 0), pipeline_mode=pl.Buffered(1))
    return pl.pallas_call(
        functools.partial(_out_proj_kernel, n_in=n_in),
        grid=(s // tm,),
        in_specs=a_specs + [w_spec, pl.BlockSpec((tm, d), lambda i: (i, 0))],
        out_specs=pl.BlockSpec((tm, d), lambda i: (i, 0)),
        out_shape=jax.ShapeDtypeStruct((s, d), F32),
        scratch_shapes=[pltpu.VMEM((kw, d), BF16)],
        compiler_params=_params(("arbitrary",), 56),
        name="out_proj",
    )(*acts, w, x)


def _mlp_kernel(x_ref, g_ref, wu_ref, wd_ref, gf_ref, o_ref, xn_ref, *, final_norm):
    f = pl.program_id(1)

    @pl.when(f == 0)
    def _():
        x = x_ref[...]
        xn_ref[...] = _rms(x, g_ref[...]).astype(BF16)
        o_ref[...] = x

    hid = jnp.dot(xn_ref[...], wu_ref[...].astype(BF16), preferred_element_type=F32)
    hid = jnp.square(jnp.maximum(hid, 0.0)).astype(BF16)
    o_ref[...] += jnp.dot(hid, wd_ref[...].astype(BF16), preferred_element_type=F32)

    if final_norm:
        @pl.when(f == pl.num_programs(1) - 1)
        def _():
            o_ref[...] = _rms(o_ref[...], gf_ref[...])


def _mlp(x, g, w_up, w_down, layer, g_final, *, final_norm, tm=1024, tf=512):
    s, d = x.shape
    dff = w_up.shape[2]
    once = pl.Buffered(1)
    return pl.pallas_call(
        functools.partial(_mlp_kernel, final_norm=final_norm),
        grid=(s // tm, dff // tf),
        in_specs=[pl.BlockSpec((tm, d), lambda i, f: (i, 0), pipeline_mode=once),
                  pl.BlockSpec((1, d), lambda i, f: (0, 0)),
                  pl.BlockSpec((None, d, tf), lambda i, f: (layer, 0, f)),
                  pl.BlockSpec((None, tf, d), lambda i, f: (layer, f, 0)),
                  pl.BlockSpec((1, d), lambda i, f: (0, 0))],
        out_specs=pl.BlockSpec((tm, d), lambda i, f: (i, 0), pipeline_mode=once),
        out_shape=jax.ShapeDtypeStruct((s, d), F32),
        scratch_shapes=[pltpu.VMEM((tm, d), BF16)],
        compiler_params=_params(("arbitrary", "arbitrary"), 56),
        name="mlp",
    )(x, g.reshape(1, d), w_up, w_down, g_final.reshape(1, d))


def _dup_heads(w, n_heads, dim):
    d = w.shape[0]
    w = w.reshape(d, n_heads, 1, dim)
    return jnp.broadcast_to(w, (d, n_heads, 2, dim)).reshape(d, n_heads * 2 * dim)


def kernel(x, positions, norm_mix, norm_mlp, norm_final, w_in_ab, w_out_ab, lambda_q1, lambda_k1,
           lambda_q2, lambda_k2, diff_subln, w_in_c, w_out_c, sinks, w_up, w_down):
    b, s, d = x.shape
    assert b == 1
    depth = norm_mix.shape[0]
    h = x.reshape(s, d)
    cos, sin = _rope_tables(positions)

    a_width = A_HEADS * LANES
    c_q = C_Q_HEADS * C_HEAD_DIM
    c_kv = C_KV_HEADS * C_HEAD_DIM
    tn = 512
    for layer in range(depth):
        j = layer // 2
        if layer % 2 == 0:
            proj = _norm_proj(h, norm_mix[layer], w_in_ab, j, cos, sin,
                              n_rope=2 * a_width // tn, n_scaled=a_width // tn,
                              scale=ROPE_DIM ** -0.5 * math.log2(math.e), tn=tn)
            lam_init = 0.8 - 0.6 * math.exp(-0.3 * layer)
            lam_params = jnp.stack([lambda_q1[j], lambda_k1[j], lambda_q2[j], lambda_k2[j]])
            oa = _diff_attn(proj, lam_params, diff_subln[j], lam_init)
            ob = _stickbreak(proj, first_col=3 * a_width)
            h = _out_proj([oa, ob], w_out_ab, j, h)
        else:
            wc = w_in_c[j]
            wq, wk, wv = wc[:, :c_q], wc[:, c_q:c_q + c_kv], wc[:, c_q + c_kv:]
            wcat = jnp.concatenate([wq, _dup_heads(wk, C_KV_HEADS, C_HEAD_DIM),
                                    _dup_heads(wv, C_KV_HEADS, C_HEAD_DIM)], axis=1)
            proj = _norm_proj(h, norm_mix[layer], wcat[None], 0, cos, sin,
                              n_rope=(c_q + 2 * c_kv) // tn, n_scaled=c_q // tn,
                              scale=C_HEAD_DIM ** -0.5, tn=tn)
            oc = _swa(proj, sinks[j])
            h = _out_proj([oc], w_out_c, j, h)
        h = _mlp(h, norm_mlp[layer], w_up, w_down, layer,
                 norm_final, final_norm=(layer == depth - 1))
    return h.reshape(b, s, d)
```

```python
import functools
import math

import jax
import jax.numpy as jnp
from jax import lax
from jax.experimental import pallas as pl
from jax.experimental.pallas import tpu as pltpu

F32 = jnp.float32
BF16 = jnp.bfloat16

EPS = 1e-6
ROPE_THETA = 10000.0
LANES = 128
ROPE_DIM = 64
A_HEADS = 8
B_HEADS = 8
B_HEAD_DIM = 128
C_Q_HEADS = 32
C_KV_HEADS = 4
C_GROUP = C_Q_HEADS // C_KV_HEADS
C_HEAD_DIM = 64
WINDOW = 128
MASK_VALUE = -1e30
MIB = 1024 * 1024

NT_DIMS = (((1,), (1,)), ((), ()))


def _params(semantics, vmem_mib):
    return pltpu.CompilerParams(dimension_semantics=semantics,
                                vmem_limit_bytes=vmem_mib * MIB)


def _rms(x, g):
    ms = jnp.mean(x * x, axis=-1, keepdims=True)
    return x * lax.rsqrt(ms + EPS) * g


def _lane_tile(a, width):
    reps = width // LANES
    return a if reps == 1 else jnp.concatenate([a] * reps, axis=1)


def _head(ref, g, rows=slice(None)):
    return ref[rows, g * LANES:(g + 1) * LANES]


def _rope_table_kernel(pos_ref, invf_ref, cos_ref, sin_ref):
    ang = pos_ref[...].astype(F32) * invf_ref[...]
    lane = lax.broadcasted_iota(jnp.int32, ang.shape, 1)
    sign = jnp.where((lane & (ROPE_DIM // 2)) == 0, -1.0, 1.0)
    cos_ref[...] = jnp.cos(ang)
    sin_ref[...] = jnp.sin(ang) * sign


def _rope_tables(positions):
    s = positions.shape[-1]
    tm = 1024
    inv_freq = ROPE_THETA ** (-jnp.arange(0, ROPE_DIM, 2, dtype=F32) / ROPE_DIM)
    invf = jnp.tile(inv_freq, LANES // (ROPE_DIM // 2)).reshape(1, LANES)
    pos = positions.reshape(s, 1)
    return pl.pallas_call(
        _rope_table_kernel,
        grid=(s // tm,),
        in_specs=[pl.BlockSpec((tm, 1), lambda i: (i, 0)),
                  pl.BlockSpec((1, LANES), lambda i: (0, 0))],
        out_specs=[pl.BlockSpec((tm, LANES), lambda i: (i, 0))] * 2,
        out_shape=[jax.ShapeDtypeStruct((s, LANES), F32)] * 2,
        compiler_params=_params(("arbitrary",), 16),
        name="rope_tables",
    )(pos, invf)


def _rope(x, cos, ss):
    half = ROPE_DIM // 2
    lane = lax.broadcasted_iota(jnp.int32, cos.shape, 1)
    low = (lane & half) == 0
    outs = []
    for c in range(x.shape[1] // LANES):
        xc = x[:, c * LANES:(c + 1) * LANES]
        partner = jnp.where(low, pltpu.roll(xc, LANES - half, 1), pltpu.roll(xc, half, 1))
        outs.append(xc * cos + partner * ss)
    return outs[0] if len(outs) == 1 else jnp.concatenate(outs, axis=1)


def _norm_proj_kernel(x_ref, g_ref, w_ref, cos_ref, sin_ref, o_ref, xn_ref, *,
                      n_rope, n_scaled, scale):
    j = pl.program_id(1)

    @pl.when(j == 0)
    def _():
        xn_ref[...] = _rms(x_ref[...], g_ref[...]).astype(BF16)

    acc = jnp.dot(xn_ref[...], w_ref[...].astype(BF16), preferred_element_type=F32)

    @pl.when(j < n_rope)
    def _():
        r = _rope(acc, cos_ref[...], sin_ref[...])
        r = r * jnp.where(j < n_scaled, scale, 1.0)
        o_ref[...] = r.astype(o_ref.dtype)

    @pl.when(j >= n_rope)
    def _():
        o_ref[...] = acc.astype(o_ref.dtype)


def _norm_proj(x, g, w, layer, cos, sin, *, n_rope, n_scaled, scale, tm=1024, tn=512):
    s, d = x.shape
    n = w.shape[2]
    kern = functools.partial(_norm_proj_kernel, n_rope=n_rope, n_scaled=n_scaled, scale=scale)
    return pl.pallas_call(
        kern,
        grid=(s // tm, n // tn),
        in_specs=[pl.BlockSpec((tm, d), lambda i, j: (i, 0)),
                  pl.BlockSpec((1, d), lambda i, j: (0, 0)),
                  pl.BlockSpec((None, d, tn), lambda i, j: (layer, 0, j)),
                  pl.BlockSpec((tm, LANES), lambda i, j: (i, 0)),
                  pl.BlockSpec((tm, LANES), lambda i, j: (i, 0))],
        out_specs=pl.BlockSpec((tm, tn), lambda i, j: (i, j)),
        out_shape=jax.ShapeDtypeStruct((s, n), BF16),
        scratch_shapes=[pltpu.VMEM((tm, d), BF16)],
        compiler_params=_params(("arbitrary", "arbitrary"), 48),
        name="norm_proj",
    )(x, g.reshape(1, d), w, cos, sin)


def _diff_attn_kernel(lam_ref, subln_ref, q_ref, k_ref, v_ref, o_ref,
                      qs_ref, m_ref, l_ref, acc_ref, *, tq, tk, heads, lam_init):
    i = pl.program_id(1)
    rows = 2 * tq
    half = LANES // 2
    sub = tq // tk

    for g in range(heads):
        q = _head(q_ref, g).astype(F32)
        lane = lax.broadcasted_iota(jnp.int32, q.shape, 1)
        qs_ref[g, 0:tq, :] = jnp.where(lane < half, q, 0.0).astype(BF16)
        qs_ref[g, tq:rows, :] = jnp.where(lane >= half, q, 0.0).astype(BF16)
    m_ref[...] = jnp.full(m_ref.shape, MASK_VALUE, F32)
    l_ref[...] = jnp.zeros(l_ref.shape, F32)
    acc_ref[...] = jnp.zeros(acc_ref.shape, F32)

    def step(j, diag_block):
        start = pl.multiple_of(j * tk, tk)
        for g in range(heads):
            kb = _head(k_ref, g, pl.ds(start, tk))
            vb = _head(v_ref, g, pl.ds(start, tk))
            s = lax.dot_general(qs_ref[g], kb, NT_DIMS, preferred_element_type=F32)
            if diag_block is not None:
                row = lax.broadcasted_iota(jnp.int32, s.shape, 0)
                col = lax.broadcasted_iota(jnp.int32, s.shape, 1) + diag_block * tk
                row = jnp.where(row >= tq, row - tq, row)
                s = jnp.where(col <= row, s, MASK_VALUE)
            m_prev = m_ref[g]
            m_new = jnp.maximum(m_prev, jnp.max(s, axis=1, keepdims=True))
            alpha = jnp.exp2(m_prev - m_new)
            p = jnp.exp2(s - _lane_tile(m_new, tk))
            l_ref[g] = alpha * l_ref[g] + jnp.sum(p, axis=1, keepdims=True)
            acc_ref[g] = alpha * acc_ref[g] + jnp.dot(p.astype(BF16), vb,
                                                      preferred_element_type=F32)
            m_ref[g] = m_new

    def body(t, carry):
        for u in range(sub):
            step(t * sub + u, None)
        return carry

    lax.fori_loop(0, i, body, 0)
    for u in range(sub):
        step(i * sub + u, u)

    lp = lam_ref[...]
    lam = (jnp.exp(jnp.sum(lp[0:1] * lp[1:2], axis=1, keepdims=True))
           - jnp.exp(jnp.sum(lp[2:3] * lp[3:4], axis=1, keepdims=True)) + lam_init)
    for g in range(heads):
        o = acc_ref[g] / l_ref[g]
        d = o[0:tq] - lam * o[tq:rows]
        y = _rms(d, subln_ref[...]) * (1.0 - lam_init)
        o_ref[:, g * LANES:(g + 1) * LANES] = y.astype(o_ref.dtype)


def _diff_attn(proj, lam_params, subln, lam_init, *, tq=512, tk=512, heads=4):
    s = proj.shape[0]
    width = A_HEADS * LANES
    hw = heads * LANES
    groups = A_HEADS // heads
    kern = functools.partial(_diff_attn_kernel, tq=tq, tk=tk, heads=heads, lam_init=lam_init)
    return pl.pallas_call(
        kern,
        grid=(groups, s // tq),
        in_specs=[pl.BlockSpec((4, ROPE_DIM), lambda h, i: (0, 0)),
                  pl.BlockSpec((1, LANES), lambda h, i: (0, 0)),
                  pl.BlockSpec((tq, hw), lambda h, i: (i, h)),
                  pl.BlockSpec((s, hw), lambda h, i: (0, groups + h),
                               pipeline_mode=pl.Buffered(1)),
                  pl.BlockSpec((s, hw), lambda h, i: (0, 2 * groups + h),
                               pipeline_mode=pl.Buffered(1))],
        out_specs=pl.BlockSpec((tq, hw), lambda h, i: (i, h)),
        out_shape=jax.ShapeDtypeStruct((s, width), BF16),
        scratch_shapes=[pltpu.VMEM((heads, 2 * tq, LANES), BF16),
                        pltpu.VMEM((heads, 2 * tq, LANES), F32),
                        pltpu.VMEM((heads, 2 * tq, LANES), F32),
                        pltpu.VMEM((heads, 2 * tq, LANES), F32)],
        compiler_params=_params(("arbitrary", "arbitrary"), 48),
        name="diff_attn",
    )(lam_params, subln.reshape(1, LANES), proj, proj, proj)


def _stickbreak_kernel(q_ref, k_ref, v_ref, o_ref, tri_ref, carry_ref, acc_ref, *,
                       tq, tk, heads, scale):
    h = pl.program_id(0)
    i = pl.program_id(1)
    sub = tq // tk

    @pl.when((h == 0) & (i == 0))
    def _():
        r = lax.broadcasted_iota(jnp.int32, tri_ref.shape, 0)
        c = lax.broadcasted_iota(jnp.int32, tri_ref.shape, 1)
        tri_ref[...] = jnp.where(r > c, 1.0, 0.0).astype(BF16)

    carry_ref[...] = jnp.zeros(carry_ref.shape, F32)
    acc_ref[...] = jnp.zeros(acc_ref.shape, F32)

    def step(j, diag_block):
        start = pl.multiple_of(j * tk, tk)
        for g in range(heads):
            kb = _head(k_ref, g, pl.ds(start, tk))
            vb = _head(v_ref, g, pl.ds(start, tk))
            qk = lax.dot_general(_head(q_ref, g), kb, NT_DIMS, preferred_element_type=F32)
            z = qk * scale
            nz = qk * (-scale)
            lb = jnp.minimum(nz, 0.0) - jnp.log2(1.0 + jnp.exp2(jnp.minimum(z, nz)))
            if diag_block is not None:
                row = lax.broadcasted_iota(jnp.int32, z.shape, 0)
                col = lax.broadcasted_iota(jnp.int32, z.shape, 1) + diag_block * tk
                strict = col < row
                lb = jnp.where(strict, lb, 0.0)
            later = jnp.dot(lb.astype(BF16), tri_ref[...], preferred_element_type=F32)
            e = z + lb + later + _lane_tile(carry_ref[g], tk)
            if diag_block is not None:
                e = jnp.where(strict, e, MASK_VALUE)
            w = jnp.exp2(e)
            acc_ref[g] += jnp.dot(w.astype(BF16), vb, preferred_element_type=F32)
            carry_ref[g] += jnp.sum(lb, axis=1, keepdims=True)

    for u in reversed(range(sub)):
        step(i * sub + u, u)

    def body(t, carry):
        for u in reversed(range(sub)):
            step((i - 1 - t) * sub + u, None)
        return carry

    lax.fori_loop(0, i, body, 0)
    for g in range(heads):
        o_ref[:, g * LANES:(g + 1) * LANES] = acc_ref[g].astype(o_ref.dtype)


def _stickbreak(proj, *, first_col, tq=512, tk=256, heads=4):
    s = proj.shape[0]
    width = B_HEADS * B_HEAD_DIM
    hw = heads * LANES
    groups = B_HEADS // heads
    c0 = first_col // hw
    kern = functools.partial(_stickbreak_kernel, tq=tq, tk=tk, heads=heads,
                             scale=B_HEAD_DIM ** -0.5 * math.log2(math.e))
    return pl.pallas_call(
        kern,
        grid=(groups, s // tq),
        in_specs=[pl.BlockSpec((tq, hw), lambda h, i: (i, c0 + h)),
                  pl.BlockSpec((s, hw), lambda h, i: (0, c0 + groups + h),
                               pipeline_mode=pl.Buffered(1)),
                  pl.BlockSpec((s, hw), lambda h, i: (0, c0 + 2 * groups + h),
                               pipeline_mode=pl.Buffered(1))],
        out_specs=pl.BlockSpec((tq, hw), lambda h, i: (i, h)),
        out_shape=jax.ShapeDtypeStruct((s, width), BF16),
        scratch_shapes=[pltpu.VMEM((tk, tk), BF16),
                        pltpu.VMEM((heads, tq, LANES), F32),
                        pltpu.VMEM((heads, tq, LANES), F32)],
        compiler_params=_params(("arbitrary", "arbitrary"), 48),
        name="stickbreak",
    )(proj, proj, proj)


def _swa_kernel(sinks_ref, q_ref, kp_ref, kc_ref, vp_ref, vc_ref, o_ref, *, blocks):
    hk = pl.program_id(0)
    n = pl.program_id(1)
    w = WINDOW
    half = LANES // 2

    def lane_halves(prev_ref, cur_ref):
        both = jnp.concatenate([prev_ref[...], cur_ref[...]], axis=0).astype(F32)
        low = lax.broadcasted_iota(jnp.int32, both.shape, 1) < half
        return jnp.where(low, both, 0.0).astype(BF16), jnp.where(low, 0.0, both).astype(BF16)

    k_lo, k_hi = lane_halves(kp_ref, kc_ref)
    v_lo, v_hi = lane_halves(vp_ref, vc_ref)

    qi = lax.broadcasted_iota(jnp.int32, (w, 2 * w), 0)
    ki = lax.broadcasted_iota(jnp.int32, (w, 2 * w), 1)
    band = (ki > qi) & (ki <= qi + w)
    first_real = jnp.where(n > 0, 0, w)
    band_first = band & (ki >= first_real)
    lane_o = lax.broadcasted_iota(jnp.int32, (w, LANES), 1)
    log2e = math.log2(math.e)

    for b in range(blocks):
        keys = slice(b * w, (b + 2) * w)
        kbd = jnp.concatenate([k_lo[keys], k_hi[keys]], axis=0)
        vbd = jnp.concatenate([v_lo[keys], v_hi[keys]], axis=0)
        mask = band_first if b == 0 else band
        pairs = C_GROUP // 2
        qs = jnp.concatenate([q_ref[b * w:(b + 1) * w, p * LANES:(p + 1) * LANES]
                              for p in range(pairs)], axis=0)
        s = lax.dot_general(qs, kbd, NT_DIMS, preferred_element_type=F32)
        probs, scales = [], []
        for p in range(pairs):
            row_p, inv_den = [], []
            for t in range(2):
                st = jnp.where(mask, s[p * w:(p + 1) * w, t * 2 * w:(t + 1) * 2 * w], MASK_VALUE)
                sink = sinks_ref[hk * C_GROUP + 2 * p + t] * log2e
                m = jnp.maximum(jnp.max(st, axis=1, keepdims=True), sink)
                pt = jnp.exp2(st - m)
                den = jnp.sum(pt, axis=1, keepdims=True) + jnp.exp2(sink - m)
                row_p.append(pt.astype(BF16))
                inv_den.append(1.0 / den)
            probs.append(jnp.concatenate(row_p, axis=1))
            scales.append(jnp.where(lane_o < half, inv_den[0], inv_den[1]))
        o = jnp.dot(jnp.concatenate(probs, axis=0), vbd, preferred_element_type=F32)
        outs = [o[p * w:(p + 1) * w] * scales[p] for p in range(pairs)]
        o_ref[b * w:(b + 1) * w, :] = jnp.concatenate(outs, axis=1).astype(o_ref.dtype)


def _swa(proj, sinks, *, blocks=4):
    s = proj.shape[0]
    w = WINDOW
    tq = blocks * w
    qw = C_GROUP * C_HEAD_DIM
    k0 = C_Q_HEADS * C_HEAD_DIM // LANES
    v0 = k0 + C_KV_HEADS
    prev = lambda n: jnp.maximum(n * blocks - 1, 0)
    return pl.pallas_call(
        functools.partial(_swa_kernel, blocks=blocks),
        grid=(C_KV_HEADS, s // tq),
        in_specs=[pl.BlockSpec(memory_space=pltpu.SMEM),
                  pl.BlockSpec((tq, qw), lambda h, n: (n, h)),
                  pl.BlockSpec((w, LANES), lambda h, n: (prev(n), k0 + h)),
                  pl.BlockSpec((tq, LANES), lambda h, n: (n, k0 + h)),
                  pl.BlockSpec((w, LANES), lambda h, n: (prev(n), v0 + h)),
                  pl.BlockSpec((tq, LANES), lambda h, n: (n, v0 + h))],
        out_specs=pl.BlockSpec((tq, qw), lambda h, n: (n, h)),
        out_shape=jax.ShapeDtypeStruct((s, C_Q_HEADS * C_HEAD_DIM), BF16),
        compiler_params=_params(("arbitrary", "arbitrary"), 16),
        name="swa_sink",
    )(sinks, proj, proj, proj, proj, proj)


def _out_proj_kernel(*refs, n_in):
    a_refs = refs[:n_in]
    w_ref, x_ref, o_ref, wb_ref = refs[n_in:]

    @pl.when(pl.program_id(0) == 0)
    def _():
        wb_ref[...] = w_ref[...].astype(BF16)

    acc = x_ref[...]
    row = 0
    for a_ref in a_refs:
        ka = a_ref.shape[1]
        acc = acc + jnp.dot(a_ref[...], wb_ref[row:row + ka, :], preferred_element_type=F32)
        row += ka
    o_ref[...] = acc


def _out_proj(acts, w, layer, x, *, tm=512):
    s, d = x.shape
    n_in = len(acts)
    kw = w.shape[1]
    a_specs = [pl.BlockSpec((tm, a.shape[1]), lambda i: (i, 0)) for a in acts]
    w_spec = pl.BlockSpec((None, kw, d), lambda i: (layer, 0, 0), pipeline_mode=pl.Buffered(1))
    return pl.pallas_call(
        functools.partial(_out_proj_kernel, n_in=n_in),
        grid=(s // tm,),
        in_specs=a_specs + [w_spec, pl.BlockSpec((tm, d), lambda i: (i, 0))],
        out_specs=pl.BlockSpec((tm, d), lambda i: (i, 0)),
        out_shape=jax.ShapeDtypeStruct((s, d), F32),
        scratch_shapes=[pltpu.VMEM((kw, d), BF16)],
        compiler_params=_params(("arbitrary",), 56),
        name="out_proj",
    )(*acts, w, x)


def _mlp_kernel(x_ref, g_ref, wu_ref, wd_ref, gf_ref, o_ref, xn_ref, *, final_norm):
    f = pl.program_id(1)

    @pl.when(f == 0)
    def _():
        x = x_ref[...]
        xn_ref[...] = _rms(x, g_ref[...]).astype(BF16)
        o_ref[...] = x

    hid = jnp.dot(xn_ref[...], wu_ref[...].astype(BF16), preferred_element_type=F32)
    hid = jnp.square(jnp.maximum(hid, 0.0)).astype(BF16)
    o_ref[...] += jnp.dot(hid, wd_ref[...].astype(BF16), preferred_element_type=F32)

    if final_norm:
        @pl.when(f == pl.num_programs(1) - 1)
        def _():
            o_ref[...] = _rms(o_ref[...], gf_ref[...])


def _mlp(x, g, w_up, w_down, layer, g_final, *, final_norm, tm=1024, tf=512):
    s, d = x.shape
    dff = w_up.shape[2]
    once = pl.Buffered(1)
    return pl.pallas_call(
        functools.partial(_mlp_kernel, final_norm=final_norm),
        grid=(s // tm, dff // tf),
        in_specs=[pl.BlockSpec((tm, d), lambda i, f: (i, 0), pipeline_mode=once),
                  pl.BlockSpec((1, d), lambda i, f: (0, 0)),
                  pl.BlockSpec((None, d, tf), lambda i, f: (layer, 0, f)),
                  pl.BlockSpec((None, tf, d), lambda i, f: (layer, f, 0)),
                  pl.BlockSpec((1, d), lambda i, f: (0, 0))],
        out_specs=pl.BlockSpec((tm, d), lambda i, f: (i, 0), pipeline_mode=once),
        out_shape=jax.ShapeDtypeStruct((s, d), F32),
        scratch_shapes=[pltpu.VMEM((tm, d), BF16)],
        compiler_params=_params(("arbitrary", "arbitrary"), 56),
        name="mlp",
    )(x, g.reshape(1, d), w_up, w_down, g_final.reshape(1, d))


def _dup_heads(w, n_heads, dim):
    d = w.shape[0]
    w = w.reshape(d, n_heads, 1, dim)
    return jnp.broadcast_to(w, (d, n_heads, 2, dim)).reshape(d, n_heads * 2 * dim)


def kernel(x, positions, norm_mix, norm_mlp, norm_final, w_in_ab, w_out_ab, lambda_q1, lambda_k1,
           lambda_q2, lambda_k2, diff_subln, w_in_c, w_out_c, sinks, w_up, w_down):
    b, s, d = x.shape
    assert b == 1
    depth = norm_mix.shape[0]
    h = x.reshape(s, d)
    cos, sin = _rope_tables(positions)

    a_width = A_HEADS * LANES
    c_q = C_Q_HEADS * C_HEAD_DIM
    c_kv = C_KV_HEADS * C_HEAD_DIM
    tn = 512
    for layer in range(depth):
        j = layer // 2
        if layer % 2 == 0:
            proj = _norm_proj(h, norm_mix[layer], w_in_ab, j, cos, sin,
                              n_rope=2 * a_width // tn, n_scaled=a_width // tn,
                              scale=ROPE_DIM ** -0.5 * math.log2(math.e), tn=tn)
            lam_init = 0.8 - 0.6 * math.exp(-0.3 * layer)
            lam_params = jnp.stack([lambda_q1[j], lambda_k1[j], lambda_q2[j], lambda_k2[j]])
            oa = _diff_attn(proj, lam_params, diff_subln[j], lam_init)
            ob = _stickbreak(proj, first_col=3 * a_width)
            h = _out_proj([oa, ob], w_out_ab, j, h)
        else:
            wc = w_in_c[j]
            wq, wk, wv = wc[:, :c_q], wc[:, c_q:c_q + c_kv], wc[:, c_q + c_kv:]
            wcat = jnp.concatenate([wq, _dup_heads(wk, C_KV_HEADS, C_HEAD_DIM),
                                    _dup_heads(wv, C_KV_HEADS, C_HEAD_DIM)], axis=1)
            proj = _norm_proj(h, norm_mix[layer], wcat[None], 0, cos, sin,
                              n_rope=(c_q + 2 * c_kv) // tn, n_scaled=c_q // tn,
                              scale=C_HEAD_DIM ** -0.5 * math.log2(math.e), tn=tn)
            oc = _swa(proj, sinks[j])
            h = _out_proj([oc], w_out_c, j, h)
        h = _mlp(h, norm_mlp[layer], w_up, w_down, layer,
                 norm_final, final_norm=(layer == depth - 1))
    return h.reshape(b, s, d)
```

```python
import functools
import math

import jax
import jax.numpy as jnp
from jax import lax
from jax.experimental import pallas as pl
from jax.experimental.pallas import tpu as pltpu

F32 = jnp.float32
BF16 = jnp.bfloat16

EPS = 1e-6
ROPE_THETA = 10000.0
LANES = 128
ROPE_DIM = 64
A_HEADS = 8
B_HEADS = 8
B_HEAD_DIM = 128
C_Q_HEADS = 32
C_KV_HEADS = 4
C_GROUP = C_Q_HEADS // C_KV_HEADS
C_HEAD_DIM = 64
WINDOW = 128
MASK_VALUE = -1e30
EXP2_UNDERFLOW = -160.0
MIB = 1024 * 1024

NT_DIMS = (((1,), (1,)), ((), ()))


def _params(semantics, vmem_mib):
    return pltpu.CompilerParams(dimension_semantics=semantics,
                                vmem_limit_bytes=vmem_mib * MIB)


def _rms(x, g):
    ms = jnp.mean(x * x, axis=-1, keepdims=True)
    return x * lax.rsqrt(ms + EPS) * g


def _lane_tile(a, width):
    reps = width // LANES
    return a if reps == 1 else jnp.concatenate([a] * reps, axis=1)


def _head(ref, g, rows=slice(None)):
    return ref[rows, g * LANES:(g + 1) * LANES]


def _rope_table_kernel(pos_ref, invf_ref, cos_ref, sin_ref):
    ang = pos_ref[...].astype(F32) * invf_ref[...]
    lane = lax.broadcasted_iota(jnp.int32, ang.shape, 1)
    sign = jnp.where((lane & (ROPE_DIM // 2)) == 0, -1.0, 1.0)
    cos_ref[...] = jnp.cos(ang)
    sin_ref[...] = jnp.sin(ang) * sign


def _rope_tables(positions):
    s = positions.shape[-1]
    tm = 1024
    inv_freq = ROPE_THETA ** (-jnp.arange(0, ROPE_DIM, 2, dtype=F32) / ROPE_DIM)
    invf = jnp.tile(inv_freq, LANES // (ROPE_DIM // 2)).reshape(1, LANES)
    pos = positions.reshape(s, 1)
    return pl.pallas_call(
        _rope_table_kernel,
        grid=(s // tm,),
        in_specs=[pl.BlockSpec((tm, 1), lambda i: (i, 0)),
                  pl.BlockSpec((1, LANES), lambda i: (0, 0))],
        out_specs=[pl.BlockSpec((tm, LANES), lambda i: (i, 0))] * 2,
        out_shape=[jax.ShapeDtypeStruct((s, LANES), F32)] * 2,
        compiler_params=_params(("arbitrary",), 16),
        name="rope_tables",
    )(pos, invf)


def _rope(x, cos, ss):
    half = ROPE_DIM // 2
    lane = lax.broadcasted_iota(jnp.int32, cos.shape, 1)
    low = (lane & half) == 0
    outs = []
    for c in range(x.shape[1] // LANES):
        xc = x[:, c * LANES:(c + 1) * LANES]
        partner = jnp.where(low, pltpu.roll(xc, LANES - half, 1), pltpu.roll(xc, half, 1))
        outs.append(xc * cos + partner * ss)
    return outs[0] if len(outs) == 1 else jnp.concatenate(outs, axis=1)


def _norm_proj_kernel(x_ref, g_ref, w_ref, cos_ref, sin_ref, o_ref, xn_ref, *,
                      n_rope, n_scaled, scale):
    j = pl.program_id(1)

    @pl.when(j == 0)
    def _():
        xn_ref[...] = _rms(x_ref[...], g_ref[...]).astype(BF16)

    acc = jnp.dot(xn_ref[...], w_ref[...].astype(BF16), preferred_element_type=F32)

    @pl.when(j < n_rope)
    def _():
        r = _rope(acc, cos_ref[...], sin_ref[...])
        r = r * jnp.where(j < n_scaled, scale, 1.0)
        o_ref[...] = r.astype(o_ref.dtype)

    @pl.when(j >= n_rope)
    def _():
        o_ref[...] = acc.astype(o_ref.dtype)


def _norm_proj(x, g, w, layer, cos, sin, *, n_rope, n_scaled, scale, tm=1024, tn=512):
    s, d = x.shape
    n = w.shape[2]
    kern = functools.partial(_norm_proj_kernel, n_rope=n_rope, n_scaled=n_scaled, scale=scale)
    return pl.pallas_call(
        kern,
        grid=(s // tm, n // tn),
        in_specs=[pl.BlockSpec((tm, d), lambda i, j: (i, 0)),
                  pl.BlockSpec((1, d), lambda i, j: (0, 0)),
                  pl.BlockSpec((None, d, tn), lambda i, j: (layer, 0, j)),
                  pl.BlockSpec((tm, LANES), lambda i, j: (i, 0)),
                  pl.BlockSpec((tm, LANES), lambda i, j: (i, 0))],
        out_specs=pl.BlockSpec((tm, tn), lambda i, j: (i, j)),
        out_shape=jax.ShapeDtypeStruct((s, n), BF16),
        scratch_shapes=[pltpu.VMEM((tm, d), BF16)],
        compiler_params=_params(("arbitrary", "arbitrary"), 48),
        name="norm_proj",
    )(x, g.reshape(1, d), w, cos, sin)


def _diff_attn_kernel(lam_ref, subln_ref, q_ref, k_ref, v_ref, o_ref,
                      qs_ref, m_ref, l_ref, acc_ref, *, tq, tk, heads, lam_init):
    i = pl.program_id(1)
    rows = 2 * tq
    half = LANES // 2
    sub = tq // tk

    for g in range(heads):
        q = _head(q_ref, g).astype(F32)
        lane = lax.broadcasted_iota(jnp.int32, q.shape, 1)
        qs_ref[g, 0:tq, :] = jnp.where(lane < half, q, 0.0).astype(BF16)
        qs_ref[g, tq:rows, :] = jnp.where(lane >= half, q, 0.0).astype(BF16)
    m_ref[...] = jnp.full(m_ref.shape, MASK_VALUE, F32)
    l_ref[...] = jnp.zeros(l_ref.shape, F32)
    acc_ref[...] = jnp.zeros(acc_ref.shape, F32)

    def step(j, diag_block):
        start = pl.multiple_of(j * tk, tk)
        for g in range(heads):
            kb = _head(k_ref, g, pl.ds(start, tk))
            vb = _head(v_ref, g, pl.ds(start, tk))
            s = lax.dot_general(qs_ref[g], kb, NT_DIMS, preferred_element_type=F32)
            if diag_block is not None:
                row = lax.broadcasted_iota(jnp.int32, s.shape, 0)
                col = lax.broadcasted_iota(jnp.int32, s.shape, 1) + diag_block * tk
                row = jnp.where(row >= tq, row - tq, row)
                s = jnp.where(col <= row, s, MASK_VALUE)
            m_prev = m_ref[g]
            m_new = jnp.maximum(m_prev, jnp.max(s, axis=1, keepdims=True))
            alpha = jnp.exp2(m_prev - m_new)
            p = jnp.exp2(s - _lane_tile(m_new, tk))
            l_ref[g] = alpha * l_ref[g] + jnp.sum(p, axis=1, keepdims=True)
            acc_ref[g] = alpha * acc_ref[g] + jnp.dot(p.astype(BF16), vb,
                                                      preferred_element_type=F32)
            m_ref[g] = m_new

    def body(t, carry):
        for u in range(sub):
            step(t * sub + u, None)
        return carry

    lax.fori_loop(0, i, body, 0)
    for u in range(sub):
        step(i * sub + u, u)

    lp = lam_ref[...]
    lam = (jnp.exp(jnp.sum(lp[0:1] * lp[1:2], axis=1, keepdims=True))
           - jnp.exp(jnp.sum(lp[2:3] * lp[3:4], axis=1, keepdims=True)) + lam_init)
    for g in range(heads):
        o = acc_ref[g] / l_ref[g]
        d = o[0:tq] - lam * o[tq:rows]
        y = _rms(d, subln_ref[...]) * (1.0 - lam_init)
        o_ref[:, g * LANES:(g + 1) * LANES] = y.astype(o_ref.dtype)


def _diff_attn(proj, lam_params, subln, lam_init, *, tq=512, tk=512, heads=4):
    s = proj.shape[0]
    width = A_HEADS * LANES
    hw = heads * LANES
    groups = A_HEADS // heads
    kern = functools.partial(_diff_attn_kernel, tq=tq, tk=tk, heads=heads, lam_init=lam_init)
    return pl.pallas_call(
        kern,
        grid=(groups, s // tq),
        in_specs=[pl.BlockSpec((4, ROPE_DIM), lambda h, i: (0, 0)),
                  pl.BlockSpec((1, LANES), lambda h, i: (0, 0)),
                  pl.BlockSpec((tq, hw), lambda h, i: (i, h)),
                  pl.BlockSpec((s, hw), lambda h, i: (0, groups + h),
                               pipeline_mode=pl.Buffered(1)),
                  pl.BlockSpec((s, hw), lambda h, i: (0, 2 * groups + h),
                               pipeline_mode=pl.Buffered(1))],
        out_specs=pl.BlockSpec((tq, hw), lambda h, i: (i, h)),
        out_shape=jax.ShapeDtypeStruct((s, width), BF16),
        scratch_shapes=[pltpu.VMEM((heads, 2 * tq, LANES), BF16),
                        pltpu.VMEM((heads, 2 * tq, LANES), F32),
                        pltpu.VMEM((heads, 2 * tq, LANES), F32),
                        pltpu.VMEM((heads, 2 * tq, LANES), F32)],
        compiler_params=_params(("arbitrary", "arbitrary"), 48),
        name="diff_attn",
    )(lam_params, subln.reshape(1, LANES), proj, proj, proj)


def _stickbreak_kernel(q_ref, k_ref, v_ref, o_ref, tri_ref, carry_ref, acc_ref, *,
                       tq, tk, heads, scale):
    h = pl.program_id(0)
    i = pl.program_id(1)
    sub = tq // tk

    @pl.when((h == 0) & (i == 0))
    def _():
        r = lax.broadcasted_iota(jnp.int32, tri_ref.shape, 0)
        c = lax.broadcasted_iota(jnp.int32, tri_ref.shape, 1)
        tri_ref[...] = jnp.where(r > c, 1.0, 0.0).astype(BF16)

    carry_ref[...] = jnp.zeros(carry_ref.shape, F32)
    acc_ref[...] = jnp.zeros(acc_ref.shape, F32)

    def step(j, diag_block):
        start = pl.multiple_of(j * tk, tk)
        for g in range(heads):
            kb = _head(k_ref, g, pl.ds(start, tk))
            vb = _head(v_ref, g, pl.ds(start, tk))
            qk = lax.dot_general(_head(q_ref, g), kb, NT_DIMS, preferred_element_type=F32)
            z = qk * scale
            nz = qk * (-scale)
            lb = jnp.minimum(nz, 0.0) - jnp.log2(1.0 + jnp.exp2(jnp.minimum(z, nz)))
            if diag_block is not None:
                row = lax.broadcasted_iota(jnp.int32, z.shape, 0)
                col = lax.broadcasted_iota(jnp.int32, z.shape, 1) + diag_block * tk
                strict = col < row
                lb = jnp.where(strict, lb, 0.0)
            later = jnp.dot(lb.astype(BF16), tri_ref[...], preferred_element_type=F32)
            e = z + lb + later + _lane_tile(carry_ref[g], tk)
            if diag_block is not None:
                e = jnp.where(strict, e, MASK_VALUE)
            w = jnp.exp2(e)
            acc_ref[g] += jnp.dot(w.astype(BF16), vb, preferred_element_type=F32)
            carry_ref[g] += jnp.sum(lb, axis=1, keepdims=True)

    for u in reversed(range(sub)):
        step(i * sub + u, u)

    def more(state):
        j, top = state
        return (j >= 0) & (top > EXP2_UNDERFLOW)

    def body(state):
        j, _ = state
        step(j, None)
        return j - 1, jnp.max(carry_ref[...])

    lax.while_loop(more, body, (i * sub - 1, jnp.float32(0.0)))
    for g in range(heads):
        o_ref[:, g * LANES:(g + 1) * LANES] = acc_ref[g].astype(o_ref.dtype)


def _stickbreak(proj, *, first_col, tq=512, tk=256, heads=4):
    s = proj.shape[0]
    width = B_HEADS * B_HEAD_DIM
    hw = heads * LANES
    groups = B_HEADS // heads
    c0 = first_col // hw
    kern = functools.partial(_stickbreak_kernel, tq=tq, tk=tk, heads=heads,
                             scale=B_HEAD_DIM ** -0.5 * math.log2(math.e))
    return pl.pallas_call(
        kern,
        grid=(groups, s // tq),
        in_specs=[pl.BlockSpec((tq, hw), lambda h, i: (i, c0 + h)),
                  pl.BlockSpec((s, hw), lambda h, i: (0, c0 + groups + h),
                               pipeline_mode=pl.Buffered(1)),
                  pl.BlockSpec((s, hw), lambda h, i: (0, c0 + 2 * groups + h),
                               pipeline_mode=pl.Buffered(1))],
        out_specs=pl.BlockSpec((tq, hw), lambda h, i: (i, h)),
        out_shape=jax.ShapeDtypeStruct((s, width), BF16),
        scratch_shapes=[pltpu.VMEM((tk, tk), BF16),
                        pltpu.VMEM((heads, tq, LANES), F32),
                        pltpu.VMEM((heads, tq, LANES), F32)],
        compiler_params=_params(("arbitrary", "arbitrary"), 48),
        name="stickbreak",
    )(proj, proj, proj)


def _swa_kernel(sinks_ref, q_ref, kp_ref, kc_ref, vp_ref, vc_ref, o_ref, *, blocks):
    hk = pl.program_id(0)
    n = pl.program_id(1)
    w = WINDOW
    half = LANES // 2

    def lane_halves(prev_ref, cur_ref):
        both = jnp.concatenate([prev_ref[...], cur_ref[...]], axis=0).astype(F32)
        low = lax.broadcasted_iota(jnp.int32, both.shape, 1) < half
        return jnp.where(low, both, 0.0).astype(BF16), jnp.where(low, 0.0, both).astype(BF16)

    k_lo, k_hi = lane_halves(kp_ref, kc_ref)
    v_lo, v_hi = lane_halves(vp_ref, vc_ref)

    qi = lax.broadcasted_iota(jnp.int32, (w, 2 * w), 0)
    ki = lax.broadcasted_iota(jnp.int32, (w, 2 * w), 1)
    band = (ki > qi) & (ki <= qi + w)
    first_real = jnp.where(n > 0, 0, w)
    band_first = band & (ki >= first_real)
    lane_o = lax.broadcasted_iota(jnp.int32, (w, LANES), 1)
    log2e = math.log2(math.e)

    for b in range(blocks):
        keys = slice(b * w, (b + 2) * w)
        kbd = jnp.concatenate([k_lo[keys], k_hi[keys]], axis=0)
        vbd = jnp.concatenate([v_lo[keys], v_hi[keys]], axis=0)
        mask = band_first if b == 0 else band
        pairs = C_GROUP // 2
        qs = jnp.concatenate([q_ref[b * w:(b + 1) * w, p * LANES:(p + 1) * LANES]
                              for p in range(pairs)], axis=0)
        s = lax.dot_general(qs, kbd, NT_DIMS, preferred_element_type=F32)
        probs, scales = [], []
        for p in range(pairs):
            row_p, inv_den = [], []
            for t in range(2):
                st = jnp.where(mask, s[p * w:(p + 1) * w, t * 2 * w:(t + 1) * 2 * w], MASK_VALUE)
                sink = sinks_ref[hk * C_GROUP + 2 * p + t] * log2e
                m = jnp.maximum(jnp.max(st, axis=1, keepdims=True), sink)
                pt = jnp.exp2(st - m)
                den = jnp.sum(pt, axis=1, keepdims=True) + jnp.exp2(sink - m)
                row_p.append(pt.astype(BF16))
                inv_den.append(1.0 / den)
            probs.append(jnp.concatenate(row_p, axis=1))
            scales.append(jnp.where(lane_o < half, inv_den[0], inv_den[1]))
        o = jnp.dot(jnp.concatenate(probs, axis=0), vbd, preferred_element_type=F32)
        outs = [o[p * w:(p + 1) * w] * scales[p] for p in range(pairs)]
        o_ref[b * w:(b + 1) * w, :] = jnp.concatenate(outs, axis=1).astype(o_ref.dtype)


def _swa(proj, sinks, *, blocks=4):
    s = proj.shape[0]
    w = WINDOW
    tq = blocks * w
    qw = C_GROUP * C_HEAD_DIM
    k0 = C_Q_HEADS * C_HEAD_DIM // LANES
    v0 = k0 + C_KV_HEADS
    prev = lambda n: jnp.maximum(n * blocks - 1, 0)
    return pl.pallas_call(
        functools.partial(_swa_kernel, blocks=blocks),
        grid=(C_KV_HEADS, s // tq),
        in_specs=[pl.BlockSpec(memory_space=pltpu.SMEM),
                  pl.BlockSpec((tq, qw), lambda h, n: (n, h)),
                  pl.BlockSpec((w, LANES), lambda h, n: (prev(n), k0 + h)),
                  pl.BlockSpec((tq, LANES), lambda h, n: (n, k0 + h)),
                  pl.BlockSpec((w, LANES), lambda h, n: (prev(n), v0 + h)),
                  pl.BlockSpec((tq, LANES), lambda h, n: (n, v0 + h))],
        out_specs=pl.BlockSpec((tq, qw), lambda h, n: (n, h)),
        out_shape=jax.ShapeDtypeStruct((s, C_Q_HEADS * C_HEAD_DIM), BF16),
        compiler_params=_params(("arbitrary", "arbitrary"), 16),
        name="swa_sink",
    )(sinks, proj, proj, proj, proj, proj)


def _out_proj_kernel(*refs, n_in):
    a_refs = refs[:n_in]
    w_ref, x_ref, o_ref, wb_ref = refs[n_in:]

    @pl.when(pl.program_id(0) == 0)
    def _():
        wb_ref[...] = w_ref[...].astype(BF16)

    acc = x_ref[...]
    row = 0
    for a_ref in a_refs:
        ka = a_ref.shape[1]
        acc = acc + jnp.dot(a_ref[...], wb_ref[row:row + ka, :], preferred_element_type=F32)
        row += ka
    o_ref[...] = acc


def _out_proj(acts, w, layer, x, *, tm=512):
    s, d = x.shape
    n_in = len(acts)
    kw = w.shape[1]
    a_specs = [pl.BlockSpec((tm, a.shape[1]), lambda i: (i, 0)) for a in acts]
    w_spec = pl.BlockSpec((None, kw, d), lambda i: (layer, 0, 0), pipeline_mode=pl.Buffered(1))
    return pl.pallas_call(
        functools.partial(_out_proj_kernel, n_in=n_in),
        grid=(s // tm,),
        in_specs=a_specs + [w_spec, pl.BlockSpec((tm, d), lambda i: (i, 0))],
        out_specs=pl.BlockSpec((tm, d), lambda i: (i, 0)),
        out_shape=jax.ShapeDtypeStruct((s, d), F32),
        scratch_shapes=[pltpu.VMEM((kw, d), BF16)],
        compiler_params=_params(("arbitrary",), 56),
        name="out_proj",
    )(*acts, w, x)


def _mlp_kernel(x_ref, g_ref, wu_ref, wd_ref, gf_ref, o_ref, xn_ref, *, final_norm):
    f = pl.program_id(1)

    @pl.when(f == 0)
    def _():
        x = x_ref[...]
        xn_ref[...] = _rms(x, g_ref[...]).astype(BF16)
        o_ref[...] = x

    hid = jnp.dot(xn_ref[...], wu_ref[...].astype(BF16), preferred_element_type=F32)
    hid = jnp.square(jnp.maximum(hid, 0.0)).astype(BF16)
    o_ref[...] += jnp.dot(hid, wd_ref[...].astype(BF16), preferred_element_type=F32)

    if final_norm:
        @pl.when(f == pl.num_programs(1) - 1)
        def _():
            o_ref[...] = _rms(o_ref[...], gf_ref[...])


def _mlp(x, g, w_up, w_down, layer, g_final, *, final_norm, tm=1024, tf=512):
    s, d = x.shape
    dff = w_up.shape[2]
    once = pl.Buffered(1)
    return pl.pallas_call(
        functools.partial(_mlp_kernel, final_norm=final_norm),
        grid=(s // tm, dff // tf),
        in_specs=[pl.BlockSpec((tm, d), lambda i, f: (i, 0), pipeline_mode=once),
                  pl.BlockSpec((1, d), lambda i, f: (0, 0)),
                  pl.BlockSpec((None, d, tf), lambda i, f: (layer, 0, f)),
                  pl.BlockSpec((None, tf, d), lambda i, f: (layer, f, 0)),
                  pl.BlockSpec((1, d), lambda i, f: (0, 0))],
        out_specs=pl.BlockSpec((tm, d), lambda i, f: (i, 0), pipeline_mode=once),
        out_shape=jax.ShapeDtypeStruct((s, d), F32),
        scratch_shapes=[pltpu.VMEM((tm, d), BF16)],
        compiler_params=_params(("arbitrary", "arbitrary"), 56),
        name="mlp",
    )(x, g.reshape(1, d), w_up, w_down, g_final.reshape(1, d))


def _dup_heads(w, n_heads, dim):
    d = w.shape[0]
    w = w.reshape(d, n_heads, 1, dim)
    return jnp.broadcast_to(w, (d, n_heads, 2, dim)).reshape(d, n_heads * 2 * dim)


def kernel(x, positions, norm_mix, norm_mlp, norm_final, w_in_ab, w_out_ab, lambda_q1, lambda_k1,
           lambda_q2, lambda_k2, diff_subln, w_in_c, w_out_c, sinks, w_up, w_down):
    b, s, d = x.shape
    assert b == 1
    depth = norm_mix.shape[0]
    h = x.reshape(s, d)
    cos, sin = _rope_tables(positions)

    a_width = A_HEADS * LANES
    c_q = C_Q_HEADS * C_HEAD_DIM
    c_kv = C_KV_HEADS * C_HEAD_DIM
    tn = 512
    for layer in range(depth):
        j = layer // 2
        if layer % 2 == 0:
            proj = _norm_proj(h, norm_mix[layer], w_in_ab, j, cos, sin,
                              n_rope=2 * a_width // tn, n_scaled=a_width // tn,
                              scale=ROPE_DIM ** -0.5 * math.log2(math.e), tn=tn)
            lam_init = 0.8 - 0.6 * math.exp(-0.3 * layer)
            lam_params = jnp.stack([lambda_q1[j], lambda_k1[j], lambda_q2[j], lambda_k2[j]])
            oa = _diff_attn(proj, lam_params, diff_subln[j], lam_init)
            ob = _stickbreak(proj, first_col=3 * a_width)
            h = _out_proj([oa, ob], w_out_ab, j, h)
        else:
            wc = w_in_c[j]
            wq, wk, wv = wc[:, :c_q], wc[:, c_q:c_q + c_kv], wc[:, c_q + c_kv:]
            wcat = jnp.concatenate([wq, _dup_heads(wk, C_KV_HEADS, C_HEAD_DIM),
                                    _dup_heads(wv, C_KV_HEADS, C_HEAD_DIM)], axis=1)
            proj = _norm_proj(h, norm_mix[layer], wcat[None], 0, cos, sin,
                              n_rope=(c_q + 2 * c_kv) // tn, n_scaled=c_q // tn,
                              scale=C_HEAD_DIM ** -0.5 * math.log2(math.e), tn=tn)
            oc = _swa(proj, sinks[j])
            h = _out_proj([oc], w_out_c, j, h)
        h = _mlp(h, norm_mlp[layer], w_up, w_down, layer,
                 norm_final, final_norm=(layer == depth - 1))
    return h.reshape(b, s, d)
```

```python
import functools
import math

import jax
import jax.numpy as jnp
from jax import lax
from jax.experimental import pallas as pl
from jax.experimental.pallas import tpu as pltpu

F32 = jnp.float32
BF16 = jnp.bfloat16

EPS = 1e-6
ROPE_THETA = 10000.0
LANES = 128
ROPE_DIM = 64
A_HEADS = 8
B_HEADS = 8
B_HEAD_DIM = 128
C_Q_HEADS = 32
C_KV_HEADS = 4
C_GROUP = C_Q_HEADS // C_KV_HEADS
C_HEAD_DIM = 64
WINDOW = 128
MASK_VALUE = -1e30
EXP2_UNDERFLOW = -160.0
MIB = 1024 * 1024

NT_DIMS = (((1,), (1,)), ((), ()))


def _params(semantics, vmem_mib):
    return pltpu.CompilerParams(dimension_semantics=semantics,
                                vmem_limit_bytes=vmem_mib * MIB)


def _rms(x, g):
    ms = jnp.mean(x * x, axis=-1, keepdims=True)
    return x * lax.rsqrt(ms + EPS) * g


def _lane_tile(a, width):
    reps = width // LANES
    return a if reps == 1 else jnp.concatenate([a] * reps, axis=1)


def _head(ref, g, rows=slice(None)):
    return ref[rows, g * LANES:(g + 1) * LANES]


def _rope_table_kernel(pos_ref, invf_ref, cos_ref, sin_ref):
    ang = pos_ref[...].astype(F32) * invf_ref[...]
    lane = lax.broadcasted_iota(jnp.int32, ang.shape, 1)
    sign = jnp.where((lane & (ROPE_DIM // 2)) == 0, -1.0, 1.0)
    cos_ref[...] = jnp.cos(ang)
    sin_ref[...] = jnp.sin(ang) * sign


def _rope_tables(positions):
    s = positions.shape[-1]
    tm = 1024
    inv_freq = ROPE_THETA ** (-jnp.arange(0, ROPE_DIM, 2, dtype=F32) / ROPE_DIM)
    invf = jnp.tile(inv_freq, LANES // (ROPE_DIM // 2)).reshape(1, LANES)
    pos = positions.reshape(s, 1)
    return pl.pallas_call(
        _rope_table_kernel,
        grid=(s // tm,),
        in_specs=[pl.BlockSpec((tm, 1), lambda i: (i, 0)),
                  pl.BlockSpec((1, LANES), lambda i: (0, 0))],
        out_specs=[pl.BlockSpec((tm, LANES), lambda i: (i, 0))] * 2,
        out_shape=[jax.ShapeDtypeStruct((s, LANES), F32)] * 2,
        compiler_params=_params(("arbitrary",), 16),
        name="rope_tables",
    )(pos, invf)


def _rope(x, cos, ss):
    half = ROPE_DIM // 2
    lane = lax.broadcasted_iota(jnp.int32, cos.shape, 1)
    low = (lane & half) == 0
    outs = []
    for c in range(x.shape[1] // LANES):
        xc = x[:, c * LANES:(c + 1) * LANES]
        partner = jnp.where(low, pltpu.roll(xc, LANES - half, 1), pltpu.roll(xc, half, 1))
        outs.append(xc * cos + partner * ss)
    return outs[0] if len(outs) == 1 else jnp.concatenate(outs, axis=1)


PROJ_CHUNK = 512


def _norm_proj_kernel(x_ref, g_ref, w_ref, cos_ref, sin_ref, o_ref, xn_ref, *, tile_runs, scale):
    j = pl.program_id(1)

    @pl.when(j == 0)
    def _():
        xn_ref[...] = _rms(x_ref[...], g_ref[...]).astype(BF16)

    def chunk(c, kind):
        cols = slice(c * PROJ_CHUNK, (c + 1) * PROJ_CHUNK)
        acc = jnp.dot(xn_ref[...], w_ref[:, cols].astype(BF16), preferred_element_type=F32)
        if kind != "plain":
            acc = _rope(acc, cos_ref[...], sin_ref[...])
        if kind == "scaled":
            acc = acc * scale
        o_ref[:, cols] = acc.astype(o_ref.dtype)

    for first, end, kinds in tile_runs:
        @pl.when((j >= first) & (j < end))
        def _(kinds=kinds):
            for c, kind in enumerate(kinds):
                chunk(c, kind)


def _tile_runs(n, tn, scaled_cols, rope_cols):
    def kind(col):
        return "scaled" if col < scaled_cols else "rope" if col < rope_cols else "plain"
    patterns = [tuple(kind(t * tn + c * PROJ_CHUNK) for c in range(tn // PROJ_CHUNK))
                for t in range(n // tn)]
    runs, first = [], 0
    for t in range(1, len(patterns) + 1):
        if t == len(patterns) or patterns[t] != patterns[first]:
            runs.append((first, t, patterns[first]))
            first = t
    return tuple(runs)


def _norm_proj(x, g, w, layer, cos, sin, *, scaled_cols, rope_cols, scale, tm=1024, tn=1024):
    s, d = x.shape
    n = w.shape[2]
    kern = functools.partial(_norm_proj_kernel, scale=scale,
                             tile_runs=_tile_runs(n, tn, scaled_cols, rope_cols))
    return pl.pallas_call(
        kern,
        grid=(s // tm, n // tn),
        in_specs=[pl.BlockSpec((tm, d), lambda i, j: (i, 0)),
                  pl.BlockSpec((1, d), lambda i, j: (0, 0)),
                  pl.BlockSpec((None, d, tn), lambda i, j: (layer, 0, j)),
                  pl.BlockSpec((tm, LANES), lambda i, j: (i, 0)),
                  pl.BlockSpec((tm, LANES), lambda i, j: (i, 0))],
        out_specs=pl.BlockSpec((tm, tn), lambda i, j: (i, j)),
        out_shape=jax.ShapeDtypeStruct((s, n), BF16),
        scratch_shapes=[pltpu.VMEM((tm, d), BF16)],
        compiler_params=_params(("arbitrary", "arbitrary"), 56),
        name="norm_proj",
    )(x, g.reshape(1, d), w, cos, sin)


def _diff_attn_kernel(lam_ref, subln_ref, q_ref, k_ref, v_ref, o_ref,
                      qs_ref, m_ref, l_ref, acc_ref, *, tq, tk, heads, lam_init):
    i = pl.program_id(1)
    rows = 2 * tq
    half = LANES // 2
    sub = tq // tk

    for g in range(heads):
        q = _head(q_ref, g).astype(F32)
        lane = lax.broadcasted_iota(jnp.int32, q.shape, 1)
        qs_ref[g, 0:tq, :] = jnp.where(lane < half, q, 0.0).astype(BF16)
        qs_ref[g, tq:rows, :] = jnp.where(lane >= half, q, 0.0).astype(BF16)
    m_ref[...] = jnp.full(m_ref.shape, MASK_VALUE, F32)
    l_ref[...] = jnp.zeros(l_ref.shape, F32)
    acc_ref[...] = jnp.zeros(acc_ref.shape, F32)

    def step(j, diag_block):
        start = pl.multiple_of(j * tk, tk)
        for g in range(heads):
            kb = _head(k_ref, g, pl.ds(start, tk))
            vb = _head(v_ref, g, pl.ds(start, tk))
            s = lax.dot_general(qs_ref[g], kb, NT_DIMS, preferred_element_type=F32)
            if diag_block is not None:
                row = lax.broadcasted_iota(jnp.int32, s.shape, 0)
                col = lax.broadcasted_iota(jnp.int32, s.shape, 1) + diag_block * tk
                row = jnp.where(row >= tq, row - tq, row)
                s = jnp.where(col <= row, s, MASK_VALUE)
            m_prev = m_ref[g]
            m_new = jnp.maximum(m_prev, jnp.max(s, axis=1, keepdims=True))
            alpha = jnp.exp2(m_prev - m_new)
            p = jnp.exp2(s - _lane_tile(m_new, tk))
            l_ref[g] = alpha * l_ref[g] + jnp.sum(p, axis=1, keepdims=True)
            acc_ref[g] = alpha * acc_ref[g] + jnp.dot(p.astype(BF16), vb,
                                                      preferred_element_type=F32)
            m_ref[g] = m_new

    def body(t, carry):
        for u in range(sub):
            step(t * sub + u, None)
        return carry

    lax.fori_loop(0, i, body, 0)
    for u in range(sub):
        step(i * sub + u, u)

    lp = lam_ref[...]
    lam = (jnp.exp(jnp.sum(lp[0:1] * lp[1:2], axis=1, keepdims=True))
           - jnp.exp(jnp.sum(lp[2:3] * lp[3:4], axis=1, keepdims=True)) + lam_init)
    for g in range(heads):
        o = acc_ref[g] / l_ref[g]
        d = o[0:tq] - lam * o[tq:rows]
        y = _rms(d, subln_ref[...]) * (1.0 - lam_init)
        o_ref[:, g * LANES:(g + 1) * LANES] = y.astype(o_ref.dtype)


def _diff_attn(proj, lam_params, subln, lam_init, *, tq=512, tk=512, heads=4):
    s = proj.shape[0]
    width = A_HEADS * LANES
    hw = heads * LANES
    groups = A_HEADS // heads
    kern = functools.partial(_diff_attn_kernel, tq=tq, tk=tk, heads=heads, lam_init=lam_init)
    return pl.pallas_call(
        kern,
        grid=(groups, s // tq),
        in_specs=[pl.BlockSpec((4, ROPE_DIM), lambda h, i: (0, 0)),
                  pl.BlockSpec((1, LANES), lambda h, i: (0, 0)),
                  pl.BlockSpec((tq, hw), lambda h, i: (i, h)),
                  pl.BlockSpec((s, hw), lambda h, i: (0, groups + h),
                               pipeline_mode=pl.Buffered(1)),
                  pl.BlockSpec((s, hw), lambda h, i: (0, 2 * groups + h),
                               pipeline_mode=pl.Buffered(1))],
        out_specs=pl.BlockSpec((tq, hw), lambda h, i: (i, h)),
        out_shape=jax.ShapeDtypeStruct((s, width), BF16),
        scratch_shapes=[pltpu.VMEM((heads, 2 * tq, LANES), BF16),
                        pltpu.VMEM((heads, 2 * tq, LANES), F32),
                        pltpu.VMEM((heads, 2 * tq, LANES), F32),
                        pltpu.VMEM((heads, 2 * tq, LANES), F32)],
        compiler_params=_params(("arbitrary", "arbitrary"), 48),
        name="diff_attn",
    )(lam_params, subln.reshape(1, LANES), proj, proj, proj)


def _stickbreak_kernel(q_ref, k_ref, v_ref, o_ref, tri_ref, carry_ref, acc_ref, *,
                       tq, tk, heads, scale):
    h = pl.program_id(0)
    i = pl.program_id(1)
    sub = tq // tk

    @pl.when((h == 0) & (i == 0))
    def _():
        r = lax.broadcasted_iota(jnp.int32, tri_ref.shape, 0)
        c = lax.broadcasted_iota(jnp.int32, tri_ref.shape, 1)
        tri_ref[...] = jnp.where(r > c, 1.0, 0.0).astype(BF16)

    carry_ref[...] = jnp.zeros(carry_ref.shape, F32)
    acc_ref[...] = jnp.zeros(acc_ref.shape, F32)

    def step(j, diag_block):
        start = pl.multiple_of(j * tk, tk)
        for g in range(heads):
            kb = _head(k_ref, g, pl.ds(start, tk))
            vb = _head(v_ref, g, pl.ds(start, tk))
            qk = lax.dot_general(_head(q_ref, g), kb, NT_DIMS, preferred_element_type=F32)
            z = qk * scale
            nz = qk * (-scale)
            lb = jnp.minimum(nz, 0.0) - jnp.log2(1.0 + jnp.exp2(jnp.minimum(z, nz)))
            if diag_block is not None:
                row = lax.broadcasted_iota(jnp.int32, z.shape, 0)
                col = lax.broadcasted_iota(jnp.int32, z.shape, 1) + diag_block * tk
                strict = col < row
                lb = jnp.where(strict, lb, 0.0)
            later = jnp.dot(lb.astype(BF16), tri_ref[...], preferred_element_type=F32)
            e = z + lb + later + _lane_tile(carry_ref[g], tk)
            if diag_block is not None:
                e = jnp.where(strict, e, MASK_VALUE)
            w = jnp.exp2(e)
            acc_ref[g] += jnp.dot(w.astype(BF16), vb, preferred_element_type=F32)
            carry_ref[g] += jnp.sum(lb, axis=1, keepdims=True)

    for u in reversed(range(sub)):
        step(i * sub + u, u)

    def more(state):
        j, top = state
        return (j >= 0) & (top > EXP2_UNDERFLOW)

    def body(state):
        j, _ = state
        step(j, None)
        return j - 1, jnp.max(carry_ref[...])

    lax.while_loop(more, body, (i * sub - 1, jnp.float32(0.0)))
    for g in range(heads):
        o_ref[:, g * LANES:(g + 1) * LANES] = acc_ref[g].astype(o_ref.dtype)


def _stickbreak(proj, *, first_col, tq=512, tk=256, heads=4):
    s = proj.shape[0]
    width = B_HEADS * B_HEAD_DIM
    hw = heads * LANES
    groups = B_HEADS // heads
    c0 = first_col // hw
    kern = functools.partial(_stickbreak_kernel, tq=tq, tk=tk, heads=heads,
                             scale=B_HEAD_DIM ** -0.5 * math.log2(math.e))
    return pl.pallas_call(
        kern,
        grid=(groups, s // tq),
        in_specs=[pl.BlockSpec((tq, hw), lambda h, i: (i, c0 + h)),
                  pl.BlockSpec((s, hw), lambda h, i: (0, c0 + groups + h),
                               pipeline_mode=pl.Buffered(1)),
                  pl.BlockSpec((s, hw), lambda h, i: (0, c0 + 2 * groups + h),
                               pipeline_mode=pl.Buffered(1))],
        out_specs=pl.BlockSpec((tq, hw), lambda h, i: (i, h)),
        out_shape=jax.ShapeDtypeStruct((s, width), BF16),
        scratch_shapes=[pltpu.VMEM((tk, tk), BF16),
                        pltpu.VMEM((heads, tq, LANES), F32),
                        pltpu.VMEM((heads, tq, LANES), F32)],
        compiler_params=_params(("arbitrary", "arbitrary"), 48),
        name="stickbreak",
    )(proj, proj, proj)


def _swa_kernel(sinks_ref, q_ref, kp_ref, kc_ref, vp_ref, vc_ref, o_ref, *, blocks):
    hk = pl.program_id(0)
    n = pl.program_id(1)
    w = WINDOW
    half = LANES // 2

    def lane_halves(prev_ref, cur_ref):
        both = jnp.concatenate([prev_ref[...], cur_ref[...]], axis=0).astype(F32)
        low = lax.broadcasted_iota(jnp.int32, both.shape, 1) < half
        return jnp.where(low, both, 0.0).astype(BF16), jnp.where(low, 0.0, both).astype(BF16)

    k_lo, k_hi = lane_halves(kp_ref, kc_ref)
    v_lo, v_hi = lane_halves(vp_ref, vc_ref)

    qi = lax.broadcasted_iota(jnp.int32, (w, 2 * w), 0)
    ki = lax.broadcasted_iota(jnp.int32, (w, 2 * w), 1)
    band = (ki > qi) & (ki <= qi + w)
    first_real = jnp.where(n > 0, 0, w)
    band_first = band & (ki >= first_real)
    lane_o = lax.broadcasted_iota(jnp.int32, (w, LANES), 1)
    log2e = math.log2(math.e)

    for b in range(blocks):
        keys = slice(b * w, (b + 2) * w)
        kbd = jnp.concatenate([k_lo[keys], k_hi[keys]], axis=0)
        vbd = jnp.concatenate([v_lo[keys], v_hi[keys]], axis=0)
        mask = band_first if b == 0 else band
        pairs = C_GROUP // 2
        qs = jnp.concatenate([q_ref[b * w:(b + 1) * w, p * LANES:(p + 1) * LANES]
                              for p in range(pairs)], axis=0)
        s = lax.dot_general(qs, kbd, NT_DIMS, preferred_element_type=F32)
        probs, scales = [], []
        for p in range(pairs):
            row_p, inv_den = [], []
            for t in range(2):
                st = jnp.where(mask, s[p * w:(p + 1) * w, t * 2 * w:(t + 1) * 2 * w], MASK_VALUE)
                sink = sinks_ref[hk * C_GROUP + 2 * p + t] * log2e
                m = jnp.maximum(jnp.max(st, axis=1, keepdims=True), sink)
                pt = jnp.exp2(st - m)
                den = jnp.sum(pt, axis=1, keepdims=True) + jnp.exp2(sink - m)
                row_p.append(pt.astype(BF16))
                inv_den.append(1.0 / den)
            probs.append(jnp.concatenate(row_p, axis=1))
            scales.append(jnp.where(lane_o < half, inv_den[0], inv_den[1]))
        o = jnp.dot(jnp.concatenate(probs, axis=0), vbd, preferred_element_type=F32)
        outs = [o[p * w:(p + 1) * w] * scales[p] for p in range(pairs)]
        o_ref[b * w:(b + 1) * w, :] = jnp.concatenate(outs, axis=1).astype(o_ref.dtype)


def _swa(proj, sinks, *, blocks=4):
    s = proj.shape[0]
    w = WINDOW
    tq = blocks * w
    qw = C_GROUP * C_HEAD_DIM
    k0 = C_Q_HEADS * C_HEAD_DIM // LANES
    v0 = k0 + C_KV_HEADS
    prev = lambda n: jnp.maximum(n * blocks - 1, 0)
    return pl.pallas_call(
        functools.partial(_swa_kernel, blocks=blocks),
        grid=(C_KV_HEADS, s // tq),
        in_specs=[pl.BlockSpec(memory_space=pltpu.SMEM),
                  pl.BlockSpec((tq, qw), lambda h, n: (n, h)),
                  pl.BlockSpec((w, LANES), lambda h, n: (prev(n), k0 + h)),
                  pl.BlockSpec((tq, LANES), lambda h, n: (n, k0 + h)),
                  pl.BlockSpec((w, LANES), lambda h, n: (prev(n), v0 + h)),
                  pl.BlockSpec((tq, LANES), lambda h, n: (n, v0 + h))],
        out_specs=pl.BlockSpec((tq, qw), lambda h, n: (n, h)),
        out_shape=jax.ShapeDtypeStruct((s, C_Q_HEADS * C_HEAD_DIM), BF16),
        compiler_params=_params(("arbitrary", "arbitrary"), 16),
        name="swa_sink",
    )(sinks, proj, proj, proj, proj, proj)


def _out_proj_kernel(*refs, n_in):
    a_refs = refs[:n_in]
    w_ref, x_ref, o_ref, wb_ref = refs[n_in:]

    @pl.when(pl.program_id(0) == 0)
    def _():
        wb_ref[...] = w_ref[...].astype(BF16)

    acc = x_ref[...]
    row = 0
    for a_ref in a_refs:
        ka = a_ref.shape[1]
        acc = acc + jnp.dot(a_ref[...], wb_ref[row:row + ka, :], preferred_element_type=F32)
        row += ka
    o_ref[...] = acc


def _out_proj(acts, w, layer, x, *, tm=512):
    s, d = x.shape
    n_in = len(acts)
    kw = w.shape[1]
    a_specs = [pl.BlockSpec((tm, a.shape[1]), lambda i: (i, 0)) for a in acts]
    w_spec = pl.BlockSpec((None, kw, d), lambda i: (layer, 0, 0), pipeline_mode=pl.Buffered(1))
    return pl.pallas_call(
        functools.partial(_out_proj_kernel, n_in=n_in),
        grid=(s // tm,),
        in_specs=a_specs + [w_spec, pl.BlockSpec((tm, d), lambda i: (i, 0))],
        out_specs=pl.BlockSpec((tm, d), lambda i: (i, 0)),
        out_shape=jax.ShapeDtypeStruct((s, d), F32),
        scratch_shapes=[pltpu.VMEM((kw, d), BF16)],
        compiler_params=_params(("arbitrary",), 56),
        name="out_proj",
    )(*acts, w, x)


def _mlp_kernel(x_ref, g_ref, wu_ref, wd_ref, gf_ref, o_ref, xn_ref, *, final_norm):
    f = pl.program_id(1)

    @pl.when(f == 0)
    def _():
        x = x_ref[...]
        xn_ref[...] = _rms(x, g_ref[...]).astype(BF16)
        o_ref[...] = x

    hid = jnp.dot(xn_ref[...], wu_ref[...].astype(BF16), preferred_element_type=F32)
    hid = jnp.square(jnp.maximum(hid, 0.0)).astype(BF16)
    o_ref[...] += jnp.dot(hid, wd_ref[...].astype(BF16), preferred_element_type=F32)

    if final_norm:
        @pl.when(f == pl.num_programs(1) - 1)
        def _():
            o_ref[...] = _rms(o_ref[...], gf_ref[...])


def _mlp(x, g, w_up, w_down, layer, g_final, *, final_norm, tm=1024, tf=512):
    s, d = x.shape
    dff = w_up.shape[2]
    return pl.pallas_call(
        functools.partial(_mlp_kernel, final_norm=final_norm),
        grid=(s // tm, dff // tf),
        in_specs=[pl.BlockSpec((tm, d), lambda i, f: (i, 0)),
                  pl.BlockSpec((1, d), lambda i, f: (0, 0)),
                  pl.BlockSpec((None, d, tf), lambda i, f: (layer, 0, f)),
                  pl.BlockSpec((None, tf, d), lambda i, f: (layer, f, 0)),
                  pl.BlockSpec((1, d), lambda i, f: (0, 0))],
        out_specs=pl.BlockSpec((tm, d), lambda i, f: (i, 0)),
        out_shape=jax.ShapeDtypeStruct((s, d), F32),
        scratch_shapes=[pltpu.VMEM((tm, d), BF16)],
        compiler_params=_params(("arbitrary", "arbitrary"), 60),
        name="mlp",
    )(x, g.reshape(1, d), w_up, w_down, g_final.reshape(1, d))


def _dup_heads(w, n_heads, dim):
    d = w.shape[0]
    w = w.reshape(d, n_heads, 1, dim)
    return jnp.broadcast_to(w, (d, n_heads, 2, dim)).reshape(d, n_heads * 2 * dim)


def kernel(x, positions, norm_mix, norm_mlp, norm_final, w_in_ab, w_out_ab, lambda_q1, lambda_k1,
           lambda_q2, lambda_k2, diff_subln, w_in_c, w_out_c, sinks, w_up, w_down):
    b, s, d = x.shape
    assert b == 1
    depth = norm_mix.shape[0]
    h = x.reshape(s, d)
    cos, sin = _rope_tables(positions)

    a_width = A_HEADS * LANES
    c_q = C_Q_HEADS * C_HEAD_DIM
    c_kv = C_KV_HEADS * C_HEAD_DIM
    for layer in range(depth):
        j = layer // 2
        if layer % 2 == 0:
            proj = _norm_proj(h, norm_mix[layer], w_in_ab, j, cos, sin,
                              scaled_cols=a_width, rope_cols=2 * a_width,
                              scale=ROPE_DIM ** -0.5 * math.log2(math.e))
            lam_init = 0.8 - 0.6 * math.exp(-0.3 * layer)
            lam_params = jnp.stack([lambda_q1[j], lambda_k1[j], lambda_q2[j], lambda_k2[j]])
            oa = _diff_attn(proj, lam_params, diff_subln[j], lam_init)
            ob = _stickbreak(proj, first_col=3 * a_width)
            h = _out_proj([oa, ob], w_out_ab, j, h)
        else:
            wc = w_in_c[j]
            wq, wk, wv = wc[:, :c_q], wc[:, c_q:c_q + c_kv], wc[:, c_q + c_kv:]
            wcat = jnp.concatenate([wq, _dup_heads(wk, C_KV_HEADS, C_HEAD_DIM),
                                    _dup_heads(wv, C_KV_HEADS, C_HEAD_DIM)], axis=1)
            proj = _norm_proj(h, norm_mix[layer], wcat[None], 0, cos, sin,
                              scaled_cols=c_q, rope_cols=c_q + 2 * c_kv,
                              scale=C_HEAD_DIM ** -0.5 * math.log2(math.e))
            oc = _swa(proj, sinks[j])
            h = _out_proj([oc], w_out_c, j, h)
        h = _mlp(h, norm_mlp[layer], w_up, w_down, layer,
                 norm_final, final_norm=(layer == depth - 1))
    return h.reshape(b, s, d)
```

```python
import functools
import math

import jax
import jax.numpy as jnp
from jax import lax
from jax.experimental import pallas as pl
from jax.experimental.pallas import tpu as pltpu

F32 = jnp.float32
BF16 = jnp.bfloat16

EPS = 1e-6
ROPE_THETA = 10000.0
LANES = 128
ROPE_DIM = 64
A_HEADS = 8
B_HEADS = 8
B_HEAD_DIM = 128
C_Q_HEADS = 32
C_KV_HEADS = 4
C_GROUP = C_Q_HEADS // C_KV_HEADS
C_HEAD_DIM = 64
WINDOW = 128
MASK_VALUE = -1e30
EXP2_UNDERFLOW = -160.0
FINITE_BOUND = 1e30
MIB = 1024 * 1024

NT_DIMS = (((1,), (1,)), ((), ()))


def _params(semantics, vmem_mib):
    return pltpu.CompilerParams(dimension_semantics=semantics,
                                vmem_limit_bytes=vmem_mib * MIB)


def _rms(x, g):
    ms = jnp.mean(x * x, axis=-1, keepdims=True)
    return x * lax.rsqrt(ms + EPS) * g


def _lane_tile(a, width):
    reps = width // LANES
    return a if reps == 1 else jnp.concatenate([a] * reps, axis=1)


def _head(ref, g, rows=slice(None)):
    return ref[rows, g * LANES:(g + 1) * LANES]


def _rope_table_kernel(pos_ref, invf_ref, cos_ref, sin_ref):
    ang = pos_ref[...].astype(F32) * invf_ref[...]
    lane = lax.broadcasted_iota(jnp.int32, ang.shape, 1)
    sign = jnp.where((lane & (ROPE_DIM // 2)) == 0, -1.0, 1.0)
    cos_ref[...] = jnp.cos(ang)
    sin_ref[...] = jnp.sin(ang) * sign


def _rope_tables(positions):
    s = positions.shape[-1]
    tm = 1024
    inv_freq = ROPE_THETA ** (-jnp.arange(0, ROPE_DIM, 2, dtype=F32) / ROPE_DIM)
    invf = jnp.tile(inv_freq, LANES // (ROPE_DIM // 2)).reshape(1, LANES)
    pos = positions.reshape(s, 1)
    return pl.pallas_call(
        _rope_table_kernel,
        grid=(s // tm,),
        in_specs=[pl.BlockSpec((tm, 1), lambda i: (i, 0)),
                  pl.BlockSpec((1, LANES), lambda i: (0, 0))],
        out_specs=[pl.BlockSpec((tm, LANES), lambda i: (i, 0))] * 2,
        out_shape=[jax.ShapeDtypeStruct((s, LANES), F32)] * 2,
        compiler_params=_params(("arbitrary",), 16),
        name="rope_tables",
    )(pos, invf)


def _rope(x, cos, ss):
    half = ROPE_DIM // 2
    lane = lax.broadcasted_iota(jnp.int32, cos.shape, 1)
    low = (lane & half) == 0
    outs = []
    for c in range(x.shape[1] // LANES):
        xc = x[:, c * LANES:(c + 1) * LANES]
        partner = jnp.where(low, pltpu.roll(xc, LANES - half, 1), pltpu.roll(xc, half, 1))
        outs.append(xc * cos + partner * ss)
    return outs[0] if len(outs) == 1 else jnp.concatenate(outs, axis=1)


PROJ_CHUNK = 512


def _norm_proj_kernel(x_ref, g_ref, w_ref, cos_ref, sin_ref, o_ref, xn_ref, *, tile_runs, scale):
    j = pl.program_id(1)

    @pl.when(j == 0)
    def _():
        xn_ref[...] = _rms(x_ref[...], g_ref[...]).astype(BF16)

    def chunk(c, kind):
        cols = slice(c * PROJ_CHUNK, (c + 1) * PROJ_CHUNK)
        acc = jnp.dot(xn_ref[...], w_ref[:, cols].astype(BF16), preferred_element_type=F32)
        if kind != "plain":
            acc = _rope(acc, cos_ref[...], sin_ref[...])
        if kind == "scaled":
            acc = acc * scale
        o_ref[:, cols] = acc.astype(o_ref.dtype)

    for first, end, kinds in tile_runs:
        @pl.when((j >= first) & (j < end))
        def _(kinds=kinds):
            for c, kind in enumerate(kinds):
                chunk(c, kind)


def _tile_runs(n, tn, scaled_cols, rope_cols):
    def kind(col):
        return "scaled" if col < scaled_cols else "rope" if col < rope_cols else "plain"
    patterns = [tuple(kind(t * tn + c * PROJ_CHUNK) for c in range(tn // PROJ_CHUNK))
                for t in range(n // tn)]
    runs, first = [], 0
    for t in range(1, len(patterns) + 1):
        if t == len(patterns) or patterns[t] != patterns[first]:
            runs.append((first, t, patterns[first]))
            first = t
    return tuple(runs)


def _norm_proj(x, g, w, layer, cos, sin, *, scaled_cols, rope_cols, scale, tm=1024, tn=1024):
    s, d = x.shape
    n = w.shape[2]
    kern = functools.partial(_norm_proj_kernel, scale=scale,
                             tile_runs=_tile_runs(n, tn, scaled_cols, rope_cols))
    return pl.pallas_call(
        kern,
        grid=(s // tm, n // tn),
        in_specs=[pl.BlockSpec((tm, d), lambda i, j: (i, 0)),
                  pl.BlockSpec((1, d), lambda i, j: (0, 0)),
                  pl.BlockSpec((None, d, tn), lambda i, j: (layer, 0, j)),
                  pl.BlockSpec((tm, LANES), lambda i, j: (i, 0)),
                  pl.BlockSpec((tm, LANES), lambda i, j: (i, 0))],
        out_specs=pl.BlockSpec((tm, tn), lambda i, j: (i, j)),
        out_shape=jax.ShapeDtypeStruct((s, n), BF16),
        scratch_shapes=[pltpu.VMEM((tm, d), BF16)],
        compiler_params=_params(("arbitrary", "arbitrary"), 56),
        name="norm_proj",
    )(x, g.reshape(1, d), w, cos, sin)


def _diff_attn_kernel(lam_ref, subln_ref, q_ref, k_ref, v_ref, o_ref,
                      qs_ref, m_ref, l_ref, acc_ref, *, tq, tk, heads, lam_init):
    i = pl.program_id(1)
    rows = 2 * tq
    half = LANES // 2
    sub = tq // tk

    for g in range(heads):
        q = _head(q_ref, g).astype(F32)
        lane = lax.broadcasted_iota(jnp.int32, q.shape, 1)
        qs_ref[g, 0:tq, :] = jnp.where(lane < half, q, 0.0).astype(BF16)
        qs_ref[g, tq:rows, :] = jnp.where(lane >= half, q, 0.0).astype(BF16)

    def reset():
        m_ref[...] = jnp.full(m_ref.shape, MASK_VALUE, F32)
        l_ref[...] = jnp.zeros(l_ref.shape, F32)
        acc_ref[...] = jnp.zeros(acc_ref.shape, F32)

    def step(j, diag_block):
        start = pl.multiple_of(j * tk, tk)
        for g in range(heads):
            kb = _head(k_ref, g, pl.ds(start, tk))
            vb = _head(v_ref, g, pl.ds(start, tk))
            s = lax.dot_general(qs_ref[g], kb, NT_DIMS, preferred_element_type=F32)
            if diag_block is not None:
                row = lax.broadcasted_iota(jnp.int32, s.shape, 0)
                col = lax.broadcasted_iota(jnp.int32, s.shape, 1) + diag_block * tk
                row = jnp.where(row >= tq, row - tq, row)
                s = jnp.where(col <= row, s, MASK_VALUE)
            m_prev = m_ref[g]
            m_new = jnp.maximum(m_prev, jnp.max(s, axis=1, keepdims=True))
            alpha = jnp.exp2(m_prev - m_new)
            p = jnp.exp2(s - _lane_tile(m_new, tk))
            l_ref[g] = alpha * l_ref[g] + jnp.sum(p, axis=1, keepdims=True)
            acc_ref[g] = alpha * acc_ref[g] + jnp.dot(p.astype(BF16), vb,
                                                      preferred_element_type=F32)
            m_ref[g] = m_new

    def frozen_step(j):
        start = pl.multiple_of(j * tk, tk)
        for g in range(heads):
            kb = _head(k_ref, g, pl.ds(start, tk))
            vb = _head(v_ref, g, pl.ds(start, tk))
            s = lax.dot_general(qs_ref[g], kb, NT_DIMS, preferred_element_type=F32)
            p = jnp.exp2(s - _lane_tile(m_ref[g], tk))
            l_ref[g] += jnp.sum(p, axis=1, keepdims=True)
            acc_ref[g] += jnp.dot(p.astype(BF16), vb, preferred_element_type=F32)

    def diagonal():
        for u in range(sub):
            step(i * sub + u, u)

    def finish():
        lp = lam_ref[...]
        lam = (jnp.exp(jnp.sum(lp[0:1] * lp[1:2], axis=1, keepdims=True))
               - jnp.exp(jnp.sum(lp[2:3] * lp[3:4], axis=1, keepdims=True)) + lam_init)
        for g in range(heads):
            o = acc_ref[g] / l_ref[g]
            d = o[0:tq] - lam * o[tq:rows]
            y = _rms(d, subln_ref[...]) * (1.0 - lam_init)
            o_ref[:, g * LANES:(g + 1) * LANES] = y.astype(o_ref.dtype)

    n_full = i * sub
    reset()
    diagonal()

    def pair(t, carry):
        frozen_step(2 * t)
        frozen_step(2 * t + 1)
        return carry

    lax.fori_loop(0, n_full // 2, pair, 0)

    @pl.when(n_full % 2 == 1)
    def _():
        frozen_step(n_full - 1)

    finite = jnp.minimum(jnp.min(jnp.where(l_ref[...] < FINITE_BOUND, 1.0, 0.0)),
                         jnp.min(jnp.where(jnp.abs(acc_ref[...]) < FINITE_BOUND, 1.0, 0.0)))

    @pl.when(finite > 0.5)
    def _():
        finish()

    @pl.when(finite <= 0.5)
    def _():
        reset()

        def body(j, carry):
            step(j, None)
            return carry

        lax.fori_loop(0, n_full, body, 0)
        diagonal()
        finish()


def _diff_attn(proj, lam_params, subln, lam_init, *, tq=512, tk=512, heads=4):
    s = proj.shape[0]
    width = A_HEADS * LANES
    hw = heads * LANES
    groups = A_HEADS // heads
    kern = functools.partial(_diff_attn_kernel, tq=tq, tk=tk, heads=heads, lam_init=lam_init)
    return pl.pallas_call(
        kern,
        grid=(groups, s // tq),
        in_specs=[pl.BlockSpec((4, ROPE_DIM), lambda h, i: (0, 0)),
                  pl.BlockSpec((1, LANES), lambda h, i: (0, 0)),
                  pl.BlockSpec((tq, hw), lambda h, i: (i, h)),
                  pl.BlockSpec((s, hw), lambda h, i: (0, groups + h),
                               pipeline_mode=pl.Buffered(1)),
                  pl.BlockSpec((s, hw), lambda h, i: (0, 2 * groups + h),
                               pipeline_mode=pl.Buffered(1))],
        out_specs=pl.BlockSpec((tq, hw), lambda h, i: (i, h)),
        out_shape=jax.ShapeDtypeStruct((s, width), BF16),
        scratch_shapes=[pltpu.VMEM((heads, 2 * tq, LANES), BF16)]
                       + [pltpu.VMEM((heads, 2 * tq, LANES), F32)] * 3,
        compiler_params=_params(("arbitrary", "arbitrary"), 48),
        name="diff_attn",
    )(lam_params, subln.reshape(1, LANES), proj, proj, proj)


def _stickbreak_kernel(q_ref, k_ref, v_ref, o_ref, tri_ref, carry_ref, acc_ref, *,
                       tq, tk, heads, scale):
    h = pl.program_id(0)
    i = pl.program_id(1)
    sub = tq // tk

    @pl.when((h == 0) & (i == 0))
    def _():
        r = lax.broadcasted_iota(jnp.int32, tri_ref.shape, 0)
        c = lax.broadcasted_iota(jnp.int32, tri_ref.shape, 1)
        tri_ref[...] = jnp.where(r > c, 1.0, 0.0).astype(BF16)

    carry_ref[...] = jnp.zeros(carry_ref.shape, F32)
    acc_ref[...] = jnp.zeros(acc_ref.shape, F32)

    def step(j, diag_block):
        start = pl.multiple_of(j * tk, tk)
        for g in range(heads):
            kb = _head(k_ref, g, pl.ds(start, tk))
            vb = _head(v_ref, g, pl.ds(start, tk))
            qk = lax.dot_general(_head(q_ref, g), kb, NT_DIMS, preferred_element_type=F32)
            z = qk * scale
            nz = qk * (-scale)
            lb = jnp.minimum(nz, 0.0) - jnp.log2(1.0 + jnp.exp2(jnp.minimum(z, nz)))
            if diag_block is not None:
                row = lax.broadcasted_iota(jnp.int32, z.shape, 0)
                col = lax.broadcasted_iota(jnp.int32, z.shape, 1) + diag_block * tk
                strict = col < row
                lb = jnp.where(strict, lb, 0.0)
            later = jnp.dot(lb.astype(BF16), tri_ref[...], preferred_element_type=F32)
            e = z + lb + later + _lane_tile(carry_ref[g], tk)
            if diag_block is not None:
                e = jnp.where(strict, e, MASK_VALUE)
            w = jnp.exp2(e)
            acc_ref[g] += jnp.dot(w.astype(BF16), vb, preferred_element_type=F32)
            carry_ref[g] += jnp.sum(lb, axis=1, keepdims=True)

    for u in reversed(range(sub)):
        step(i * sub + u, u)

    def more(state):
        j, top = state
        return (j >= 0) & (top > EXP2_UNDERFLOW)

    def body(state):
        j, _ = state
        step(j, None)
        return j - 1, jnp.max(carry_ref[...])

    lax.while_loop(more, body, (i * sub - 1, jnp.float32(0.0)))
    for g in range(heads):
        o_ref[:, g * LANES:(g + 1) * LANES] = acc_ref[g].astype(o_ref.dtype)


def _stickbreak(proj, *, first_col, tq=512, tk=256, heads=4):
    s = proj.shape[0]
    width = B_HEADS * B_HEAD_DIM
    hw = heads * LANES
    groups = B_HEADS // heads
    c0 = first_col // hw
    kern = functools.partial(_stickbreak_kernel, tq=tq, tk=tk, heads=heads,
                             scale=B_HEAD_DIM ** -0.5 * math.log2(math.e))
    return pl.pallas_call(
        kern,
        grid=(groups, s // tq),
        in_specs=[pl.BlockSpec((tq, hw), lambda h, i: (i, c0 + h)),
                  pl.BlockSpec((s, hw), lambda h, i: (0, c0 + groups + h),
                               pipeline_mode=pl.Buffered(1)),
                  pl.BlockSpec((s, hw), lambda h, i: (0, c0 + 2 * groups + h),
                               pipeline_mode=pl.Buffered(1))],
        out_specs=pl.BlockSpec((tq, hw), lambda h, i: (i, h)),
        out_shape=jax.ShapeDtypeStruct((s, width), BF16),
        scratch_shapes=[pltpu.VMEM((tk, tk), BF16),
                        pltpu.VMEM((heads, tq, LANES), F32),
                        pltpu.VMEM((heads, tq, LANES), F32)],
        compiler_params=_params(("arbitrary", "arbitrary"), 48),
        name="stickbreak",
    )(proj, proj, proj)


def _swa_kernel(sinks_ref, q_ref, kp_ref, kc_ref, vp_ref, vc_ref, o_ref, *, blocks):
    hk = pl.program_id(0)
    n = pl.program_id(1)
    w = WINDOW
    half = LANES // 2

    def lane_halves(prev_ref, cur_ref):
        both = jnp.concatenate([prev_ref[...], cur_ref[...]], axis=0).astype(F32)
        low = lax.broadcasted_iota(jnp.int32, both.shape, 1) < half
        return jnp.where(low, both, 0.0).astype(BF16), jnp.where(low, 0.0, both).astype(BF16)

    k_lo, k_hi = lane_halves(kp_ref, kc_ref)
    v_lo, v_hi = lane_halves(vp_ref, vc_ref)

    qi = lax.broadcasted_iota(jnp.int32, (w, 2 * w), 0)
    ki = lax.broadcasted_iota(jnp.int32, (w, 2 * w), 1)
    band = (ki > qi) & (ki <= qi + w)
    first_real = jnp.where(n > 0, 0, w)
    band_first = band & (ki >= first_real)
    lane_o = lax.broadcasted_iota(jnp.int32, (w, LANES), 1)
    log2e = math.log2(math.e)

    for b in range(blocks):
        keys = slice(b * w, (b + 2) * w)
        kbd = jnp.concatenate([k_lo[keys], k_hi[keys]], axis=0)
        vbd = jnp.concatenate([v_lo[keys], v_hi[keys]], axis=0)
        mask = band_first if b == 0 else band
        pairs = C_GROUP // 2
        qs = jnp.concatenate([q_ref[b * w:(b + 1) * w, p * LANES:(p + 1) * LANES]
                              for p in range(pairs)], axis=0)
        s = lax.dot_general(qs, kbd, NT_DIMS, preferred_element_type=F32)
        probs, scales = [], []
        for p in range(pairs):
            row_p, inv_den = [], []
            for t in range(2):
                st = jnp.where(mask, s[p * w:(p + 1) * w, t * 2 * w:(t + 1) * 2 * w], MASK_VALUE)
                sink = sinks_ref[hk * C_GROUP + 2 * p + t] * log2e
                m = jnp.maximum(jnp.max(st, axis=1, keepdims=True), sink)
                pt = jnp.exp2(st - m)
                den = jnp.sum(pt, axis=1, keepdims=True) + jnp.exp2(sink - m)
                row_p.append(pt.astype(BF16))
                inv_den.append(1.0 / den)
            probs.append(jnp.concatenate(row_p, axis=1))
            scales.append(jnp.where(lane_o < half, inv_den[0], inv_den[1]))
        o = jnp.dot(jnp.concatenate(probs, axis=0), vbd, preferred_element_type=F32)
        outs = [o[p * w:(p + 1) * w] * scales[p] for p in range(pairs)]
        o_ref[b * w:(b + 1) * w, :] = jnp.concatenate(outs, axis=1).astype(o_ref.dtype)


def _swa(proj, sinks, *, blocks=4):
    s = proj.shape[0]
    w = WINDOW
    tq = blocks * w
    qw = C_GROUP * C_HEAD_DIM
    k0 = C_Q_HEADS * C_HEAD_DIM // LANES
    v0 = k0 + C_KV_HEADS
    prev = lambda n: jnp.maximum(n * blocks - 1, 0)
    return pl.pallas_call(
        functools.partial(_swa_kernel, blocks=blocks),
        grid=(C_KV_HEADS, s // tq),
        in_specs=[pl.BlockSpec(memory_space=pltpu.SMEM),
                  pl.BlockSpec((tq, qw), lambda h, n: (n, h)),
                  pl.BlockSpec((w, LANES), lambda h, n: (prev(n), k0 + h)),
                  pl.BlockSpec((tq, LANES), lambda h, n: (n, k0 + h)),
                  pl.BlockSpec((w, LANES), lambda h, n: (prev(n), v0 + h)),
                  pl.BlockSpec((tq, LANES), lambda h, n: (n, v0 + h))],
        out_specs=pl.BlockSpec((tq, qw), lambda h, n: (n, h)),
        out_shape=jax.ShapeDtypeStruct((s, C_Q_HEADS * C_HEAD_DIM), BF16),
        compiler_params=_params(("arbitrary", "arbitrary"), 16),
        name="swa_sink",
    )(sinks, proj, proj, proj, proj, proj)


def _out_proj_kernel(*refs, n_in):
    a_refs = refs[:n_in]
    w_ref, x_ref, o_ref, wb_ref = refs[n_in:]

    @pl.when(pl.program_id(0) == 0)
    def _():
        wb_ref[...] = w_ref[...].astype(BF16)

    acc = x_ref[...]
    row = 0
    for a_ref in a_refs:
        ka = a_ref.shape[1]
        acc = acc + jnp.dot(a_ref[...], wb_ref[row:row + ka, :], preferred_element_type=F32)
        row += ka
    o_ref[...] = acc


def _out_proj(acts, w, layer, x, *, tm=512):
    s, d = x.shape
    n_in = len(acts)
    kw = w.shape[1]
    a_specs = [pl.BlockSpec((tm, a.shape[1]), lambda i: (i, 0)) for a in acts]
    w_spec = pl.BlockSpec((None, kw, d), lambda i: (layer, 0, 0), pipeline_mode=pl.Buffered(1))
    return pl.pallas_call(
        functools.partial(_out_proj_kernel, n_in=n_in),
        grid=(s // tm,),
        in_specs=a_specs + [w_spec, pl.BlockSpec((tm, d), lambda i: (i, 0))],
        out_specs=pl.BlockSpec((tm, d), lambda i: (i, 0)),
        out_shape=jax.ShapeDtypeStruct((s, d), F32),
        scratch_shapes=[pltpu.VMEM((kw, d), BF16)],
        compiler_params=_params(("arbitrary",), 56),
        name="out_proj",
    )(*acts, w, x)


def _mlp_kernel(x_ref, g_ref, wu_ref, wd_ref, gf_ref, o_ref, xn_ref, *, final_norm):
    f = pl.program_id(1)

    @pl.when(f == 0)
    def _():
        x = x_ref[...]
        xn_ref[...] = _rms(x, g_ref[...]).astype(BF16)
        o_ref[...] = x

    hid = jnp.dot(xn_ref[...], wu_ref[...].astype(BF16), preferred_element_type=F32)
    hid = jnp.square(jnp.maximum(hid, 0.0)).astype(BF16)
    o_ref[...] += jnp.dot(hid, wd_ref[...].astype(BF16), preferred_element_type=F32)

    if final_norm:
        @pl.when(f == pl.num_programs(1) - 1)
        def _():
            o_ref[...] = _rms(o_ref[...], gf_ref[...])


def _mlp(x, g, w_up, w_down, layer, g_final, *, final_norm, tm=1024, tf=512):
    s, d = x.shape
    dff = w_up.shape[2]
    return pl.pallas_call(
        functools.partial(_mlp_kernel, final_norm=final_norm),
        grid=(s // tm, dff // tf),
        in_specs=[pl.BlockSpec((tm, d), lambda i, f: (i, 0)),
                  pl.BlockSpec((1, d), lambda i, f: (0, 0)),
                  pl.BlockSpec((None, d, tf), lambda i, f: (layer, 0, f)),
                  pl.BlockSpec((None, tf, d), lambda i, f: (layer, f, 0)),
                  pl.BlockSpec((1, d), lambda i, f: (0, 0))],
        out_specs=pl.BlockSpec((tm, d), lambda i, f: (i, 0)),
        out_shape=jax.ShapeDtypeStruct((s, d), F32),
        scratch_shapes=[pltpu.VMEM((tm, d), BF16)],
        compiler_params=_params(("arbitrary", "arbitrary"), 60),
        name="mlp",
    )(x, g.reshape(1, d), w_up, w_down, g_final.reshape(1, d))


def _dup_heads(w, n_heads, dim):
    d = w.shape[0]
    w = w.reshape(d, n_heads, 1, dim)
    return jnp.broadcast_to(w, (d, n_heads, 2, dim)).reshape(d, n_heads * 2 * dim)


def kernel(x, positions, norm_mix, norm_mlp, norm_final, w_in_ab, w_out_ab, lambda_q1, lambda_k1,
           lambda_q2, lambda_k2, diff_subln, w_in_c, w_out_c, sinks, w_up, w_down):
    b, s, d = x.shape
    assert b == 1
    depth = norm_mix.shape[0]
    h = x.reshape(s, d)
    cos, sin = _rope_tables(positions)

    a_width = A_HEADS * LANES
    c_q = C_Q_HEADS * C_HEAD_DIM
    c_kv = C_KV_HEADS * C_HEAD_DIM
    for layer in range(depth):
        j = layer // 2
        if layer % 2 == 0:
            proj = _norm_proj(h, norm_mix[layer], w_in_ab, j, cos, sin,
                              scaled_cols=a_width, rope_cols=2 * a_width,
                              scale=ROPE_DIM ** -0.5 * math.log2(math.e))
            lam_init = 0.8 - 0.6 * math.exp(-0.3 * layer)
            lam_params = jnp.stack([lambda_q1[j], lambda_k1[j], lambda_q2[j], lambda_k2[j]])
            oa = _diff_attn(proj, lam_params, diff_subln[j], lam_init)
            ob = _stickbreak(proj, first_col=3 * a_width)
            h = _out_proj([oa, ob], w_out_ab, j, h)
        else:
            wc = w_in_c[j]
            wq, wk, wv = wc[:, :c_q], wc[:, c_q:c_q + c_kv], wc[:, c_q + c_kv:]
            wcat = jnp.concatenate([wq, _dup_heads(wk, C_KV_HEADS, C_HEAD_DIM),
                                    _dup_heads(wv, C_KV_HEADS, C_HEAD_DIM)], axis=1)
            proj = _norm_proj(h, norm_mix[layer], wcat[None], 0, cos, sin,
                              scaled_cols=c_q, rope_cols=c_q + 2 * c_kv,
                              scale=C_HEAD_DIM ** -0.5 * math.log2(math.e))
            oc = _swa(proj, sinks[j])
            h = _out_proj([oc], w_out_c, j, h)
        h = _mlp(h, norm_mlp[layer], w_up, w_down, layer,
                 norm_final, final_norm=(layer == depth - 1))
    return h.reshape(b, s, d)
```

```python
import functools
import math

import jax
import jax.numpy as jnp
from jax import lax
from jax.experimental import pallas as pl
from jax.experimental.pallas import tpu as pltpu

F32 = jnp.float32
BF16 = jnp.bfloat16

EPS = 1e-6
ROPE_THETA = 10000.0
LANES = 128
ROPE_DIM = 64
A_HEADS = 8
B_HEADS = 8
B_HEAD_DIM = 128
C_Q_HEADS = 32
C_KV_HEADS = 4
C_GROUP = C_Q_HEADS // C_KV_HEADS
C_HEAD_DIM = 64
WINDOW = 128
MASK_VALUE = -1e30
EXP2_UNDERFLOW = -160.0
FINITE_BOUND = 1e30
MIB = 1024 * 1024

NT_DIMS = (((1,), (1,)), ((), ()))


def _params(semantics, vmem_mib):
    return pltpu.CompilerParams(dimension_semantics=semantics,
                                vmem_limit_bytes=vmem_mib * MIB)


def _rms(x, g):
    ms = jnp.mean(x * x, axis=-1, keepdims=True)
    return x * lax.rsqrt(ms + EPS) * g


def _lane_tile(a, width):
    reps = width // LANES
    return a if reps == 1 else jnp.concatenate([a] * reps, axis=1)


def _head(ref, g, rows=slice(None)):
    return ref[rows, g * LANES:(g + 1) * LANES]


def _rope_table_kernel(pos_ref, invf_ref, cos_ref, sin_ref):
    ang = pos_ref[...].astype(F32) * invf_ref[...]
    lane = lax.broadcasted_iota(jnp.int32, ang.shape, 1)
    sign = jnp.where((lane & (ROPE_DIM // 2)) == 0, -1.0, 1.0)
    cos_ref[...] = jnp.cos(ang)
    sin_ref[...] = jnp.sin(ang) * sign


def _rope_tables(positions):
    s = positions.shape[-1]
    tm = 1024
    inv_freq = ROPE_THETA ** (-jnp.arange(0, ROPE_DIM, 2, dtype=F32) / ROPE_DIM)
    invf = jnp.tile(inv_freq, LANES // (ROPE_DIM // 2)).reshape(1, LANES)
    pos = positions.reshape(s, 1)
    return pl.pallas_call(
        _rope_table_kernel,
        grid=(s // tm,),
        in_specs=[pl.BlockSpec((tm, 1), lambda i: (i, 0)),
                  pl.BlockSpec((1, LANES), lambda i: (0, 0))],
        out_specs=[pl.BlockSpec((tm, LANES), lambda i: (i, 0))] * 2,
        out_shape=[jax.ShapeDtypeStruct((s, LANES), F32)] * 2,
        compiler_params=_params(("arbitrary",), 16),
        name="rope_tables",
    )(pos, invf)


def _rope(x, cos, ss):
    half = ROPE_DIM // 2
    lane = lax.broadcasted_iota(jnp.int32, cos.shape, 1)
    low = (lane & half) == 0
    outs = []
    for c in range(x.shape[1] // LANES):
        xc = x[:, c * LANES:(c + 1) * LANES]
        partner = jnp.where(low, pltpu.roll(xc, LANES - half, 1), pltpu.roll(xc, half, 1))
        outs.append(xc * cos + partner * ss)
    return outs[0] if len(outs) == 1 else jnp.concatenate(outs, axis=1)


PROJ_CHUNK = 512


def _norm_proj_kernel(x_ref, g_ref, w_ref, cos_ref, sin_ref, o_ref, xn_ref, *, tile_runs, scale):
    j = pl.program_id(1)

    @pl.when(j == 0)
    def _():
        xn_ref[...] = _rms(x_ref[...], g_ref[...]).astype(BF16)

    def chunk(c, kind):
        cols = slice(c * PROJ_CHUNK, (c + 1) * PROJ_CHUNK)
        acc = jnp.dot(xn_ref[...], w_ref[:, cols].astype(BF16), preferred_element_type=F32)
        if kind != "plain":
            acc = _rope(acc, cos_ref[...], sin_ref[...])
        if kind == "scaled":
            acc = acc * scale
        o_ref[:, cols] = acc.astype(o_ref.dtype)

    for first, end, kinds in tile_runs:
        @pl.when((j >= first) & (j < end))
        def _(kinds=kinds):
            for c, kind in enumerate(kinds):
                chunk(c, kind)


def _tile_runs(n, tn, scaled_cols, rope_cols):
    def kind(col):
        return "scaled" if col < scaled_cols else "rope" if col < rope_cols else "plain"
    patterns = [tuple(kind(t * tn + c * PROJ_CHUNK) for c in range(tn // PROJ_CHUNK))
                for t in range(n // tn)]
    runs, first = [], 0
    for t in range(1, len(patterns) + 1):
        if t == len(patterns) or patterns[t] != patterns[first]:
            runs.append((first, t, patterns[first]))
            first = t
    return tuple(runs)


def _norm_proj(x, g, w, layer, cos, sin, *, scaled_cols, rope_cols, scale, tm=1024, tn=1024):
    s, d = x.shape
    n = w.shape[2]
    kern = functools.partial(_norm_proj_kernel, scale=scale,
                             tile_runs=_tile_runs(n, tn, scaled_cols, rope_cols))
    return pl.pallas_call(
        kern,
        grid=(s // tm, n // tn),
        in_specs=[pl.BlockSpec((tm, d), lambda i, j: (i, 0)),
                  pl.BlockSpec((1, d), lambda i, j: (0, 0)),
                  pl.BlockSpec((None, d, tn), lambda i, j: (layer, 0, j)),
                  pl.BlockSpec((tm, LANES), lambda i, j: (i, 0)),
                  pl.BlockSpec((tm, LANES), lambda i, j: (i, 0))],
        out_specs=pl.BlockSpec((tm, tn), lambda i, j: (i, j)),
        out_shape=jax.ShapeDtypeStruct((s, n), BF16),
        scratch_shapes=[pltpu.VMEM((tm, d), BF16)],
        compiler_params=_params(("arbitrary", "arbitrary"), 56),
        name="norm_proj",
    )(x, g.reshape(1, d), w, cos, sin)


def _diff_attn_kernel(lam_ref, subln_ref, q_ref, k_ref, v_ref, o_ref,
                      qs_ref, m_ref, l_ref, acc_ref, *, tq, tk, heads, lam_init):
    i = pl.program_id(1)
    rows = 2 * tq
    half = LANES // 2
    sub = tq // tk

    for g in range(heads):
        q = _head(q_ref, g).astype(F32)
        lane = lax.broadcasted_iota(jnp.int32, q.shape, 1)
        qs_ref[g, 0:tq, :] = jnp.where(lane < half, q, 0.0).astype(BF16)
        qs_ref[g, tq:rows, :] = jnp.where(lane >= half, q, 0.0).astype(BF16)

    def reset():
        m_ref[...] = jnp.full(m_ref.shape, MASK_VALUE, F32)
        l_ref[...] = jnp.zeros(l_ref.shape, F32)
        acc_ref[...] = jnp.zeros(acc_ref.shape, F32)

    def step(j, diag_block):
        start = pl.multiple_of(j * tk, tk)
        for g in range(heads):
            kb = _head(k_ref, g, pl.ds(start, tk))
            vb = _head(v_ref, g, pl.ds(start, tk))
            s = lax.dot_general(qs_ref[g], kb, NT_DIMS, preferred_element_type=F32)
            if diag_block is not None:
                row = lax.broadcasted_iota(jnp.int32, s.shape, 0)
                col = lax.broadcasted_iota(jnp.int32, s.shape, 1) + diag_block * tk
                row = jnp.where(row >= tq, row - tq, row)
                s = jnp.where(col <= row, s, MASK_VALUE)
            m_prev = m_ref[g]
            m_new = jnp.maximum(m_prev, jnp.max(s, axis=1, keepdims=True))
            alpha = jnp.exp2(m_prev - m_new)
            p = jnp.exp2(s - _lane_tile(m_new, tk))
            l_ref[g] = alpha * l_ref[g] + jnp.sum(p, axis=1, keepdims=True)
            acc_ref[g] = alpha * acc_ref[g] + jnp.dot(p.astype(BF16), vb,
                                                      preferred_element_type=F32)
            m_ref[g] = m_new

    def frozen_step(j):
        start = pl.multiple_of(j * tk, tk)
        for g in range(heads):
            kb = _head(k_ref, g, pl.ds(start, tk))
            vb = _head(v_ref, g, pl.ds(start, tk))
            s = lax.dot_general(qs_ref[g], kb, NT_DIMS, preferred_element_type=F32)
            p = jnp.exp2(s - _lane_tile(m_ref[g], tk))
            l_ref[g] += jnp.sum(p, axis=1, keepdims=True)
            acc_ref[g] += jnp.dot(p.astype(BF16), vb, preferred_element_type=F32)

    def diagonal():
        for u in range(sub):
            step(i * sub + u, u)

    def finish():
        lp = lam_ref[...]
        lam = (jnp.exp(jnp.sum(lp[0:1] * lp[1:2], axis=1, keepdims=True))
               - jnp.exp(jnp.sum(lp[2:3] * lp[3:4], axis=1, keepdims=True)) + lam_init)
        for g in range(heads):
            o = acc_ref[g] / l_ref[g]
            d = o[0:tq] - lam * o[tq:rows]
            y = _rms(d, subln_ref[...]) * (1.0 - lam_init)
            o_ref[:, g * LANES:(g + 1) * LANES] = y.astype(o_ref.dtype)

    n_full = i * sub
    reset()
    diagonal()

    def pair(t, carry):
        frozen_step(2 * t)
        frozen_step(2 * t + 1)
        return carry

    lax.fori_loop(0, n_full // 2, pair, 0)

    @pl.when(n_full % 2 == 1)
    def _():
        frozen_step(n_full - 1)

    finite = jnp.minimum(jnp.min(jnp.where(l_ref[...] < FINITE_BOUND, 1.0, 0.0)),
                         jnp.min(jnp.where(jnp.abs(acc_ref[...]) < FINITE_BOUND, 1.0, 0.0)))

    @pl.when(finite > 0.5)
    def _():
        finish()

    @pl.when(finite <= 0.5)
    def _():
        reset()

        def body(j, carry):
            step(j, None)
            return carry

        lax.fori_loop(0, n_full, body, 0)
        diagonal()
        finish()


def _diff_attn(proj, lam_params, subln, lam_init, *, tq=512, tk=512, heads=4):
    s = proj.shape[0]
    width = A_HEADS * LANES
    hw = heads * LANES
    groups = A_HEADS // heads
    kern = functools.partial(_diff_attn_kernel, tq=tq, tk=tk, heads=heads, lam_init=lam_init)
    return pl.pallas_call(
        kern,
        grid=(groups, s // tq),
        in_specs=[pl.BlockSpec((4, ROPE_DIM), lambda h, i: (0, 0)),
                  pl.BlockSpec((1, LANES), lambda h, i: (0, 0)),
                  pl.BlockSpec((tq, hw), lambda h, i: (i, h)),
                  pl.BlockSpec((s, hw), lambda h, i: (0, groups + h),
                               pipeline_mode=pl.Buffered(1)),
                  pl.BlockSpec((s, hw), lambda h, i: (0, 2 * groups + h),
                               pipeline_mode=pl.Buffered(1))],
        out_specs=pl.BlockSpec((tq, hw), lambda h, i: (i, h)),
        out_shape=jax.ShapeDtypeStruct((s, width), BF16),
        scratch_shapes=[pltpu.VMEM((heads, 2 * tq, LANES), BF16)]
                       + [pltpu.VMEM((heads, 2 * tq, LANES), F32)] * 3,
        compiler_params=_params(("arbitrary", "arbitrary"), 56),
        name="diff_attn",
    )(lam_params, subln.reshape(1, LANES), proj, proj, proj)


def _stickbreak_kernel(q_ref, k_ref, v_ref, o_ref, tri_ref, carry_ref, acc_ref, *,
                       tq, tk, heads, scale):
    h = pl.program_id(0)
    i = pl.program_id(1)
    sub = tq // tk

    @pl.when((h == 0) & (i == 0))
    def _():
        r = lax.broadcasted_iota(jnp.int32, tri_ref.shape, 0)
        c = lax.broadcasted_iota(jnp.int32, tri_ref.shape, 1)
        tri_ref[...] = jnp.where(r > c, 1.0, 0.0).astype(BF16)

    carry_ref[...] = jnp.zeros(carry_ref.shape, F32)
    acc_ref[...] = jnp.zeros(acc_ref.shape, F32)

    def step(j, diag_block):
        start = pl.multiple_of(j * tk, tk)
        for g in range(heads):
            kb = _head(k_ref, g, pl.ds(start, tk))
            vb = _head(v_ref, g, pl.ds(start, tk))
            qk = lax.dot_general(_head(q_ref, g), kb, NT_DIMS, preferred_element_type=F32)
            z = qk * scale
            nz = qk * (-scale)
            lb = jnp.minimum(nz, 0.0) - jnp.log2(1.0 + jnp.exp2(jnp.minimum(z, nz)))
            if diag_block is not None:
                row = lax.broadcasted_iota(jnp.int32, z.shape, 0)
                col = lax.broadcasted_iota(jnp.int32, z.shape, 1) + diag_block * tk
                strict = col < row
                lb = jnp.where(strict, lb, 0.0)
            later = jnp.dot(lb.astype(BF16), tri_ref[...], preferred_element_type=F32)
            e = z + lb + later + _lane_tile(carry_ref[g], tk)
            if diag_block is not None:
                e = jnp.where(strict, e, MASK_VALUE)
            w = jnp.exp2(e)
            acc_ref[g] += jnp.dot(w.astype(BF16), vb, preferred_element_type=F32)
            carry_ref[g] += jnp.sum(lb, axis=1, keepdims=True)

    for u in reversed(range(sub)):
        step(i * sub + u, u)

    def more(state):
        j, top = state
        return (j >= 0) & (top > EXP2_UNDERFLOW)

    def body(state):
        j, _ = state
        step(j, None)
        return j - 1, jnp.max(carry_ref[...])

    lax.while_loop(more, body, (i * sub - 1, jnp.float32(0.0)))
    for g in range(heads):
        o_ref[:, g * LANES:(g + 1) * LANES] = acc_ref[g].astype(o_ref.dtype)


def _stickbreak(proj, *, first_col, tq=512, tk=256, heads=4):
    s = proj.shape[0]
    width = B_HEADS * B_HEAD_DIM
    hw = heads * LANES
    groups = B_HEADS // heads
    c0 = first_col // hw
    kern = functools.partial(_stickbreak_kernel, tq=tq, tk=tk, heads=heads,
                             scale=B_HEAD_DIM ** -0.5 * math.log2(math.e))
    return pl.pallas_call(
        kern,
        grid=(groups, s // tq),
        in_specs=[pl.BlockSpec((tq, hw), lambda h, i: (i, c0 + h)),
                  pl.BlockSpec((s, hw), lambda h, i: (0, c0 + groups + h)),
                  pl.BlockSpec((s, hw), lambda h, i: (0, c0 + 2 * groups + h))],
        out_specs=pl.BlockSpec((tq, hw), lambda h, i: (i, h)),
        out_shape=jax.ShapeDtypeStruct((s, width), BF16),
        scratch_shapes=[pltpu.VMEM((tk, tk), BF16),
                        pltpu.VMEM((heads, tq, LANES), F32),
                        pltpu.VMEM((heads, tq, LANES), F32)],
        compiler_params=_params(("arbitrary", "arbitrary"), 56),
        name="stickbreak",
    )(proj, proj, proj)


def _swa_kernel(sinks_ref, q_ref, kp_ref, kc_ref, vp_ref, vc_ref, o_ref, *, blocks):
    hk = pl.program_id(0)
    n = pl.program_id(1)
    w = WINDOW
    half = LANES // 2

    def lane_halves(prev_ref, cur_ref):
        both = jnp.concatenate([prev_ref[...], cur_ref[...]], axis=0).astype(F32)
        low = lax.broadcasted_iota(jnp.int32, both.shape, 1) < half
        return jnp.where(low, both, 0.0).astype(BF16), jnp.where(low, 0.0, both).astype(BF16)

    k_lo, k_hi = lane_halves(kp_ref, kc_ref)
    v_lo, v_hi = lane_halves(vp_ref, vc_ref)

    qi = lax.broadcasted_iota(jnp.int32, (w, 2 * w), 0)
    ki = lax.broadcasted_iota(jnp.int32, (w, 2 * w), 1)
    band = (ki > qi) & (ki <= qi + w)
    first_real = jnp.where(n > 0, 0, w)
    band_first = band & (ki >= first_real)
    lane_o = lax.broadcasted_iota(jnp.int32, (w, LANES), 1)
    log2e = math.log2(math.e)

    for b in range(blocks):
        keys = slice(b * w, (b + 2) * w)
        kbd = jnp.concatenate([k_lo[keys], k_hi[keys]], axis=0)
        vbd = jnp.concatenate([v_lo[keys], v_hi[keys]], axis=0)
        mask = band_first if b == 0 else band
        pairs = C_GROUP // 2
        qs = jnp.concatenate([q_ref[b * w:(b + 1) * w, p * LANES:(p + 1) * LANES]
                              for p in range(pairs)], axis=0)
        s = lax.dot_general(qs, kbd, NT_DIMS, preferred_element_type=F32)
        probs, scales = [], []
        for p in range(pairs):
            row_p, inv_den = [], []
            for t in range(2):
                st = jnp.where(mask, s[p * w:(p + 1) * w, t * 2 * w:(t + 1) * 2 * w], MASK_VALUE)
                sink = sinks_ref[hk * C_GROUP + 2 * p + t] * log2e
                m = jnp.maximum(jnp.max(st, axis=1, keepdims=True), sink)
                pt = jnp.exp2(st - m)
                den = jnp.sum(pt, axis=1, keepdims=True) + jnp.exp2(sink - m)
                row_p.append(pt.astype(BF16))
                inv_den.append(1.0 / den)
            probs.append(jnp.concatenate(row_p, axis=1))
            scales.append(jnp.where(lane_o < half, inv_den[0], inv_den[1]))
        o = jnp.dot(jnp.concatenate(probs, axis=0), vbd, preferred_element_type=F32)
        outs = [o[p * w:(p + 1) * w] * scales[p] for p in range(pairs)]
        o_ref[b * w:(b + 1) * w, :] = jnp.concatenate(outs, axis=1).astype(o_ref.dtype)


def _swa(proj, sinks, *, blocks=4):
    s = proj.shape[0]
    w = WINDOW
    tq = blocks * w
    qw = C_GROUP * C_HEAD_DIM
    k0 = C_Q_HEADS * C_HEAD_DIM // LANES
    v0 = k0 + C_KV_HEADS
    prev = lambda n: jnp.maximum(n * blocks - 1, 0)
    return pl.pallas_call(
        functools.partial(_swa_kernel, blocks=blocks),
        grid=(C_KV_HEADS, s // tq),
        in_specs=[pl.BlockSpec(memory_space=pltpu.SMEM),
                  pl.BlockSpec((tq, qw), lambda h, n: (n, h)),
                  pl.BlockSpec((w, LANES), lambda h, n: (prev(n), k0 + h)),
                  pl.BlockSpec((tq, LANES), lambda h, n: (n, k0 + h)),
                  pl.BlockSpec((w, LANES), lambda h, n: (prev(n), v0 + h)),
                  pl.BlockSpec((tq, LANES), lambda h, n: (n, v0 + h))],
        out_specs=pl.BlockSpec((tq, qw), lambda h, n: (n, h)),
        out_shape=jax.ShapeDtypeStruct((s, C_Q_HEADS * C_HEAD_DIM), BF16),
        compiler_params=_params(("arbitrary", "arbitrary"), 16),
        name="swa_sink",
    )(sinks, proj, proj, proj, proj, proj)


def _out_proj_kernel(*refs, n_in):
    a_refs = refs[:n_in]
    w_ref, x_ref, o_ref, wb_ref = refs[n_in:]

    @pl.when(pl.program_id(0) == 0)
    def _():
        wb_ref[...] = w_ref[...].astype(BF16)

    acc = x_ref[...]
    row = 0
    for a_ref in a_refs:
        ka = a_ref.shape[1]
        acc = acc + jnp.dot(a_ref[...], wb_ref[row:row + ka, :], preferred_element_type=F32)
        row += ka
    o_ref[...] = acc


def _out_proj(acts, w, layer, x, *, tm=512):
    s, d = x.shape
    n_in = len(acts)
    kw = w.shape[1]
    a_specs = [pl.BlockSpec((tm, a.shape[1]), lambda i: (i, 0)) for a in acts]
    w_spec = pl.BlockSpec((None, kw, d), lambda i: (layer, 0, 0), pipeline_mode=pl.Buffered(1))
    return pl.pallas_call(
        functools.partial(_out_proj_kernel, n_in=n_in),
        grid=(s // tm,),
        in_specs=a_specs + [w_spec, pl.BlockSpec((tm, d), lambda i: (i, 0))],
        out_specs=pl.BlockSpec((tm, d), lambda i: (i, 0)),
        out_shape=jax.ShapeDtypeStruct((s, d), F32),
        scratch_shapes=[pltpu.VMEM((kw, d), BF16)],
        compiler_params=_params(("arbitrary",), 56),
        name="out_proj",
    )(*acts, w, x)


def _mlp_kernel(x_ref, g_ref, wu_ref, wd_ref, gf_ref, o_ref, xn_ref, *, final_norm):
    f = pl.program_id(1)

    @pl.when(f == 0)
    def _():
        x = x_ref[...]
        xn_ref[...] = _rms(x, g_ref[...]).astype(BF16)
        o_ref[...] = x

    hid = jnp.dot(xn_ref[...], wu_ref[...].astype(BF16), preferred_element_type=F32)
    hid = jnp.square(jnp.maximum(hid, 0.0)).astype(BF16)
    o_ref[...] += jnp.dot(hid, wd_ref[...].astype(BF16), preferred_element_type=F32)

    if final_norm:
        @pl.when(f == pl.num_programs(1) - 1)
        def _():
            o_ref[...] = _rms(o_ref[...], gf_ref[...])


def _mlp(x, g, w_up, w_down, layer, g_final, *, final_norm, tm=1024, tf=512):
    s, d = x.shape
    dff = w_up.shape[2]
    return pl.pallas_call(
        functools.partial(_mlp_kernel, final_norm=final_norm),
        grid=(s // tm, dff // tf),
        in_specs=[pl.BlockSpec((tm, d), lambda i, f: (i, 0)),
                  pl.BlockSpec((1, d), lambda i, f: (0, 0)),
                  pl.BlockSpec((None, d, tf), lambda i, f: (layer, 0, f)),
                  pl.BlockSpec((None, tf, d), lambda i, f: (layer, f, 0)),
                  pl.BlockSpec((1, d), lambda i, f: (0, 0))],
        out_specs=pl.BlockSpec((tm, d), lambda i, f: (i, 0)),
        out_shape=jax.ShapeDtypeStruct((s, d), F32),
        scratch_shapes=[pltpu.VMEM((tm, d), BF16)],
        compiler_params=_params(("arbitrary", "arbitrary"), 60),
        name="mlp",
    )(x, g.reshape(1, d), w_up, w_down, g_final.reshape(1, d))


def _dup_kv_columns(w_in_c):
    layers, d, _ = w_in_c.shape
    c_q = C_Q_HEADS * C_HEAD_DIM
    kv_heads = 2 * C_KV_HEADS
    wkv = w_in_c[:, :, c_q:].reshape(layers, d, kv_heads, 1, C_HEAD_DIM)
    wkv = jnp.broadcast_to(wkv, (layers, d, kv_heads, 2, C_HEAD_DIM))
    return jnp.concatenate([w_in_c[:, :, :c_q], wkv.reshape(layers, d, -1)], axis=2)


def kernel(x, positions, norm_mix, norm_mlp, norm_final, w_in_ab, w_out_ab, lambda_q1, lambda_k1,
           lambda_q2, lambda_k2, diff_subln, w_in_c, w_out_c, sinks, w_up, w_down):
    b, s, d = x.shape
    assert b == 1
    depth = norm_mix.shape[0]
    h = x.reshape(s, d)
    cos, sin = _rope_tables(positions)
    w_in_c_dup = _dup_kv_columns(w_in_c)

    a_width = A_HEADS * LANES
    c_q = C_Q_HEADS * C_HEAD_DIM
    c_kv = C_KV_HEADS * C_HEAD_DIM
    for layer in range(depth):
        j = layer // 2
        if layer % 2 == 0:
            proj = _norm_proj(h, norm_mix[layer], w_in_ab, j, cos, sin,
                              scaled_cols=a_width, rope_cols=2 * a_width,
                              scale=ROPE_DIM ** -0.5 * math.log2(math.e))
            lam_init = 0.8 - 0.6 * math.exp(-0.3 * layer)
            lam_params = jnp.stack([lambda_q1[j], lambda_k1[j], lambda_q2[j], lambda_k2[j]])
            oa = _diff_attn(proj, lam_params, diff_subln[j], lam_init)
            ob = _stickbreak(proj, first_col=3 * a_width)
            h = _out_proj([oa, ob], w_out_ab, j, h)
        else:
            proj = _norm_proj(h, norm_mix[layer], w_in_c_dup, j, cos, sin,
                              scaled_cols=c_q, rope_cols=c_q + 2 * c_kv,
                              scale=C_HEAD_DIM ** -0.5 * math.log2(math.e))
            oc = _swa(proj, sinks[j])
            h = _out_proj([oc], w_out_c, j, h)
        h = _mlp(h, norm_mlp[layer], w_up, w_down, layer,
                 norm_final, final_norm=(layer == depth - 1))
    return h.reshape(b, s, d)
```

```python
import functools
import math

import jax
import jax.numpy as jnp
from jax import lax
from jax.experimental import pallas as pl
from jax.experimental.pallas import tpu as pltpu

F32 = jnp.float32
BF16 = jnp.bfloat16

EPS = 1e-6
ROPE_THETA = 10000.0
LANES = 128
ROPE_DIM = 64
A_HEADS = 8
B_HEADS = 8
B_HEAD_DIM = 128
C_Q_HEADS = 32
C_KV_HEADS = 4
C_GROUP = C_Q_HEADS // C_KV_HEADS
C_HEAD_DIM = 64
WINDOW = 128
MASK_VALUE = -1e30
EXP2_UNDERFLOW = -160.0
FINITE_BOUND = 1e30
MIB = 1024 * 1024

NT_DIMS = (((1,), (1,)), ((), ()))


def _params(semantics, vmem_mib):
    return pltpu.CompilerParams(dimension_semantics=semantics,
                                vmem_limit_bytes=vmem_mib * MIB)


def _rms(x, g):
    ms = jnp.mean(x * x, axis=-1, keepdims=True)
    return x * lax.rsqrt(ms + EPS) * g


def _lane_tile(a, width):
    reps = width // LANES
    return a if reps == 1 else jnp.concatenate([a] * reps, axis=1)


def _head(ref, g, rows=slice(None)):
    return ref[rows, g * LANES:(g + 1) * LANES]


def _rope_table_kernel(pos_ref, invf_ref, cos_ref, sin_ref):
    ang = pos_ref[...].astype(F32) * invf_ref[...]
    lane = lax.broadcasted_iota(jnp.int32, ang.shape, 1)
    sign = jnp.where((lane & (ROPE_DIM // 2)) == 0, -1.0, 1.0)
    cos_ref[...] = jnp.cos(ang)
    sin_ref[...] = jnp.sin(ang) * sign


def _rope_tables(positions):
    s = positions.shape[-1]
    tm = 1024
    inv_freq = ROPE_THETA ** (-jnp.arange(0, ROPE_DIM, 2, dtype=F32) / ROPE_DIM)
    invf = jnp.tile(inv_freq, LANES // (ROPE_DIM // 2)).reshape(1, LANES)
    pos = positions.reshape(s, 1)
    return pl.pallas_call(
        _rope_table_kernel,
        grid=(s // tm,),
        in_specs=[pl.BlockSpec((tm, 1), lambda i: (i, 0)),
                  pl.BlockSpec((1, LANES), lambda i: (0, 0))],
        out_specs=[pl.BlockSpec((tm, LANES), lambda i: (i, 0))] * 2,
        out_shape=[jax.ShapeDtypeStruct((s, LANES), F32)] * 2,
        compiler_params=_params(("arbitrary",), 16),
        name="rope_tables",
    )(pos, invf)


def _rope(x, cos, ss):
    half = ROPE_DIM // 2
    lane = lax.broadcasted_iota(jnp.int32, cos.shape, 1)
    low = (lane & half) == 0
    outs = []
    for c in range(x.shape[1] // LANES):
        xc = x[:, c * LANES:(c + 1) * LANES]
        partner = jnp.where(low, pltpu.roll(xc, LANES - half, 1), pltpu.roll(xc, half, 1))
        outs.append(xc * cos + partner * ss)
    return outs[0] if len(outs) == 1 else jnp.concatenate(outs, axis=1)


PROJ_CHUNK = 512


def _norm_proj_kernel(x_ref, g_ref, w_ref, cos_ref, sin_ref, o_ref, xn_ref, *, tile_runs, scale):
    j = pl.program_id(1)

    def chunk(xn, c, kind):
        cols = slice(c * PROJ_CHUNK, (c + 1) * PROJ_CHUNK)
        acc = jnp.dot(xn, w_ref[:, cols].astype(BF16), preferred_element_type=F32)
        if kind != "plain":
            acc = _rope(acc, cos_ref[...], sin_ref[...])
        if kind == "scaled":
            acc = acc * scale
        o_ref[:, cols] = acc.astype(o_ref.dtype)

    @pl.when(j == 0)
    def _():
        xn = _rms(x_ref[...], g_ref[...]).astype(BF16)
        xn_ref[...] = xn
        for c, kind in enumerate(tile_runs[0][2]):
            chunk(xn, c, kind)

    for first, end, kinds in tile_runs:
        first = max(first, 1)
        if first < end:
            @pl.when((j >= first) & (j < end))
            def _(kinds=kinds):
                for c, kind in enumerate(kinds):
                    chunk(xn_ref[...], c, kind)


def _tile_runs(n, tn, scaled_cols, rope_cols):
    def kind(col):
        return "scaled" if col < scaled_cols else "rope" if col < rope_cols else "plain"
    patterns = [tuple(kind(t * tn + c * PROJ_CHUNK) for c in range(tn // PROJ_CHUNK))
                for t in range(n // tn)]
    runs, first = [], 0
    for t in range(1, len(patterns) + 1):
        if t == len(patterns) or patterns[t] != patterns[first]:
            runs.append((first, t, patterns[first]))
            first = t
    return tuple(runs)


def _norm_proj(x, g, w, layer, cos, sin, *, scaled_cols, rope_cols, scale, tm=1024, tn=1024):
    s, d = x.shape
    n = w.shape[2]
    kern = functools.partial(_norm_proj_kernel, scale=scale,
                             tile_runs=_tile_runs(n, tn, scaled_cols, rope_cols))
    return pl.pallas_call(
        kern,
        grid=(s // tm, n // tn),
        in_specs=[pl.BlockSpec((tm, d), lambda i, j: (i, 0)),
                  pl.BlockSpec((1, d), lambda i, j: (0, 0)),
                  pl.BlockSpec((None, d, tn), lambda i, j: (layer, 0, j)),
                  pl.BlockSpec((tm, LANES), lambda i, j: (i, 0)),
                  pl.BlockSpec((tm, LANES), lambda i, j: (i, 0))],
        out_specs=pl.BlockSpec((tm, tn), lambda i, j: (i, j)),
        out_shape=jax.ShapeDtypeStruct((s, n), BF16),
        scratch_shapes=[pltpu.VMEM((tm, d), BF16)],
        compiler_params=_params(("arbitrary", "arbitrary"), 56),
        name="norm_proj",
    )(x, g.reshape(1, d), w, cos, sin)


def _diff_attn_kernel(lam_ref, subln_ref, q_ref, k_ref, v_ref, o_ref,
                      qs_ref, m_ref, l_ref, acc_ref, *, tq, tk, heads, lam_init):
    i = pl.program_id(1)
    rows = 2 * tq
    half = LANES // 2
    hq = tq // 2
    sub = tq // tk

    for g in range(heads):
        q = _head(q_ref, g).astype(F32)
        lane = lax.broadcasted_iota(jnp.int32, q.shape, 1)
        q1 = jnp.where(lane < half, q, 0.0).astype(BF16)
        q2 = jnp.where(lane >= half, q, 0.0).astype(BF16)
        for part, src in enumerate((q1[0:hq], q2[0:hq], q1[hq:tq], q2[hq:tq])):
            qs_ref[g, part * hq:(part + 1) * hq, :] = src

    def reset():
        m_ref[...] = jnp.full(m_ref.shape, MASK_VALUE, F32)
        l_ref[...] = jnp.zeros(l_ref.shape, F32)
        acc_ref[...] = jnp.zeros(acc_ref.shape, F32)

    def step(start, width, diag_half):
        first = 0 if diag_half is None else diag_half * tq
        rs = slice(first, rows)
        for g in range(heads):
            kb = _head(k_ref, g, pl.ds(start, width))
            vb = _head(v_ref, g, pl.ds(start, width))
            s = lax.dot_general(qs_ref[g, rs], kb, NT_DIMS, preferred_element_type=F32)
            if diag_half is not None:
                r = lax.broadcasted_iota(jnp.int32, s.shape, 0) + first
                qrow = (r & (hq - 1)) + jnp.where(r >= tq, hq, 0)
                col = lax.broadcasted_iota(jnp.int32, s.shape, 1) + diag_half * hq
                s = jnp.where(col <= qrow, s, MASK_VALUE)
            m_prev = m_ref[g, rs]
            m_new = jnp.maximum(m_prev, jnp.max(s, axis=1, keepdims=True))
            alpha = jnp.exp2(m_prev - m_new)
            p = jnp.exp2(s - _lane_tile(m_new, width))
            l_ref[g, rs] = alpha * l_ref[g, rs] + jnp.sum(p, axis=1, keepdims=True)
            acc_ref[g, rs] = alpha * acc_ref[g, rs] + jnp.dot(p.astype(BF16), vb,
                                                              preferred_element_type=F32)
            m_ref[g, rs] = m_new

    def frozen_step(j):
        start = pl.multiple_of(j * tk, tk)
        for g in range(heads):
            kb = _head(k_ref, g, pl.ds(start, tk))
            vb = _head(v_ref, g, pl.ds(start, tk))
            s = lax.dot_general(qs_ref[g], kb, NT_DIMS, preferred_element_type=F32)
            p = jnp.exp2(s - _lane_tile(m_ref[g], tk))
            l_ref[g] += jnp.sum(p, axis=1, keepdims=True)
            acc_ref[g] += jnp.dot(p.astype(BF16), vb, preferred_element_type=F32)

    def diagonal():
        for u in range(2):
            step(pl.multiple_of(i * tq + u * hq, hq), hq, u)

    def finish():
        lp = lam_ref[...]
        lam = (jnp.exp(jnp.sum(lp[0:1] * lp[1:2], axis=1, keepdims=True))
               - jnp.exp(jnp.sum(lp[2:3] * lp[3:4], axis=1, keepdims=True)) + lam_init)
        for g in range(heads):
            o = acc_ref[g] / l_ref[g]
            for u in range(2):
                d = o[2 * u * hq:(2 * u + 1) * hq] - lam * o[(2 * u + 1) * hq:(2 * u + 2) * hq]
                y = _rms(d, subln_ref[...]) * (1.0 - lam_init)
                o_ref[u * hq:(u + 1) * hq, g * LANES:(g + 1) * LANES] = y.astype(o_ref.dtype)

    n_full = i * sub
    reset()
    diagonal()

    def pair(t, carry):
        frozen_step(2 * t)
        frozen_step(2 * t + 1)
        return carry

    lax.fori_loop(0, n_full // 2, pair, 0)

    @pl.when(n_full % 2 == 1)
    def _():
        frozen_step(n_full - 1)

    finite = jnp.minimum(jnp.min(jnp.where(l_ref[...] < FINITE_BOUND, 1.0, 0.0)),
                         jnp.min(jnp.where(jnp.abs(acc_ref[...]) < FINITE_BOUND, 1.0, 0.0)))

    @pl.when(finite > 0.5)
    def _():
        finish()

    @pl.when(finite <= 0.5)
    def _():
        reset()

        def body(j, carry):
            step(pl.multiple_of(j * tk, tk), tk, None)
            return carry

        lax.fori_loop(0, n_full, body, 0)
        diagonal()
        finish()


def _diff_attn(proj, lam_params, subln, lam_init, *, tq=512, tk=512, heads=4):
    s = proj.shape[0]
    width = A_HEADS * LANES
    hw = heads * LANES
    groups = A_HEADS // heads
    kern = functools.partial(_diff_attn_kernel, tq=tq, tk=tk, heads=heads, lam_init=lam_init)
    return pl.pallas_call(
        kern,
        grid=(groups, s // tq),
        in_specs=[pl.BlockSpec((4, ROPE_DIM), lambda h, i: (0, 0)),
                  pl.BlockSpec((1, LANES), lambda h, i: (0, 0)),
                  pl.BlockSpec((tq, hw), lambda h, i: (i, h)),
                  pl.BlockSpec((s, hw), lambda h, i: (0, groups + h),
                               pipeline_mode=pl.Buffered(1)),
                  pl.BlockSpec((s, hw), lambda h, i: (0, 2 * groups + h),
                               pipeline_mode=pl.Buffered(1))],
        out_specs=pl.BlockSpec((tq, hw), lambda h, i: (i, h)),
        out_shape=jax.ShapeDtypeStruct((s, width), BF16),
        scratch_shapes=[pltpu.VMEM((heads, 2 * tq, LANES), BF16)]
                       + [pltpu.VMEM((heads, 2 * tq, LANES), F32)] * 3,
        compiler_params=_params(("arbitrary", "arbitrary"), 56),
        name="diff_attn",
    )(lam_params, subln.reshape(1, LANES), proj, proj, proj)


def _stickbreak_kernel(q_ref, k_ref, v_ref, o_ref, tri_ref, carry_ref, acc_ref, *,
                       tq, tk, heads, scale):
    h = pl.program_id(0)
    i = pl.program_id(1)
    sub = tq // tk

    @pl.when((h == 0) & (i == 0))
    def _():
        r = lax.broadcasted_iota(jnp.int32, tri_ref.shape, 0)
        c = lax.broadcasted_iota(jnp.int32, tri_ref.shape, 1)
        tri_ref[...] = jnp.where(r > c, 1.0, 0.0).astype(BF16)

    carry_ref[...] = jnp.zeros(carry_ref.shape, F32)
    acc_ref[...] = jnp.zeros(acc_ref.shape, F32)

    def step(j, diag_block):
        start = pl.multiple_of(j * tk, tk)
        rows = slice(None) if diag_block is None else slice(diag_block * tk, tq)
        for g in range(heads):
            kb = _head(k_ref, g, pl.ds(start, tk))
            vb = _head(v_ref, g, pl.ds(start, tk))
            qk = lax.dot_general(_head(q_ref, g, rows), kb, NT_DIMS, preferred_element_type=F32)
            z = qk * scale
            nz = qk * (-scale)
            lb = jnp.minimum(nz, 0.0) - jnp.log2(1.0 + jnp.exp2(jnp.minimum(z, nz)))
            if diag_block is not None:
                row = lax.broadcasted_iota(jnp.int32, z.shape, 0)
                col = lax.broadcasted_iota(jnp.int32, z.shape, 1)
                strict = col < row
                lb = jnp.where(strict, lb, 0.0)
            later = jnp.dot(lb.astype(BF16), tri_ref[...], preferred_element_type=F32)
            e = z + lb + later + _lane_tile(carry_ref[g, rows], tk)
            if diag_block is not None:
                e = jnp.where(strict, e, MASK_VALUE)
            w = jnp.exp2(e)
            acc_ref[g, rows] += jnp.dot(w.astype(BF16), vb, preferred_element_type=F32)
            carry_ref[g, rows] += jnp.sum(lb, axis=1, keepdims=True)

    for u in reversed(range(sub)):
        step(i * sub + u, u)

    def more(state):
        j, top = state
        return (j >= 0) & (top > EXP2_UNDERFLOW)

    def body(state):
        j, _ = state
        step(j, None)
        return j - 1, jnp.max(carry_ref[...])

    lax.while_loop(more, body, (i * sub - 1, jnp.float32(0.0)))
    for g in range(heads):
        o_ref[:, g * LANES:(g + 1) * LANES] = acc_ref[g].astype(o_ref.dtype)


def _stickbreak(proj, *, first_col, tq=512, tk=256, heads=4):
    s = proj.shape[0]
    width = B_HEADS * B_HEAD_DIM
    hw = heads * LANES
    groups = B_HEADS // heads
    c0 = first_col // hw
    kern = functools.partial(_stickbreak_kernel, tq=tq, tk=tk, heads=heads,
                             scale=B_HEAD_DIM ** -0.5 * math.log2(math.e))
    return pl.pallas_call(
        kern,
        grid=(groups, s // tq),
        in_specs=[pl.BlockSpec((tq, hw), lambda h, i: (i, c0 + h)),
                  pl.BlockSpec((s, hw), lambda h, i: (0, c0 + groups + h)),
                  pl.BlockSpec((s, hw), lambda h, i: (0, c0 + 2 * groups + h))],
        out_specs=pl.BlockSpec((tq, hw), lambda h, i: (i, h)),
        out_shape=jax.ShapeDtypeStruct((s, width), BF16),
        scratch_shapes=[pltpu.VMEM((tk, tk), BF16),
                        pltpu.VMEM((heads, tq, LANES), F32),
                        pltpu.VMEM((heads, tq, LANES), F32)],
        compiler_params=_params(("arbitrary", "arbitrary"), 56),
        name="stickbreak",
    )(proj, proj, proj)


def _swa_kernel(sinks_ref, q_ref, kp_ref, kc_ref, vp_ref, vc_ref, o_ref, *, blocks):
    hk = pl.program_id(0)
    n = pl.program_id(1)
    w = WINDOW
    half = LANES // 2

    def lane_halves(prev_ref, cur_ref):
        both = jnp.concatenate([prev_ref[...], cur_ref[...]], axis=0).astype(F32)
        low = lax.broadcasted_iota(jnp.int32, both.shape, 1) < half
        return jnp.where(low, both, 0.0).astype(BF16), jnp.where(low, 0.0, both).astype(BF16)

    k_lo, k_hi = lane_halves(kp_ref, kc_ref)
    v_lo, v_hi = lane_halves(vp_ref, vc_ref)

    qi = lax.broadcasted_iota(jnp.int32, (w, 2 * w), 0)
    ki = lax.broadcasted_iota(jnp.int32, (w, 2 * w), 1)
    band = (ki > qi) & (ki <= qi + w)
    first_real = jnp.where(n > 0, 0, w)
    band_first = band & (ki >= first_real)
    lane_o = lax.broadcasted_iota(jnp.int32, (w, LANES), 1)
    log2e = math.log2(math.e)

    for b in range(blocks):
        keys = slice(b * w, (b + 2) * w)
        kbd = jnp.concatenate([k_lo[keys], k_hi[keys]], axis=0)
        vbd = jnp.concatenate([v_lo[keys], v_hi[keys]], axis=0)
        mask = band_first if b == 0 else band
        pairs = C_GROUP // 2
        qs = jnp.concatenate([q_ref[b * w:(b + 1) * w, p * LANES:(p + 1) * LANES]
                              for p in range(pairs)], axis=0)
        s = lax.dot_general(qs, kbd, NT_DIMS, preferred_element_type=F32)
        probs, scales = [], []
        for p in range(pairs):
            row_p, inv_den = [], []
            for t in range(2):
                st = jnp.where(mask, s[p * w:(p + 1) * w, t * 2 * w:(t + 1) * 2 * w], MASK_VALUE)
                sink = sinks_ref[hk * C_GROUP + 2 * p + t] * log2e
                m = jnp.maximum(jnp.max(st, axis=1, keepdims=True), sink)
                pt = jnp.exp2(st - m)
                den = jnp.sum(pt, axis=1, keepdims=True) + jnp.exp2(sink - m)
                row_p.append(pt.astype(BF16))
                inv_den.append(1.0 / den)
            probs.append(jnp.concatenate(row_p, axis=1))
            scales.append(jnp.where(lane_o < half, inv_den[0], inv_den[1]))
        o = jnp.dot(jnp.concatenate(probs, axis=0), vbd, preferred_element_type=F32)
        outs = [o[p * w:(p + 1) * w] * scales[p] for p in range(pairs)]
        o_ref[b * w:(b + 1) * w, :] = jnp.concatenate(outs, axis=1).astype(o_ref.dtype)


def _swa(proj, sinks, *, blocks=4):
    s = proj.shape[0]
    w = WINDOW
    tq = blocks * w
    qw = C_GROUP * C_HEAD_DIM
    k0 = C_Q_HEADS * C_HEAD_DIM // LANES
    v0 = k0 + C_KV_HEADS
    prev = lambda n: jnp.maximum(n * blocks - 1, 0)
    return pl.pallas_call(
        functools.partial(_swa_kernel, blocks=blocks),
        grid=(C_KV_HEADS, s // tq),
        in_specs=[pl.BlockSpec(memory_space=pltpu.SMEM),
                  pl.BlockSpec((tq, qw), lambda h, n: (n, h)),
                  pl.BlockSpec((w, LANES), lambda h, n: (prev(n), k0 + h)),
                  pl.BlockSpec((tq, LANES), lambda h, n: (n, k0 + h)),
                  pl.BlockSpec((w, LANES), lambda h, n: (prev(n), v0 + h)),
                  pl.BlockSpec((tq, LANES), lambda h, n: (n, v0 + h))],
        out_specs=pl.BlockSpec((tq, qw), lambda h, n: (n, h)),
        out_shape=jax.ShapeDtypeStruct((s, C_Q_HEADS * C_HEAD_DIM), BF16),
        compiler_params=_params(("arbitrary", "arbitrary"), 16),
        name="swa_sink",
    )(sinks, proj, proj, proj, proj, proj)


def _out_proj_kernel(*refs, n_in):
    a_refs = refs[:n_in]
    w_ref, x_ref, o_ref, wb_ref = refs[n_in:]

    @pl.when(pl.program_id(0) == 0)
    def _():
        wb_ref[...] = w_ref[...].astype(BF16)

    acc = x_ref[...]
    row = 0
    for a_ref in a_refs:
        ka = a_ref.shape[1]
        acc = acc + jnp.dot(a_ref[...], wb_ref[row:row + ka, :], preferred_element_type=F32)
        row += ka
    o_ref[...] = acc


def _out_proj(acts, w, layer, x, *, tm=512):
    s, d = x.shape
    n_in = len(acts)
    kw = w.shape[1]
    a_specs = [pl.BlockSpec((tm, a.shape[1]), lambda i: (i, 0)) for a in acts]
    w_spec = pl.BlockSpec((None, kw, d), lambda i: (layer, 0, 0), pipeline_mode=pl.Buffered(1))
    return pl.pallas_call(
        functools.partial(_out_proj_kernel, n_in=n_in),
        grid=(s // tm,),
        in_specs=a_specs + [w_spec, pl.BlockSpec((tm, d), lambda i: (i, 0))],
        out_specs=pl.BlockSpec((tm, d), lambda i: (i, 0)),
        out_shape=jax.ShapeDtypeStruct((s, d), F32),
        scratch_shapes=[pltpu.VMEM((kw, d), BF16)],
        compiler_params=_params(("arbitrary",), 56),
        name="out_proj",
    )(*acts, w, x)


def _mlp_kernel(x_ref, g_ref, wu_ref, wd_ref, gf_ref, o_ref, xn_ref, *, final_norm):
    f = pl.program_id(1)

    def up_down(xn):
        hid = jnp.dot(xn, wu_ref[...].astype(BF16), preferred_element_type=F32)
        hid = jnp.square(jnp.maximum(hid, 0.0)).astype(BF16)
        return jnp.dot(hid, wd_ref[...].astype(BF16), preferred_element_type=F32)

    @pl.when(f == 0)
    def _():
        x = x_ref[...]
        xn = _rms(x, g_ref[...]).astype(BF16)
        xn_ref[...] = xn
        o_ref[...] = x + up_down(xn)

    @pl.when(f > 0)
    def _():
        o_ref[...] += up_down(xn_ref[...])

    if final_norm:
        @pl.when(f == pl.num_programs(1) - 1)
        def _():
            o_ref[...] = _rms(o_ref[...], gf_ref[...])


def _mlp(x, g, w_up, w_down, layer, g_final, *, final_norm, tm=1024, tf=512):
    s, d = x.shape
    dff = w_up.shape[2]
    return pl.pallas_call(
        functools.partial(_mlp_kernel, final_norm=final_norm),
        grid=(s // tm, dff // tf),
        in_specs=[pl.BlockSpec((tm, d), lambda i, f: (i, 0)),
                  pl.BlockSpec((1, d), lambda i, f: (0, 0)),
                  pl.BlockSpec((None, d, tf), lambda i, f: (layer, 0, f)),
                  pl.BlockSpec((None, tf, d), lambda i, f: (layer, f, 0)),
                  pl.BlockSpec((1, d), lambda i, f: (0, 0))],
        out_specs=pl.BlockSpec((tm, d), lambda i, f: (i, 0)),
        out_shape=jax.ShapeDtypeStruct((s, d), F32),
        scratch_shapes=[pltpu.VMEM((tm, d), BF16)],
        compiler_params=_params(("arbitrary", "arbitrary"), 60),
        name="mlp",
    )(x, g.reshape(1, d), w_up, w_down, g_final.reshape(1, d))


def _dup_kv_columns(w_in_c):
    layers, d, _ = w_in_c.shape
    c_q = C_Q_HEADS * C_HEAD_DIM
    kv_heads = 2 * C_KV_HEADS
    wkv = w_in_c[:, :, c_q:].reshape(layers, d, kv_heads, 1, C_HEAD_DIM)
    wkv = jnp.broadcast_to(wkv, (layers, d, kv_heads, 2, C_HEAD_DIM))
    return jnp.concatenate([w_in_c[:, :, :c_q], wkv.reshape(layers, d, -1)], axis=2)


def kernel(x, positions, norm_mix, norm_mlp, norm_final, w_in_ab, w_out_ab, lambda_q1, lambda_k1,
           lambda_q2, lambda_k2, diff_subln, w_in_c, w_out_c, sinks, w_up, w_down):
    b, s, d = x.shape
    assert b == 1
    depth = norm_mix.shape[0]
    h = x.reshape(s, d)
    cos, sin = _rope_tables(positions)
    w_in_c_dup = _dup_kv_columns(w_in_c)

    a_width = A_HEADS * LANES
    c_q = C_Q_HEADS * C_HEAD_DIM
    c_kv = C_KV_HEADS * C_HEAD_DIM
    for layer in range(depth):
        j = layer // 2
        if layer % 2 == 0:
            proj = _norm_proj(h, norm_mix[layer], w_in_ab, j, cos, sin,
                              scaled_cols=a_width, rope_cols=2 * a_width,
                              scale=ROPE_DIM ** -0.5 * math.log2(math.e))
            lam_init = 0.8 - 0.6 * math.exp(-0.3 * layer)
            lam_params = jnp.stack([lambda_q1[j], lambda_k1[j], lambda_q2[j], lambda_k2[j]])
            oa = _diff_attn(proj, lam_params, diff_subln[j], lam_init)
            ob = _stickbreak(proj, first_col=3 * a_width)
            h = _out_proj([oa, ob], w_out_ab, j, h)
        else:
            proj = _norm_proj(h, norm_mix[layer], w_in_c_dup, j, cos, sin,
                              scaled_cols=c_q, rope_cols=c_q + 2 * c_kv,
                              scale=C_HEAD_DIM ** -0.5 * math.log2(math.e))
            oc = _swa(proj, sinks[j])
            h = _out_proj([oc], w_out_c, j, h)
        h = _mlp(h, norm_mlp[layer], w_up, w_down, layer,
                 norm_final, final_norm=(layer == depth - 1))
    return h.reshape(b, s, d)
```

```python
import functools
import math

import jax
import jax.numpy as jnp
from jax import lax
from jax.experimental import pallas as pl
from jax.experimental.pallas import tpu as pltpu

F32 = jnp.float32
BF16 = jnp.bfloat16

EPS = 1e-6
ROPE_THETA = 10000.0
LANES = 128
ROPE_DIM = 64
A_HEADS = 8
B_HEADS = 8
B_HEAD_DIM = 128
C_Q_HEADS = 32
C_KV_HEADS = 4
C_GROUP = C_Q_HEADS // C_KV_HEADS
C_HEAD_DIM = 64
WINDOW = 128
MASK_VALUE = -1e30
EXP2_UNDERFLOW = -160.0
FINITE_BOUND = 1e30
MIB = 1024 * 1024

NT_DIMS = (((1,), (1,)), ((), ()))


def _params(semantics, vmem_mib):
    return pltpu.CompilerParams(dimension_semantics=semantics,
                                vmem_limit_bytes=vmem_mib * MIB)


def _rms(x, g):
    ms = jnp.mean(x * x, axis=-1, keepdims=True)
    return x * lax.rsqrt(ms + EPS) * g


def _lane_tile(a, width):
    reps = width // LANES
    return a if reps == 1 else jnp.concatenate([a] * reps, axis=1)


def _head(ref, g, rows=slice(None)):
    return ref[rows, g * LANES:(g + 1) * LANES]


def _rope_table_kernel(pos_ref, invf_ref, cos_ref, sin_ref):
    ang = pos_ref[...].astype(F32) * invf_ref[...]
    lane = lax.broadcasted_iota(jnp.int32, ang.shape, 1)
    sign = jnp.where((lane & (ROPE_DIM // 2)) == 0, -1.0, 1.0)
    cos_ref[...] = jnp.cos(ang)
    sin_ref[...] = jnp.sin(ang) * sign


def _rope_tables(positions):
    s = positions.shape[-1]
    tm = 1024
    inv_freq = ROPE_THETA ** (-jnp.arange(0, ROPE_DIM, 2, dtype=F32) / ROPE_DIM)
    invf = jnp.tile(inv_freq, LANES // (ROPE_DIM // 2)).reshape(1, LANES)
    pos = positions.reshape(s, 1)
    return pl.pallas_call(
        _rope_table_kernel,
        grid=(s // tm,),
        in_specs=[pl.BlockSpec((tm, 1), lambda i: (i, 0)),
                  pl.BlockSpec((1, LANES), lambda i: (0, 0))],
        out_specs=[pl.BlockSpec((tm, LANES), lambda i: (i, 0))] * 2,
        out_shape=[jax.ShapeDtypeStruct((s, LANES), F32)] * 2,
        compiler_params=_params(("arbitrary",), 16),
        name="rope_tables",
    )(pos, invf)


def _rope(x, cos, ss):
    half = ROPE_DIM // 2
    lane = lax.broadcasted_iota(jnp.int32, cos.shape, 1)
    low = (lane & half) == 0
    outs = []
    for c in range(x.shape[1] // LANES):
        xc = x[:, c * LANES:(c + 1) * LANES]
        partner = jnp.where(low, pltpu.roll(xc, LANES - half, 1), pltpu.roll(xc, half, 1))
        outs.append(xc * cos + partner * ss)
    return outs[0] if len(outs) == 1 else jnp.concatenate(outs, axis=1)


PROJ_CHUNK = 512


def _norm_proj_kernel(*refs, tile_runs, scale, has_tail):
    x_ref, g_ref, w_ref = refs[:3]
    tail_ref = refs[3] if has_tail else None
    cos_ref, sin_ref, o_ref, xn_ref = refs[-4:]
    j = pl.program_id(1)

    def chunk(xn, c, kind, from_tail):
        cols = slice(c * PROJ_CHUNK, (c + 1) * PROJ_CHUNK)
        wt = (tail_ref if from_tail else w_ref)[:, cols].astype(BF16)
        acc = jnp.dot(xn, wt, preferred_element_type=F32)
        if kind != "plain":
            acc = _rope(acc, cos_ref[...], sin_ref[...])
        if kind == "scaled":
            acc = acc * scale
        o_ref[:, cols] = acc.astype(o_ref.dtype)

    @pl.when(j == 0)
    def _():
        xn = _rms(x_ref[...], g_ref[...]).astype(BF16)
        xn_ref[...] = xn
        for c, kind in enumerate(tile_runs[0][2]):
            chunk(xn, c, kind, tile_runs[0][3])

    for first, end, kinds, from_tail in tile_runs:
        first = max(first, 1)
        if first < end:
            @pl.when((j >= first) & (j < end))
            def _(kinds=kinds, from_tail=from_tail):
                for c, kind in enumerate(kinds):
                    chunk(xn_ref[...], c, kind, from_tail)


def _tile_runs(n_tiles, tn, scaled_cols, rope_cols, has_tail):
    def kind(col):
        return "scaled" if col < scaled_cols else "rope" if col < rope_cols else "plain"
    patterns = [(tuple(kind(t * tn + c * PROJ_CHUNK) for c in range(tn // PROJ_CHUNK)),
                 has_tail and t == n_tiles - 1) for t in range(n_tiles)]
    runs, first = [], 0
    for t in range(1, n_tiles + 1):
        if t == n_tiles or patterns[t] != patterns[first]:
            runs.append((first, t) + patterns[first])
            first = t
    return tuple(runs)


def _norm_proj(x, g, w, layer, cos, sin, *, n_main, scaled_cols, rope_cols, scale, w_tail=None,
               tm=1024, tn=1024):
    s, d = x.shape
    main_tiles = n_main // tn
    n_tiles = main_tiles + (w_tail is not None)
    kern = functools.partial(_norm_proj_kernel, scale=scale, has_tail=w_tail is not None,
                             tile_runs=_tile_runs(n_tiles, tn, scaled_cols, rope_cols,
                                                  w_tail is not None))
    w_specs = [pl.BlockSpec((None, d, tn), lambda i, j: (layer, 0, jnp.minimum(j, main_tiles - 1)))]
    weights = [w]
    if w_tail is not None:
        w_specs.append(pl.BlockSpec((None, d, tn), lambda i, j: (layer, 0, 0),
                                    pipeline_mode=pl.Buffered(1)))
        weights.append(w_tail)
    return pl.pallas_call(
        kern,
        grid=(s // tm, n_tiles),
        in_specs=[pl.BlockSpec((tm, d), lambda i, j: (i, 0)),
                  pl.BlockSpec((1, d), lambda i, j: (0, 0))] + w_specs +
                 [pl.BlockSpec((tm, LANES), lambda i, j: (i, 0)),
                  pl.BlockSpec((tm, LANES), lambda i, j: (i, 0))],
        out_specs=pl.BlockSpec((tm, tn), lambda i, j: (i, j)),
        out_shape=jax.ShapeDtypeStruct((s, n_tiles * tn), BF16),
        scratch_shapes=[pltpu.VMEM((tm, d), BF16)],
        compiler_params=_params(("arbitrary", "arbitrary"), 56),
        name="norm_proj",
    )(x, g.reshape(1, d), *weights, cos, sin)


def _diff_attn_kernel(lam_ref, subln_ref, q_ref, k_ref, v_ref, o_ref,
                      qs_ref, m_ref, l_ref, acc_ref, *, tq, tk, heads, lam_init):
    i = pl.program_id(1)
    rows = 2 * tq
    half = LANES // 2
    hq = tq // 2
    sub = tq // tk

    for g in range(heads):
        q = _head(q_ref, g).astype(F32)
        lane = lax.broadcasted_iota(jnp.int32, q.shape, 1)
        q1 = jnp.where(lane < half, q, 0.0).astype(BF16)
        q2 = jnp.where(lane >= half, q, 0.0).astype(BF16)
        for part, src in enumerate((q1[0:hq], q2[0:hq], q1[hq:tq], q2[hq:tq])):
            qs_ref[g, part * hq:(part + 1) * hq, :] = src

    def reset():
        m_ref[...] = jnp.full(m_ref.shape, MASK_VALUE, F32)
        l_ref[...] = jnp.zeros(l_ref.shape, F32)
        acc_ref[...] = jnp.zeros(acc_ref.shape, F32)

    def step(start, width, diag_half):
        first = 0 if diag_half is None else diag_half * tq
        rs = slice(first, rows)
        for g in range(heads):
            kb = _head(k_ref, g, pl.ds(start, width))
            vb = _head(v_ref, g, pl.ds(start, width))
            s = lax.dot_general(qs_ref[g, rs], kb, NT_DIMS, preferred_element_type=F32)
            if diag_half is not None:
                r = lax.broadcasted_iota(jnp.int32, s.shape, 0) + first
                qrow = (r & (hq - 1)) + jnp.where(r >= tq, hq, 0)
                col = lax.broadcasted_iota(jnp.int32, s.shape, 1) + diag_half * hq
                s = jnp.where(col <= qrow, s, MASK_VALUE)
            m_prev = m_ref[g, rs]
            m_new = jnp.maximum(m_prev, jnp.max(s, axis=1, keepdims=True))
            alpha = jnp.exp2(m_prev - m_new)
            p = jnp.exp2(s - _lane_tile(m_new, width))
            l_ref[g, rs] = alpha * l_ref[g, rs] + jnp.sum(p, axis=1, keepdims=True)
            acc_ref[g, rs] = alpha * acc_ref[g, rs] + jnp.dot(p.astype(BF16), vb,
                                                              preferred_element_type=F32)
            m_ref[g, rs] = m_new

    def frozen_step(j):
        start = pl.multiple_of(j * tk, tk)
        for g in range(heads):
            kb = _head(k_ref, g, pl.ds(start, tk))
            vb = _head(v_ref, g, pl.ds(start, tk))
            s = lax.dot_general(qs_ref[g], kb, NT_DIMS, preferred_element_type=F32)
            p = jnp.exp2(s - _lane_tile(m_ref[g], tk))
            l_ref[g] += jnp.sum(p, axis=1, keepdims=True)
            acc_ref[g] += jnp.dot(p.astype(BF16), vb, preferred_element_type=F32)

    def diagonal():
        for u in range(2):
            step(pl.multiple_of(i * tq + u * hq, hq), hq, u)

    def finish():
        lp = lam_ref[...]
        lam = (jnp.exp(jnp.sum(lp[0:1] * lp[1:2], axis=1, keepdims=True))
               - jnp.exp(jnp.sum(lp[2:3] * lp[3:4], axis=1, keepdims=True)) + lam_init)
        for g in range(heads):
            o = acc_ref[g] / l_ref[g]
            for u in range(2):
                d = o[2 * u * hq:(2 * u + 1) * hq] - lam * o[(2 * u + 1) * hq:(2 * u + 2) * hq]
                y = _rms(d, subln_ref[...]) * (1.0 - lam_init)
                o_ref[u * hq:(u + 1) * hq, g * LANES:(g + 1) * LANES] = y.astype(o_ref.dtype)

    n_full = i * sub
    reset()
    diagonal()

    def pair(t, carry):
        frozen_step(2 * t)
        frozen_step(2 * t + 1)
        return carry

    lax.fori_loop(0, n_full // 2, pair, 0)

    @pl.when(n_full % 2 == 1)
    def _():
        frozen_step(n_full - 1)

    finite = jnp.minimum(jnp.min(jnp.where(l_ref[...] < FINITE_BOUND, 1.0, 0.0)),
                         jnp.min(jnp.where(jnp.abs(acc_ref[...]) < FINITE_BOUND, 1.0, 0.0)))

    @pl.when(finite > 0.5)
    def _():
        finish()

    @pl.when(finite <= 0.5)
    def _():
        reset()

        def body(j, carry):
            step(pl.multiple_of(j * tk, tk), tk, None)
            return carry

        lax.fori_loop(0, n_full, body, 0)
        diagonal()
        finish()


def _diff_attn(proj, lam_params, subln, lam_init, *, tq=512, tk=512, heads=4):
    s = proj.shape[0]
    width = A_HEADS * LANES
    hw = heads * LANES
    groups = A_HEADS // heads
    kern = functools.partial(_diff_attn_kernel, tq=tq, tk=tk, heads=heads, lam_init=lam_init)
    return pl.pallas_call(
        kern,
        grid=(groups, s // tq),
        in_specs=[pl.BlockSpec((4, ROPE_DIM), lambda h, i: (0, 0)),
                  pl.BlockSpec((1, LANES), lambda h, i: (0, 0)),
                  pl.BlockSpec((tq, hw), lambda h, i: (i, h)),
                  pl.BlockSpec((s, hw), lambda h, i: (0, groups + h),
                               pipeline_mode=pl.Buffered(1)),
                  pl.BlockSpec((s, hw), lambda h, i: (0, 2 * groups + h),
                               pipeline_mode=pl.Buffered(1))],
        out_specs=pl.BlockSpec((tq, hw), lambda h, i: (i, h)),
        out_shape=jax.ShapeDtypeStruct((s, width), BF16),
        scratch_shapes=[pltpu.VMEM((heads, 2 * tq, LANES), BF16)]
                       + [pltpu.VMEM((heads, 2 * tq, LANES), F32)] * 3,
        compiler_params=_params(("arbitrary", "arbitrary"), 56),
        name="diff_attn",
    )(lam_params, subln.reshape(1, LANES), proj, proj, proj)


def _stickbreak_kernel(q_ref, k_ref, v_ref, o_ref, tri_ref, carry_ref, acc_ref, *,
                       tq, tk, heads, scale):
    h = pl.program_id(0)
    i = pl.program_id(1)
    sub = tq // tk

    @pl.when((h == 0) & (i == 0))
    def _():
        r = lax.broadcasted_iota(jnp.int32, tri_ref.shape, 0)
        c = lax.broadcasted_iota(jnp.int32, tri_ref.shape, 1)
        tri_ref[...] = jnp.where(r > c, 1.0, 0.0).astype(BF16)

    carry_ref[...] = jnp.zeros(carry_ref.shape, F32)
    acc_ref[...] = jnp.zeros(acc_ref.shape, F32)

    def step(j, diag_block):
        start = pl.multiple_of(j * tk, tk)
        rows = slice(None) if diag_block is None else slice(diag_block * tk, tq)
        for g in range(heads):
            kb = _head(k_ref, g, pl.ds(start, tk))
            vb = _head(v_ref, g, pl.ds(start, tk))
            qk = lax.dot_general(_head(q_ref, g, rows), kb, NT_DIMS, preferred_element_type=F32)
            z = qk * scale
            nz = qk * (-scale)
            lb = jnp.minimum(nz, 0.0) - jnp.log2(1.0 + jnp.exp2(jnp.minimum(z, nz)))
            if diag_block is not None:
                row = lax.broadcasted_iota(jnp.int32, z.shape, 0)
                col = lax.broadcasted_iota(jnp.int32, z.shape, 1)
                strict = col < row
                lb = jnp.where(strict, lb, 0.0)
            later = jnp.dot(lb.astype(BF16), tri_ref[...], preferred_element_type=F32)
            e = z + lb + later + _lane_tile(carry_ref[g, rows], tk)
            if diag_block is not None:
                e = jnp.where(strict, e, MASK_VALUE)
            w = jnp.exp2(e)
            acc_ref[g, rows] += jnp.dot(w.astype(BF16), vb, preferred_element_type=F32)
            carry_ref[g, rows] += jnp.sum(lb, axis=1, keepdims=True)

    for u in reversed(range(sub)):
        step(i * sub + u, u)

    def more(state):
        j, top = state
        return (j >= 0) & (top > EXP2_UNDERFLOW)

    def body(state):
        j, _ = state
        step(j, None)
        return j - 1, jnp.max(carry_ref[...])

    lax.while_loop(more, body, (i * sub - 1, jnp.float32(0.0)))
    for g in range(heads):
        o_ref[:, g * LANES:(g + 1) * LANES] = acc_ref[g].astype(o_ref.dtype)


def _stickbreak(proj, *, first_col, tq=512, tk=256, heads=4):
    s = proj.shape[0]
    width = B_HEADS * B_HEAD_DIM
    hw = heads * LANES
    groups = B_HEADS // heads
    c0 = first_col // hw
    kern = functools.partial(_stickbreak_kernel, tq=tq, tk=tk, heads=heads,
                             scale=B_HEAD_DIM ** -0.5 * math.log2(math.e))
    return pl.pallas_call(
        kern,
        grid=(groups, s // tq),
        in_specs=[pl.BlockSpec((tq, hw), lambda h, i: (i, c0 + h)),
                  pl.BlockSpec((s, hw), lambda h, i: (0, c0 + groups + h)),
                  pl.BlockSpec((s, hw), lambda h, i: (0, c0 + 2 * groups + h))],
        out_specs=pl.BlockSpec((tq, hw), lambda h, i: (i, h)),
        out_shape=jax.ShapeDtypeStruct((s, width), BF16),
        scratch_shapes=[pltpu.VMEM((tk, tk), BF16),
                        pltpu.VMEM((heads, tq, LANES), F32),
                        pltpu.VMEM((heads, tq, LANES), F32)],
        compiler_params=_params(("arbitrary", "arbitrary"), 56),
        name="stickbreak",
    )(proj, proj, proj)


def _swa_kernel(sinks_ref, q_ref, kp_ref, kc_ref, vp_ref, vc_ref, o_ref, *, blocks):
    hk = pl.program_id(0)
    n = pl.program_id(1)
    w = WINDOW
    half = LANES // 2

    def lane_halves(prev_ref, cur_ref):
        both = jnp.concatenate([prev_ref[...], cur_ref[...]], axis=0).astype(F32)
        low = lax.broadcasted_iota(jnp.int32, both.shape, 1) < half
        return jnp.where(low, both, 0.0).astype(BF16), jnp.where(low, 0.0, both).astype(BF16)

    k_lo, k_hi = lane_halves(kp_ref, kc_ref)
    v_lo, v_hi = lane_halves(vp_ref, vc_ref)

    qi = lax.broadcasted_iota(jnp.int32, (w, 2 * w), 0)
    ki = lax.broadcasted_iota(jnp.int32, (w, 2 * w), 1)
    band = (ki > qi) & (ki <= qi + w)
    first_real = jnp.where(n > 0, 0, w)
    band_first = band & (ki >= first_real)
    lane_o = lax.broadcasted_iota(jnp.int32, (w, LANES), 1)
    log2e = math.log2(math.e)

    for b in range(blocks):
        keys = slice(b * w, (b + 2) * w)
        kbd = jnp.concatenate([k_lo[keys], k_hi[keys]], axis=0)
        vbd = jnp.concatenate([v_lo[keys], v_hi[keys]], axis=0)
        mask = band_first if b == 0 else band
        pairs = C_GROUP // 2
        qs = jnp.concatenate([q_ref[b * w:(b + 1) * w, p * LANES:(p + 1) * LANES]
                              for p in range(pairs)], axis=0)
        s = lax.dot_general(qs, kbd, NT_DIMS, preferred_element_type=F32)
        probs, scales = [], []
        for p in range(pairs):
            row_p, inv_den = [], []
            for t in range(2):
                st = jnp.where(mask, s[p * w:(p + 1) * w, t * 2 * w:(t + 1) * 2 * w], MASK_VALUE)
                sink = sinks_ref[hk * C_GROUP + 2 * p + t] * log2e
                m = jnp.maximum(jnp.max(st, axis=1, keepdims=True), sink)
                pt = jnp.exp2(st - m)
                den = jnp.sum(pt, axis=1, keepdims=True) + jnp.exp2(sink - m)
                row_p.append(pt.astype(BF16))
                inv_den.append(1.0 / den)
            probs.append(jnp.concatenate(row_p, axis=1))
            scales.append(jnp.where(lane_o < half, inv_den[0], inv_den[1]))
        o = jnp.dot(jnp.concatenate(probs, axis=0), vbd, preferred_element_type=F32)
        outs = [o[p * w:(p + 1) * w] * scales[p] for p in range(pairs)]
        o_ref[b * w:(b + 1) * w, :] = jnp.concatenate(outs, axis=1).astype(o_ref.dtype)


def _swa(proj, sinks, *, blocks=4):
    s = proj.shape[0]
    w = WINDOW
    tq = blocks * w
    qw = C_GROUP * C_HEAD_DIM
    k0 = C_Q_HEADS * C_HEAD_DIM // LANES
    v0 = k0 + C_KV_HEADS
    prev = lambda n: jnp.maximum(n * blocks - 1, 0)
    return pl.pallas_call(
        functools.partial(_swa_kernel, blocks=blocks),
        grid=(C_KV_HEADS, s // tq),
        in_specs=[pl.BlockSpec(memory_space=pltpu.SMEM),
                  pl.BlockSpec((tq, qw), lambda h, n: (n, h)),
                  pl.BlockSpec((w, LANES), lambda h, n: (prev(n), k0 + h)),
                  pl.BlockSpec((tq, LANES), lambda h, n: (n, k0 + h)),
                  pl.BlockSpec((w, LANES), lambda h, n: (prev(n), v0 + h)),
                  pl.BlockSpec((tq, LANES), lambda h, n: (n, v0 + h))],
        out_specs=pl.BlockSpec((tq, qw), lambda h, n: (n, h)),
        out_shape=jax.ShapeDtypeStruct((s, C_Q_HEADS * C_HEAD_DIM), BF16),
        compiler_params=_params(("arbitrary", "arbitrary"), 16),
        name="swa_sink",
    )(sinks, proj, proj, proj, proj, proj)


def _out_proj_kernel(*refs, n_in):
    a_refs = refs[:n_in]
    w_ref, x_ref, o_ref, wb_ref = refs[n_in:]

    @pl.when(pl.program_id(0) == 0)
    def _():
        wb_ref[...] = w_ref[...].astype(BF16)

    acc = x_ref[...]
    row = 0
    for a_ref in a_refs:
        ka = a_ref.shape[1]
        acc = acc + jnp.dot(a_ref[...], wb_ref[row:row + ka, :], preferred_element_type=F32)
        row += ka
    o_ref[...] = acc


def _out_proj(acts, w, layer, x, *, tm=512):
    s, d = x.shape
    n_in = len(acts)
    kw = w.shape[1]
    a_specs = [pl.BlockSpec((tm, a.shape[1]), lambda i: (i, 0)) for a in acts]
    w_spec = pl.BlockSpec((None, kw, d), lambda i: (layer, 0, 0), pipeline_mode=pl.Buffered(1))
    return pl.pallas_call(
        functools.partial(_out_proj_kernel, n_in=n_in),
        grid=(s // tm,),
        in_specs=a_specs + [w_spec, pl.BlockSpec((tm, d), lambda i: (i, 0))],
        out_specs=pl.BlockSpec((tm, d), lambda i: (i, 0)),
        out_shape=jax.ShapeDtypeStruct((s, d), F32),
        scratch_shapes=[pltpu.VMEM((kw, d), BF16)],
        compiler_params=_params(("arbitrary",), 56),
        name="out_proj",
    )(*acts, w, x)


def _mlp_kernel(x_ref, g_ref, wu_ref, wd_ref, gf_ref, o_ref, xn_ref, *, final_norm):
    f = pl.program_id(1)

    def up_down(xn):
        hid = jnp.dot(xn, wu_ref[...].astype(BF16), preferred_element_type=F32)
        hid = jnp.square(jnp.maximum(hid, 0.0)).astype(BF16)
        return jnp.dot(hid, wd_ref[...].astype(BF16), preferred_element_type=F32)

    @pl.when(f == 0)
    def _():
        x = x_ref[...]
        xn = _rms(x, g_ref[...]).astype(BF16)
        xn_ref[...] = xn
        o_ref[...] = x + up_down(xn)

    @pl.when(f > 0)
    def _():
        o_ref[...] += up_down(xn_ref[...])

    if final_norm:
        @pl.when(f == pl.num_programs(1) - 1)
        def _():
            o_ref[...] = _rms(o_ref[...], gf_ref[...])


def _mlp(x, g, w_up, w_down, layer, g_final, *, final_norm, tm=1024, tf=512):
    s, d = x.shape
    dff = w_up.shape[2]
    return pl.pallas_call(
        functools.partial(_mlp_kernel, final_norm=final_norm),
        grid=(s // tm, dff // tf),
        in_specs=[pl.BlockSpec((tm, d), lambda i, f: (i, 0)),
                  pl.BlockSpec((1, d), lambda i, f: (0, 0)),
                  pl.BlockSpec((None, d, tf), lambda i, f: (layer, 0, f)),
                  pl.BlockSpec((None, tf, d), lambda i, f: (layer, f, 0)),
                  pl.BlockSpec((1, d), lambda i, f: (0, 0))],
        out_specs=pl.BlockSpec((tm, d), lambda i, f: (i, 0)),
        out_shape=jax.ShapeDtypeStruct((s, d), F32),
        scratch_shapes=[pltpu.VMEM((tm, d), BF16)],
        compiler_params=_params(("arbitrary", "arbitrary"), 60),
        name="mlp",
    )(x, g.reshape(1, d), w_up, w_down, g_final.reshape(1, d))


def _dup_kv_columns(w_in_c):
    layers, d, _ = w_in_c.shape
    c_q = C_Q_HEADS * C_HEAD_DIM
    kv_heads = 2 * C_KV_HEADS
    wkv = w_in_c[:, :, c_q:].reshape(layers, d, kv_heads, 1, C_HEAD_DIM)
    wkv = jnp.broadcast_to(wkv, (layers, d, kv_heads, 2, C_HEAD_DIM))
    return wkv.reshape(layers, d, 2 * kv_heads * C_HEAD_DIM)


def kernel(x, positions, norm_mix, norm_mlp, norm_final, w_in_ab, w_out_ab, lambda_q1, lambda_k1,
           lambda_q2, lambda_k2, diff_subln, w_in_c, w_out_c, sinks, w_up, w_down):
    b, s, d = x.shape
    assert b == 1
    depth = norm_mix.shape[0]
    h = x.reshape(s, d)
    cos, sin = _rope_tables(positions)
    w_kv_dup = _dup_kv_columns(w_in_c)

    a_width = A_HEADS * LANES
    c_q = C_Q_HEADS * C_HEAD_DIM
    c_kv = C_KV_HEADS * C_HEAD_DIM
    for layer in range(depth):
        j = layer // 2
        if layer % 2 == 0:
            proj = _norm_proj(h, norm_mix[layer], w_in_ab, j, cos, sin,
                              n_main=w_in_ab.shape[2], scaled_cols=a_width, rope_cols=2 * a_width,
                              scale=ROPE_DIM ** -0.5 * math.log2(math.e))
            lam_init = 0.8 - 0.6 * math.exp(-0.3 * layer)
            lam_params = jnp.stack([lambda_q1[j], lambda_k1[j], lambda_q2[j], lambda_k2[j]])
            oa = _diff_attn(proj, lam_params, diff_subln[j], lam_init)
            ob = _stickbreak(proj, first_col=3 * a_width)
            h = _out_proj([oa, ob], w_out_ab, j, h)
        else:
            proj = _norm_proj(h, norm_mix[layer], w_in_c, j, cos, sin, w_tail=w_kv_dup,
                              n_main=c_q, scaled_cols=c_q, rope_cols=c_q + 2 * c_kv,
                              scale=C_HEAD_DIM ** -0.5 * math.log2(math.e))
            oc = _swa(proj, sinks[j])
            h = _out_proj([oc], w_out_c, j, h)
        h = _mlp(h, norm_mlp[layer], w_up, w_down, layer,
                 norm_final, final_norm=(layer == depth - 1))
    return h.reshape(b, s, d)
```

```python
import functools
import math

import jax
import jax.numpy as jnp
from jax import lax
from jax.experimental import pallas as pl
from jax.experimental.pallas import tpu as pltpu

F32 = jnp.float32
BF16 = jnp.bfloat16

EPS = 1e-6
ROPE_THETA = 10000.0
LANES = 128
ROPE_DIM = 64
A_HEADS = 8
B_HEADS = 8
B_HEAD_DIM = 128
C_Q_HEADS = 32
C_KV_HEADS = 4
C_GROUP = C_Q_HEADS // C_KV_HEADS
C_HEAD_DIM = 64
WINDOW = 128
MASK_VALUE = -1e30
EXP2_UNDERFLOW = -160.0
FINITE_BOUND = 1e30
MIB = 1024 * 1024

NT_DIMS = (((1,), (1,)), ((), ()))


VMEM_CAPACITY_MIB = 64
VMEM_RESERVED_MIB = 4
VMEM_COMPILER_MIB = 12


def _params(semantics, *buffers):
    need = sum(math.prod(shape) * jnp.dtype(dtype).itemsize * copies
               for shape, dtype, copies in buffers)
    limit = min(need + VMEM_COMPILER_MIB * MIB, (VMEM_CAPACITY_MIB - VMEM_RESERVED_MIB) * MIB)
    return pltpu.CompilerParams(dimension_semantics=semantics, vmem_limit_bytes=limit)


def _rms(x, g):
    ms = jnp.mean(x * x, axis=-1, keepdims=True)
    return x * lax.rsqrt(ms + EPS) * g


def _lane_tile(a, width):
    reps = width // LANES
    return a if reps == 1 else jnp.concatenate([a] * reps, axis=1)


def _head(ref, g, rows=slice(None)):
    return ref[rows, g * LANES:(g + 1) * LANES]


def _rope_table_kernel(pos_ref, invf_ref, cos_ref, sin_ref):
    ang = pos_ref[...].astype(F32) * invf_ref[...]
    lane = lax.broadcasted_iota(jnp.int32, ang.shape, 1)
    sign = jnp.where((lane & (ROPE_DIM // 2)) == 0, -1.0, 1.0)
    cos_ref[...] = jnp.cos(ang)
    sin_ref[...] = jnp.sin(ang) * sign


def _rope_tables(positions):
    s = positions.shape[-1]
    tm = 1024
    inv_freq = ROPE_THETA ** (-jnp.arange(0, ROPE_DIM, 2, dtype=F32) / ROPE_DIM)
    invf = jnp.tile(inv_freq, LANES // (ROPE_DIM // 2)).reshape(1, LANES)
    pos = positions.reshape(s, 1)
    return pl.pallas_call(
        _rope_table_kernel,
        grid=(s // tm,),
        in_specs=[pl.BlockSpec((tm, 1), lambda i: (i, 0)),
                  pl.BlockSpec((1, LANES), lambda i: (0, 0))],
        out_specs=[pl.BlockSpec((tm, LANES), lambda i: (i, 0))] * 2,
        out_shape=[jax.ShapeDtypeStruct((s, LANES), F32)] * 2,
        compiler_params=_params(("arbitrary",), ((tm, LANES), F32, 6)),
        name="rope_tables",
    )(pos, invf)


def _rope(x, cos, ss):
    half = ROPE_DIM // 2
    lane = lax.broadcasted_iota(jnp.int32, cos.shape, 1)
    low = (lane & half) == 0
    outs = []
    for c in range(x.shape[1] // LANES):
        xc = x[:, c * LANES:(c + 1) * LANES]
        partner = jnp.where(low, pltpu.roll(xc, LANES - half, 1), pltpu.roll(xc, half, 1))
        outs.append(xc * cos + partner * ss)
    return outs[0] if len(outs) == 1 else jnp.concatenate(outs, axis=1)


PROJ_CHUNK = 512


def _norm_proj_kernel(*refs, tile_runs, scale, has_tail):
    x_ref, g_ref, w_ref = refs[:3]
    tail_ref = refs[3] if has_tail else None
    cos_ref, sin_ref, o_ref, xn_ref = refs[-4:]
    j = pl.program_id(1)

    def chunk(xn, c, kind, from_tail):
        cols = slice(c * PROJ_CHUNK, (c + 1) * PROJ_CHUNK)
        wt = (tail_ref if from_tail else w_ref)[:, cols].astype(BF16)
        acc = jnp.dot(xn, wt, preferred_element_type=F32)
        if kind != "plain":
            acc = _rope(acc, cos_ref[...], sin_ref[...])
        if kind == "scaled":
            acc = acc * scale
        o_ref[:, cols] = acc.astype(o_ref.dtype)

    @pl.when(j == 0)
    def _():
        xn = _rms(x_ref[...], g_ref[...]).astype(BF16)
        xn_ref[...] = xn
        for c, kind in enumerate(tile_runs[0][2]):
            chunk(xn, c, kind, tile_runs[0][3])

    for first, end, kinds, from_tail in tile_runs:
        first = max(first, 1)
        if first < end:
            @pl.when((j >= first) & (j < end))
            def _(kinds=kinds, from_tail=from_tail):
                for c, kind in enumerate(kinds):
                    chunk(xn_ref[...], c, kind, from_tail)


def _tile_runs(n_tiles, tn, scaled_cols, rope_cols, has_tail):
    def kind(col):
        return "scaled" if col < scaled_cols else "rope" if col < rope_cols else "plain"
    patterns = [(tuple(kind(t * tn + c * PROJ_CHUNK) for c in range(tn // PROJ_CHUNK)),
                 has_tail and t == n_tiles - 1) for t in range(n_tiles)]
    runs, first = [], 0
    for t in range(1, n_tiles + 1):
        if t == n_tiles or patterns[t] != patterns[first]:
            runs.append((first, t) + patterns[first])
            first = t
    return tuple(runs)


def _norm_proj(x, g, w, layer, cos, sin, *, n_main, scaled_cols, rope_cols, scale, w_tail=None,
               tm=1024, tn=1024):
    s, d = x.shape
    main_tiles = n_main // tn
    n_tiles = main_tiles + (w_tail is not None)
    kern = functools.partial(_norm_proj_kernel, scale=scale, has_tail=w_tail is not None,
                             tile_runs=_tile_runs(n_tiles, tn, scaled_cols, rope_cols,
                                                  w_tail is not None))
    w_specs = [pl.BlockSpec((None, d, tn), lambda i, j: (layer, 0, jnp.minimum(j, main_tiles - 1)))]
    weights = [w]
    if w_tail is not None:
        w_specs.append(pl.BlockSpec((None, d, tn), lambda i, j: (layer, 0, 0),
                                    pipeline_mode=pl.Buffered(1)))
        weights.append(w_tail)
    return pl.pallas_call(
        kern,
        grid=(s // tm, n_tiles),
        in_specs=[pl.BlockSpec((tm, d), lambda i, j: (i, 0)),
                  pl.BlockSpec((1, d), lambda i, j: (0, 0))] + w_specs +
                 [pl.BlockSpec((tm, LANES), lambda i, j: (i, 0)),
                  pl.BlockSpec((tm, LANES), lambda i, j: (i, 0))],
        out_specs=pl.BlockSpec((tm, tn), lambda i, j: (i, j)),
        out_shape=jax.ShapeDtypeStruct((s, n_tiles * tn), BF16),
        scratch_shapes=[pltpu.VMEM((tm, d), BF16)],
        compiler_params=_params(("arbitrary", "arbitrary"),
                                ((tm, d), F32, 2), ((d, tn), F32, 2 + (w_tail is not None)),
                                ((tm, LANES), F32, 4), ((tm, tn), BF16, 2), ((tm, d), BF16, 1)),
        name="norm_proj",
    )(x, g.reshape(1, d), *weights, cos, sin)


def _diff_attn_kernel(lam_ref, subln_ref, q_ref, k_ref, v_ref, o_ref,
                      qs_ref, m_ref, l_ref, acc_ref, *, tq, tk, heads, lam_init):
    i = pl.program_id(1)
    rows = 2 * tq
    half = LANES // 2
    hq = tq // 2
    sub = tq // tk

    for g in range(heads):
        q = _head(q_ref, g).astype(F32)
        lane = lax.broadcasted_iota(jnp.int32, q.shape, 1)
        q1 = jnp.where(lane < half, q, 0.0).astype(BF16)
        q2 = jnp.where(lane >= half, q, 0.0).astype(BF16)
        for part, src in enumerate((q1[0:hq], q2[0:hq], q1[hq:tq], q2[hq:tq])):
            qs_ref[g, part * hq:(part + 1) * hq, :] = src

    def reset():
        m_ref[...] = jnp.full(m_ref.shape, MASK_VALUE, F32)
        l_ref[...] = jnp.zeros(l_ref.shape, F32)
        acc_ref[...] = jnp.zeros(acc_ref.shape, F32)

    def step(start, width, diag_half):
        first = 0 if diag_half is None else diag_half * tq
        rs = slice(first, rows)
        for g in range(heads):
            kb = _head(k_ref, g, pl.ds(start, width))
            vb = _head(v_ref, g, pl.ds(start, width))
            s = lax.dot_general(qs_ref[g, rs], kb, NT_DIMS, preferred_element_type=F32)
            if diag_half is not None:
                r = lax.broadcasted_iota(jnp.int32, s.shape, 0) + first
                qrow = (r & (hq - 1)) + jnp.where(r >= tq, hq, 0)
                col = lax.broadcasted_iota(jnp.int32, s.shape, 1) + diag_half * hq
                s = jnp.where(col <= qrow, s, MASK_VALUE)
            m_prev = m_ref[g, rs]
            m_new = jnp.maximum(m_prev, jnp.max(s, axis=1, keepdims=True))
            alpha = jnp.exp2(m_prev - m_new)
            p = jnp.exp2(s - _lane_tile(m_new, width))
            l_ref[g, rs] = alpha * l_ref[g, rs] + jnp.sum(p, axis=1, keepdims=True)
            acc_ref[g, rs] = alpha * acc_ref[g, rs] + jnp.dot(p.astype(BF16), vb,
                                                              preferred_element_type=F32)
            m_ref[g, rs] = m_new

    def frozen_step(j):
        start = pl.multiple_of(j * tk, tk)
        for g in range(heads):
            kb = _head(k_ref, g, pl.ds(start, tk))
            vb = _head(v_ref, g, pl.ds(start, tk))
            s = lax.dot_general(qs_ref[g], kb, NT_DIMS, preferred_element_type=F32)
            p = jnp.exp2(s - _lane_tile(m_ref[g], tk))
            l_ref[g] += jnp.sum(p, axis=1, keepdims=True)
            acc_ref[g] += jnp.dot(p.astype(BF16), vb, preferred_element_type=F32)

    def diagonal():
        for u in range(2):
            step(pl.multiple_of(i * tq + u * hq, hq), hq, u)

    def finish():
        lp = lam_ref[...]
        lam = (jnp.exp(jnp.sum(lp[0:1] * lp[1:2], axis=1, keepdims=True))
               - jnp.exp(jnp.sum(lp[2:3] * lp[3:4], axis=1, keepdims=True)) + lam_init)
        for g in range(heads):
            o = acc_ref[g] / l_ref[g]
            for u in range(2):
                d = o[2 * u * hq:(2 * u + 1) * hq] - lam * o[(2 * u + 1) * hq:(2 * u + 2) * hq]
                y = _rms(d, subln_ref[...]) * (1.0 - lam_init)
                o_ref[u * hq:(u + 1) * hq, g * LANES:(g + 1) * LANES] = y.astype(o_ref.dtype)

    n_full = i * sub
    reset()
    diagonal()

    def pair(t, carry):
        frozen_step(2 * t)
        frozen_step(2 * t + 1)
        return carry

    lax.fori_loop(0, n_full // 2, pair, 0)

    @pl.when(n_full % 2 == 1)
    def _():
        frozen_step(n_full - 1)

    finite = jnp.minimum(jnp.min(jnp.where(l_ref[...] < FINITE_BOUND, 1.0, 0.0)),
                         jnp.min(jnp.where(jnp.abs(acc_ref[...]) < FINITE_BOUND, 1.0, 0.0)))

    @pl.when(finite > 0.5)
    def _():
        finish()

    @pl.when(finite <= 0.5)
    def _():
        reset()

        def body(j, carry):
            step(pl.multiple_of(j * tk, tk), tk, None)
            return carry

        lax.fori_loop(0, n_full, body, 0)
        diagonal()
        finish()


def _diff_attn(proj, lam_params, subln, lam_init, *, tq=512, tk=512, heads=4):
    s = proj.shape[0]
    width = A_HEADS * LANES
    hw = heads * LANES
    groups = A_HEADS // heads
    half_rows = tq // 2
    assert tq % tk == 0 and half_rows & (half_rows - 1) == 0, (tq, tk)
    kern = functools.partial(_diff_attn_kernel, tq=tq, tk=tk, heads=heads, lam_init=lam_init)
    return pl.pallas_call(
        kern,
        grid=(groups, s // tq),
        in_specs=[pl.BlockSpec((4, ROPE_DIM), lambda h, i: (0, 0)),
                  pl.BlockSpec((1, LANES), lambda h, i: (0, 0)),
                  pl.BlockSpec((tq, hw), lambda h, i: (i, h)),
                  pl.BlockSpec((s, hw), lambda h, i: (0, groups + h),
                               pipeline_mode=pl.Buffered(1)),
                  pl.BlockSpec((s, hw), lambda h, i: (0, 2 * groups + h),
                               pipeline_mode=pl.Buffered(1))],
        out_specs=pl.BlockSpec((tq, hw), lambda h, i: (i, h)),
        out_shape=jax.ShapeDtypeStruct((s, width), BF16),
        scratch_shapes=[pltpu.VMEM((heads, 2 * tq, LANES), BF16)]
                       + [pltpu.VMEM((heads, 2 * tq, LANES), F32)] * 3,
        compiler_params=_params(("arbitrary", "arbitrary"),
                                ((tq, hw), BF16, 4), ((s, hw), BF16, 2),
                                ((heads, 2 * tq, LANES), BF16, 1), ((heads, 2 * tq, LANES), F32, 3)),
        name="diff_attn",
    )(lam_params, subln.reshape(1, LANES), proj, proj, proj)


def _stickbreak_kernel(q_ref, k_ref, v_ref, o_ref, tri_ref, carry_ref, acc_ref, *,
                       tq, tk, heads, scale):
    h = pl.program_id(0)
    i = pl.program_id(1)
    sub = tq // tk

    @pl.when((h == 0) & (i == 0))
    def _():
        r = lax.broadcasted_iota(jnp.int32, tri_ref.shape, 0)
        c = lax.broadcasted_iota(jnp.int32, tri_ref.shape, 1)
        tri_ref[...] = jnp.where(r > c, 1.0, 0.0).astype(BF16)

    carry_ref[...] = jnp.zeros(carry_ref.shape, F32)
    acc_ref[...] = jnp.zeros(acc_ref.shape, F32)

    def step(j, diag_block):
        start = pl.multiple_of(j * tk, tk)
        rows = slice(None) if diag_block is None else slice(diag_block * tk, tq)
        for g in range(heads):
            kb = _head(k_ref, g, pl.ds(start, tk))
            vb = _head(v_ref, g, pl.ds(start, tk))
            qk = lax.dot_general(_head(q_ref, g, rows), kb, NT_DIMS, preferred_element_type=F32)
            z = qk * scale
            nz = qk * (-scale)
            lb = jnp.minimum(nz, 0.0) - jnp.log2(1.0 + jnp.exp2(jnp.minimum(z, nz)))
            if diag_block is not None:
                row = lax.broadcasted_iota(jnp.int32, z.shape, 0)
                col = lax.broadcasted_iota(jnp.int32, z.shape, 1)
                strict = col < row
                lb = jnp.where(strict, lb, 0.0)
            later = jnp.dot(lb.astype(BF16), tri_ref[...], preferred_element_type=F32)
            e = z + lb + later + _lane_tile(carry_ref[g, rows], tk)
            if diag_block is not None:
                e = jnp.where(strict, e, MASK_VALUE)
            w = jnp.exp2(e)
            acc_ref[g, rows] += jnp.dot(w.astype(BF16), vb, preferred_element_type=F32)
            carry_ref[g, rows] += jnp.sum(lb, axis=1, keepdims=True)

    for u in reversed(range(sub)):
        step(i * sub + u, u)

    def more(state):
        j, top = state
        return (j >= 0) & (top > EXP2_UNDERFLOW)

    def body(state):
        j, _ = state
        step(j, None)
        return j - 1, jnp.max(carry_ref[...])

    lax.while_loop(more, body, (i * sub - 1, jnp.float32(0.0)))
    for g in range(heads):
        o_ref[:, g * LANES:(g + 1) * LANES] = acc_ref[g].astype(o_ref.dtype)


def _stickbreak(proj, *, first_col, tq=512, tk=256, heads=4):
    s = proj.shape[0]
    width = B_HEADS * B_HEAD_DIM
    hw = heads * LANES
    groups = B_HEADS // heads
    c0 = first_col // hw
    kern = functools.partial(_stickbreak_kernel, tq=tq, tk=tk, heads=heads,
                             scale=B_HEAD_DIM ** -0.5 * math.log2(math.e))
    return pl.pallas_call(
        kern,
        grid=(groups, s // tq),
        in_specs=[pl.BlockSpec((tq, hw), lambda h, i: (i, c0 + h)),
                  pl.BlockSpec((s, hw), lambda h, i: (0, c0 + groups + h)),
                  pl.BlockSpec((s, hw), lambda h, i: (0, c0 + 2 * groups + h))],
        out_specs=pl.BlockSpec((tq, hw), lambda h, i: (i, h)),
        out_shape=jax.ShapeDtypeStruct((s, width), BF16),
        scratch_shapes=[pltpu.VMEM((tk, tk), BF16),
                        pltpu.VMEM((heads, tq, LANES), F32),
                        pltpu.VMEM((heads, tq, LANES), F32)],
        compiler_params=_params(("arbitrary", "arbitrary"),
                                ((tq, hw), BF16, 4), ((s, hw), BF16, 4),
                                ((tk, tk), BF16, 1), ((heads, tq, LANES), F32, 2)),
        name="stickbreak",
    )(proj, proj, proj)


def _swa_kernel(sinks_ref, q_ref, kp_ref, kc_ref, vp_ref, vc_ref, o_ref, *, blocks):
    hk = pl.program_id(0)
    n = pl.program_id(1)
    w = WINDOW
    half = LANES // 2

    def lane_halves(prev_ref, cur_ref):
        both = jnp.concatenate([prev_ref[...], cur_ref[...]], axis=0).astype(F32)
        low = lax.broadcasted_iota(jnp.int32, both.shape, 1) < half
        return jnp.where(low, both, 0.0).astype(BF16), jnp.where(low, 0.0, both).astype(BF16)

    k_lo, k_hi = lane_halves(kp_ref, kc_ref)
    v_lo, v_hi = lane_halves(vp_ref, vc_ref)

    qi = lax.broadcasted_iota(jnp.int32, (w, 2 * w), 0)
    ki = lax.broadcasted_iota(jnp.int32, (w, 2 * w), 1)
    band = (ki > qi) & (ki <= qi + w)
    first_real = jnp.where(n > 0, 0, w)
    band_first = band & (ki >= first_real)
    lane_o = lax.broadcasted_iota(jnp.int32, (w, LANES), 1)
    log2e = math.log2(math.e)

    for b in range(blocks):
        keys = slice(b * w, (b + 2) * w)
        kbd = jnp.concatenate([k_lo[keys], k_hi[keys]], axis=0)
        vbd = jnp.concatenate([v_lo[keys], v_hi[keys]], axis=0)
        mask = band_first if b == 0 else band
        pairs = C_GROUP // 2
        qs = jnp.concatenate([q_ref[b * w:(b + 1) * w, p * LANES:(p + 1) * LANES]
                              for p in range(pairs)], axis=0)
        s = lax.dot_general(qs, kbd, NT_DIMS, preferred_element_type=F32)
        probs, scales = [], []
        for p in range(pairs):
            row_p, inv_den = [], []
            for t in range(2):
                st = jnp.where(mask, s[p * w:(p + 1) * w, t * 2 * w:(t + 1) * 2 * w], MASK_VALUE)
                sink = sinks_ref[hk * C_GROUP + 2 * p + t] * log2e
                m = jnp.maximum(jnp.max(st, axis=1, keepdims=True), sink)
                pt = jnp.exp2(st - m)
                den = jnp.sum(pt, axis=1, keepdims=True) + jnp.exp2(sink - m)
                row_p.append(pt.astype(BF16))
                inv_den.append(1.0 / den)
            probs.append(jnp.concatenate(row_p, axis=1))
            scales.append(jnp.where(lane_o < half, inv_den[0], inv_den[1]))
        o = jnp.dot(jnp.concatenate(probs, axis=0), vbd, preferred_element_type=F32)
        outs = [o[p * w:(p + 1) * w] * scales[p] for p in range(pairs)]
        o_ref[b * w:(b + 1) * w, :] = jnp.concatenate(outs, axis=1).astype(o_ref.dtype)


def _swa(proj, sinks, *, blocks=4):
    s = proj.shape[0]
    w = WINDOW
    tq = blocks * w
    qw = C_GROUP * C_HEAD_DIM
    k0 = C_Q_HEADS * C_HEAD_DIM // LANES
    v0 = k0 + C_KV_HEADS
    prev = lambda n: jnp.maximum(n * blocks - 1, 0)
    return pl.pallas_call(
        functools.partial(_swa_kernel, blocks=blocks),
        grid=(C_KV_HEADS, s // tq),
        in_specs=[pl.BlockSpec(memory_space=pltpu.SMEM),
                  pl.BlockSpec((tq, qw), lambda h, n: (n, h)),
                  pl.BlockSpec((w, LANES), lambda h, n: (prev(n), k0 + h)),
                  pl.BlockSpec((tq, LANES), lambda h, n: (n, k0 + h)),
                  pl.BlockSpec((w, LANES), lambda h, n: (prev(n), v0 + h)),
                  pl.BlockSpec((tq, LANES), lambda h, n: (n, v0 + h))],
        out_specs=pl.BlockSpec((tq, qw), lambda h, n: (n, h)),
        out_shape=jax.ShapeDtypeStruct((s, C_Q_HEADS * C_HEAD_DIM), BF16),
        compiler_params=_params(("arbitrary", "arbitrary"),
                                ((tq, qw), BF16, 4), ((tq + w, LANES), BF16, 4)),
        name="swa_sink",
    )(sinks, proj, proj, proj, proj, proj)


def _out_proj_kernel(*refs, n_in):
    a_refs = refs[:n_in]
    w_ref, x_ref, o_ref, wb_ref = refs[n_in:]

    @pl.when(pl.program_id(0) == 0)
    def _():
        wb_ref[...] = w_ref[...].astype(BF16)

    acc = x_ref[...]
    row = 0
    for a_ref in a_refs:
        ka = a_ref.shape[1]
        acc = acc + jnp.dot(a_ref[...], wb_ref[row:row + ka, :], preferred_element_type=F32)
        row += ka
    o_ref[...] = acc


def _out_proj(acts, w, layer, x, *, tm=512):
    s, d = x.shape
    n_in = len(acts)
    kw = w.shape[1]
    a_specs = [pl.BlockSpec((tm, a.shape[1]), lambda i: (i, 0)) for a in acts]
    w_spec = pl.BlockSpec((None, kw, d), lambda i: (layer, 0, 0), pipeline_mode=pl.Buffered(1))
    return pl.pallas_call(
        functools.partial(_out_proj_kernel, n_in=n_in),
        grid=(s // tm,),
        in_specs=a_specs + [w_spec, pl.BlockSpec((tm, d), lambda i: (i, 0))],
        out_specs=pl.BlockSpec((tm, d), lambda i: (i, 0)),
        out_shape=jax.ShapeDtypeStruct((s, d), F32),
        scratch_shapes=[pltpu.VMEM((kw, d), BF16)],
        compiler_params=_params(("arbitrary",),
                                ((tm, kw), BF16, 2), ((kw, d), F32, 1), ((kw, d), BF16, 1),
                                ((tm, d), F32, 4)),
        name="out_proj",
    )(*acts, w, x)


def _mlp_kernel(x_ref, g_ref, wu_ref, wd_ref, gf_ref, o_ref, xn_ref, *, final_norm):
    f = pl.program_id(1)

    def up_down(xn):
        hid = jnp.dot(xn, wu_ref[...].astype(BF16), preferred_element_type=F32)
        hid = jnp.square(jnp.maximum(hid, 0.0)).astype(BF16)
        return jnp.dot(hid, wd_ref[...].astype(BF16), preferred_element_type=F32)

    @pl.when(f == 0)
    def _():
        x = x_ref[...]
        xn = _rms(x, g_ref[...]).astype(BF16)
        xn_ref[...] = xn
        o_ref[...] = x + up_down(xn)

    @pl.when(f > 0)
    def _():
        o_ref[...] += up_down(xn_ref[...])

    if final_norm:
        @pl.when(f == pl.num_programs(1) - 1)
        def _():
            o_ref[...] = _rms(o_ref[...], gf_ref[...])


def _mlp(x, g, w_up, w_down, layer, g_final, *, final_norm, tm=1024, tf=512):
    s, d = x.shape
    dff = w_up.shape[2]
    return pl.pallas_call(
        functools.partial(_mlp_kernel, final_norm=final_norm),
        grid=(s // tm, dff // tf),
        in_specs=[pl.BlockSpec((tm, d), lambda i, f: (i, 0)),
                  pl.BlockSpec((1, d), lambda i, f: (0, 0)),
                  pl.BlockSpec((None, d, tf), lambda i, f: (layer, 0, f)),
                  pl.BlockSpec((None, tf, d), lambda i, f: (layer, f, 0)),
                  pl.BlockSpec((1, d), lambda i, f: (0, 0))],
        out_specs=pl.BlockSpec((tm, d), lambda i, f: (i, 0)),
        out_shape=jax.ShapeDtypeStruct((s, d), F32),
        scratch_shapes=[pltpu.VMEM((tm, d), BF16)],
        compiler_params=_params(("arbitrary", "arbitrary"),
                                ((tm, d), F32, 4), ((d, tf), F32, 4), ((tm, d), BF16, 1)),
        name="mlp",
    )(x, g.reshape(1, d), w_up, w_down, g_final.reshape(1, d))


def _dup_kv_columns(w_in_c):
    layers, d, _ = w_in_c.shape
    c_q = C_Q_HEADS * C_HEAD_DIM
    kv_heads = 2 * C_KV_HEADS
    wkv = w_in_c[:, :, c_q:].reshape(layers, d, kv_heads, 1, C_HEAD_DIM)
    wkv = jnp.broadcast_to(wkv, (layers, d, kv_heads, 2, C_HEAD_DIM))
    return wkv.reshape(layers, d, 2 * kv_heads * C_HEAD_DIM)


def kernel(x, positions, norm_mix, norm_mlp, norm_final, w_in_ab, w_out_ab, lambda_q1, lambda_k1,
           lambda_q2, lambda_k2, diff_subln, w_in_c, w_out_c, sinks, w_up, w_down):
    b, s, d = x.shape
    assert b == 1
    depth = norm_mix.shape[0]
    h = x.reshape(s, d)
    cos, sin = _rope_tables(positions)
    w_kv_dup = _dup_kv_columns(w_in_c)

    a_width = A_HEADS * LANES
    c_q = C_Q_HEADS * C_HEAD_DIM
    c_kv = C_KV_HEADS * C_HEAD_DIM
    for layer in range(depth):
        j = layer // 2
        if layer % 2 == 0:
            proj = _norm_proj(h, norm_mix[layer], w_in_ab, j, cos, sin,
                              n_main=w_in_ab.shape[2], scaled_cols=a_width, rope_cols=2 * a_width,
                              scale=ROPE_DIM ** -0.5 * math.log2(math.e))
            lam_init = 0.8 - 0.6 * math.exp(-0.3 * layer)
            lam_params = jnp.stack([lambda_q1[j], lambda_k1[j], lambda_q2[j], lambda_k2[j]])
            oa = _diff_attn(proj, lam_params, diff_subln[j], lam_init)
            ob = _stickbreak(proj, first_col=3 * a_width)
            h = _out_proj([oa, ob], w_out_ab, j, h)
        else:
            proj = _norm_proj(h, norm_mix[layer], w_in_c, j, cos, sin, w_tail=w_kv_dup,
                              n_main=c_q, scaled_cols=c_q, rope_cols=c_q + 2 * c_kv,
                              scale=C_HEAD_DIM ** -0.5 * math.log2(math.e))
            oc = _swa(proj, sinks[j])
            h = _out_proj([oc], w_out_c, j, h)
        h = _mlp(h, norm_mlp[layer], w_up, w_down, layer,
                 norm_final, final_norm=(layer == depth - 1))
    return h.reshape(b, s, d)
```

```python
import functools
import math

import jax
import jax.numpy as jnp
import numpy as np
from jax import lax
from jax.experimental import pallas as pl
from jax.experimental.pallas import tpu as pltpu

F32 = jnp.float32
BF16 = jnp.bfloat16

EPS = 1e-6
ROPE_THETA = 10000.0
LANES = 128
ROPE_DIM = 64
A_HEADS = 8
B_HEADS = 8
B_HEAD_DIM = 128
C_Q_HEADS = 32
C_KV_HEADS = 4
C_GROUP = C_Q_HEADS // C_KV_HEADS
C_HEAD_DIM = 64
WINDOW = 128
MASK_VALUE = -1e30
EXP2_UNDERFLOW = -160.0
FINITE_BOUND = 1e30
MIB = 1024 * 1024

NT_DIMS = (((1,), (1,)), ((), ()))


VMEM_CAPACITY_MIB = 64
VMEM_RESERVED_MIB = 4
VMEM_COMPILER_MIB = 12


def _params(semantics, *buffers):
    need = sum(math.prod(shape) * jnp.dtype(dtype).itemsize * copies
               for shape, dtype, copies in buffers)
    limit = min(need + VMEM_COMPILER_MIB * MIB, (VMEM_CAPACITY_MIB - VMEM_RESERVED_MIB) * MIB)
    return pltpu.CompilerParams(dimension_semantics=semantics, vmem_limit_bytes=limit)


def _rms(x, g):
    ms = jnp.mean(x * x, axis=-1, keepdims=True)
    return x * lax.rsqrt(ms + EPS) * g


def _lane_tile(a, width):
    reps = width // LANES
    return a if reps == 1 else jnp.concatenate([a] * reps, axis=1)


def _head(ref, g, rows=slice(None)):
    return ref[rows, g * LANES:(g + 1) * LANES]


def _rope_table_kernel(pos_ref, invf_ref, cos_ref, sin_ref):
    ang = pos_ref[...].astype(F32) * invf_ref[...]
    lane = lax.broadcasted_iota(jnp.int32, ang.shape, 1)
    sign = jnp.where((lane & (ROPE_DIM // 2)) == 0, -1.0, 1.0)
    cos_ref[...] = jnp.cos(ang)
    sin_ref[...] = jnp.sin(ang) * sign


def _rope_tables(positions):
    s = positions.shape[-1]
    tm = 1024
    inv_freq = ROPE_THETA ** (-jnp.arange(0, ROPE_DIM, 2, dtype=F32) / ROPE_DIM)
    invf = jnp.tile(inv_freq, LANES // (ROPE_DIM // 2)).reshape(1, LANES)
    pos = positions.reshape(s, 1)
    return pl.pallas_call(
        _rope_table_kernel,
        grid=(s // tm,),
        in_specs=[pl.BlockSpec((tm, 1), lambda i: (i, 0)),
                  pl.BlockSpec((1, LANES), lambda i: (0, 0))],
        out_specs=[pl.BlockSpec((tm, LANES), lambda i: (i, 0))] * 2,
        out_shape=[jax.ShapeDtypeStruct((s, LANES), F32)] * 2,
        compiler_params=_params(("arbitrary",), ((tm, LANES), F32, 6)),
        name="rope_tables",
    )(pos, invf)


def _rope(x, cos, ss):
    half = ROPE_DIM // 2
    lane = lax.broadcasted_iota(jnp.int32, cos.shape, 1)
    low = (lane & half) == 0
    outs = []
    for c in range(x.shape[1] // LANES):
        xc = x[:, c * LANES:(c + 1) * LANES]
        partner = jnp.where(low, pltpu.roll(xc, LANES - half, 1), pltpu.roll(xc, half, 1))
        outs.append(xc * cos + partner * ss)
    return outs[0] if len(outs) == 1 else jnp.concatenate(outs, axis=1)


PROJ_CHUNK = 512


def _norm_proj_kernel(*refs, tile_runs, scale, tail_rope_cols):
    x_ref, g_ref, w_ref = refs[:3]
    tail_ref, dup_ref = refs[3:5] if tail_rope_cols is not None else (None, None)
    cos_ref, sin_ref, o_ref, xn_ref = refs[-4:]
    j = pl.program_id(1)

    def chunk(xn, c, kind):
        cols = slice(c * PROJ_CHUNK, (c + 1) * PROJ_CHUNK)
        acc = jnp.dot(xn, w_ref[:, cols].astype(BF16), preferred_element_type=F32)
        if kind != "plain":
            acc = _rope(acc, cos_ref[...], sin_ref[...])
        if kind == "scaled":
            acc = acc * scale
        o_ref[:, cols] = acc.astype(o_ref.dtype)

    @pl.when(j == 0)
    def _():
        xn = _rms(x_ref[...], g_ref[...]).astype(BF16)
        xn_ref[...] = xn
        for c, kind in enumerate(tile_runs[0][2]):
            chunk(xn, c, kind)

    for first, end, kinds in tile_runs:
        first = max(first, 1)
        if first < end:
            @pl.when((j >= first) & (j < end))
            def _(kinds=kinds):
                for c, kind in enumerate(kinds):
                    chunk(xn_ref[...], c, kind)

    if tail_rope_cols is not None:
        @pl.when(j == tile_runs[-1][1])
        def _():
            acc = jnp.dot(xn_ref[...], tail_ref[...].astype(BF16), preferred_element_type=F32)
            roped = _rope(acc[:, :tail_rope_cols], cos_ref[...], sin_ref[...])
            narrow = jnp.concatenate([roped, acc[:, tail_rope_cols:]], axis=1).astype(BF16)
            o_ref[...] = jnp.dot(narrow, dup_ref[...],
                                 preferred_element_type=F32).astype(o_ref.dtype)


def _tile_runs(n_tiles, tn, scaled_cols, rope_cols):
    def kind(col):
        return "scaled" if col < scaled_cols else "rope" if col < rope_cols else "plain"
    patterns = [tuple(kind(t * tn + c * PROJ_CHUNK) for c in range(tn // PROJ_CHUNK))
                for t in range(n_tiles)]
    runs, first = [], 0
    for t in range(1, n_tiles + 1):
        if t == n_tiles or patterns[t] != patterns[first]:
            runs.append((first, t, patterns[first]))
            first = t
    return tuple(runs)


def _head_dup_matrix(n_in, head_dim):
    out_col = np.arange(2 * n_in)
    src = (out_col // (2 * head_dim)) * head_dim + out_col % head_dim
    return jnp.asarray(np.arange(n_in)[:, None] == src[None, :], dtype=BF16)


def _norm_proj(x, g, w, layer, cos, sin, *, n_main, scaled_cols, rope_cols, scale,
               tail_rope_cols=None, tm=1024, tn=1024):
    s, d = x.shape
    main_tiles = n_main // tn
    has_tail = tail_rope_cols is not None
    n_tiles = main_tiles + has_tail
    kern = functools.partial(_norm_proj_kernel, scale=scale, tail_rope_cols=tail_rope_cols,
                             tile_runs=_tile_runs(main_tiles, tn, scaled_cols, rope_cols))
    w_specs = [pl.BlockSpec((None, d, tn), lambda i, j: (layer, 0, jnp.minimum(j, main_tiles - 1)))]
    weights = [w]
    buffers = [((tm, d), F32, 2), ((d, tn), F32, 2), ((tm, LANES), F32, 4), ((tm, tn), BF16, 2),
               ((tm, d), BF16, 1)]
    if has_tail:
        half = tn // 2
        once = pl.Buffered(1)
        w_specs += [pl.BlockSpec((None, d, half), lambda i, j: (layer, 0, n_main // half),
                                 pipeline_mode=once),
                    pl.BlockSpec((half, tn), lambda i, j: (0, 0), pipeline_mode=once)]
        weights += [w, _head_dup_matrix(half, C_HEAD_DIM)]
        buffers += [((d, half), F32, 1), ((half, tn), BF16, 1)]
    return pl.pallas_call(
        kern,
        grid=(s // tm, n_tiles),
        in_specs=[pl.BlockSpec((tm, d), lambda i, j: (i, 0)),
                  pl.BlockSpec((1, d), lambda i, j: (0, 0))] + w_specs +
                 [pl.BlockSpec((tm, LANES), lambda i, j: (i, 0)),
                  pl.BlockSpec((tm, LANES), lambda i, j: (i, 0))],
        out_specs=pl.BlockSpec((tm, tn), lambda i, j: (i, j)),
        out_shape=jax.ShapeDtypeStruct((s, n_tiles * tn), BF16),
        scratch_shapes=[pltpu.VMEM((tm, d), BF16)],
        compiler_params=_params(("arbitrary", "arbitrary"), *buffers),
        name="norm_proj",
    )(x, g.reshape(1, d), *weights, cos, sin)


def _diff_attn_kernel(lam_ref, subln_ref, q_ref, k_ref, v_ref, o_ref,
                      qs_ref, m_ref, l_ref, acc_ref, *, tq, tk, heads, lam_init):
    i = pl.program_id(1)
    rows = 2 * tq
    half = LANES // 2
    hq = tq // 2
    sub = tq // tk

    for g in range(heads):
        q = _head(q_ref, g).astype(F32)
        lane = lax.broadcasted_iota(jnp.int32, q.shape, 1)
        q1 = jnp.where(lane < half, q, 0.0).astype(BF16)
        q2 = jnp.where(lane >= half, q, 0.0).astype(BF16)
        for part, src in enumerate((q1[0:hq], q2[0:hq], q1[hq:tq], q2[hq:tq])):
            qs_ref[g, part * hq:(part + 1) * hq, :] = src

    def reset():
        m_ref[...] = jnp.full(m_ref.shape, MASK_VALUE, F32)
        l_ref[...] = jnp.zeros(l_ref.shape, F32)
        acc_ref[...] = jnp.zeros(acc_ref.shape, F32)

    def step(start, width, diag_half):
        first = 0 if diag_half is None else diag_half * tq
        rs = slice(first, rows)
        for g in range(heads):
            kb = _head(k_ref, g, pl.ds(start, width))
            vb = _head(v_ref, g, pl.ds(start, width))
            s = lax.dot_general(qs_ref[g, rs], kb, NT_DIMS, preferred_element_type=F32)
            if diag_half is not None:
                r = lax.broadcasted_iota(jnp.int32, s.shape, 0) + first
                qrow = (r & (hq - 1)) + jnp.where(r >= tq, hq, 0)
                col = lax.broadcasted_iota(jnp.int32, s.shape, 1) + diag_half * hq
                s = jnp.where(col <= qrow, s, MASK_VALUE)
            m_prev = m_ref[g, rs]
            m_new = jnp.maximum(m_prev, jnp.max(s, axis=1, keepdims=True))
            alpha = jnp.exp2(m_prev - m_new)
            p = jnp.exp2(s - _lane_tile(m_new, width))
            l_ref[g, rs] = alpha * l_ref[g, rs] + jnp.sum(p, axis=1, keepdims=True)
            acc_ref[g, rs] = alpha * acc_ref[g, rs] + jnp.dot(p.astype(BF16), vb,
                                                              preferred_element_type=F32)
            m_ref[g, rs] = m_new

    def frozen_step(j):
        start = pl.multiple_of(j * tk, tk)
        for g in range(heads):
            kb = _head(k_ref, g, pl.ds(start, tk))
            vb = _head(v_ref, g, pl.ds(start, tk))
            s = lax.dot_general(qs_ref[g], kb, NT_DIMS, preferred_element_type=F32)
            p = jnp.exp2(s - _lane_tile(m_ref[g], tk))
            l_ref[g] += jnp.sum(p, axis=1, keepdims=True)
            acc_ref[g] += jnp.dot(p.astype(BF16), vb, preferred_element_type=F32)

    def diagonal():
        for u in range(2):
            step(pl.multiple_of(i * tq + u * hq, hq), hq, u)

    def finish():
        lp = lam_ref[...]
        lam = (jnp.exp(jnp.sum(lp[0:1] * lp[1:2], axis=1, keepdims=True))
               - jnp.exp(jnp.sum(lp[2:3] * lp[3:4], axis=1, keepdims=True)) + lam_init)
        for g in range(heads):
            o = acc_ref[g] / l_ref[g]
            for u in range(2):
                d = o[2 * u * hq:(2 * u + 1) * hq] - lam * o[(2 * u + 1) * hq:(2 * u + 2) * hq]
                y = _rms(d, subln_ref[...]) * (1.0 - lam_init)
                o_ref[u * hq:(u + 1) * hq, g * LANES:(g + 1) * LANES] = y.astype(o_ref.dtype)

    n_full = i * sub
    reset()
    diagonal()

    def pair(t, carry):
        frozen_step(2 * t)
        frozen_step(2 * t + 1)
        return carry

    lax.fori_loop(0, n_full // 2, pair, 0)

    @pl.when(n_full % 2 == 1)
    def _():
        frozen_step(n_full - 1)

    finite = jnp.minimum(jnp.min(jnp.where(l_ref[...] < FINITE_BOUND, 1.0, 0.0)),
                         jnp.min(jnp.where(jnp.abs(acc_ref[...]) < FINITE_BOUND, 1.0, 0.0)))

    @pl.when(finite > 0.5)
    def _():
        finish()

    @pl.when(finite <= 0.5)
    def _():
        reset()

        def body(j, carry):
            step(pl.multiple_of(j * tk, tk), tk, None)
            return carry

        lax.fori_loop(0, n_full, body, 0)
        diagonal()
        finish()


def _diff_attn(proj, lam_params, subln, lam_init, *, tq=512, tk=512, heads=4):
    s = proj.shape[0]
    width = A_HEADS * LANES
    hw = heads * LANES
    groups = A_HEADS // heads
    half_rows = tq // 2
    assert tq % tk == 0 and half_rows & (half_rows - 1) == 0, (tq, tk)
    kern = functools.partial(_diff_attn_kernel, tq=tq, tk=tk, heads=heads, lam_init=lam_init)
    return pl.pallas_call(
        kern,
        grid=(groups, s // tq),
        in_specs=[pl.BlockSpec((4, ROPE_DIM), lambda h, i: (0, 0)),
                  pl.BlockSpec((1, LANES), lambda h, i: (0, 0)),
                  pl.BlockSpec((tq, hw), lambda h, i: (i, h)),
                  pl.BlockSpec((s, hw), lambda h, i: (0, groups + h),
                               pipeline_mode=pl.Buffered(1)),
                  pl.BlockSpec((s, hw), lambda h, i: (0, 2 * groups + h),
                               pipeline_mode=pl.Buffered(1))],
        out_specs=pl.BlockSpec((tq, hw), lambda h, i: (i, h)),
        out_shape=jax.ShapeDtypeStruct((s, width), BF16),
        scratch_shapes=[pltpu.VMEM((heads, 2 * tq, LANES), BF16)]
                       + [pltpu.VMEM((heads, 2 * tq, LANES), F32)] * 3,
        compiler_params=_params(("arbitrary", "arbitrary"),
                                ((tq, hw), BF16, 4), ((s, hw), BF16, 2),
                                ((heads, 2 * tq, LANES), BF16, 1), ((heads, 2 * tq, LANES), F32, 3)),
        name="diff_attn",
    )(lam_params, subln.reshape(1, LANES), proj, proj, proj)


def _stickbreak_kernel(q_ref, k_ref, v_ref, o_ref, tri_ref, carry_ref, acc_ref, *,
                       tq, tk, heads, scale):
    h = pl.program_id(0)
    i = pl.program_id(1)
    sub = tq // tk

    @pl.when((h == 0) & (i == 0))
    def _():
        r = lax.broadcasted_iota(jnp.int32, tri_ref.shape, 0)
        c = lax.broadcasted_iota(jnp.int32, tri_ref.shape, 1)
        tri_ref[...] = jnp.where(r > c, 1.0, 0.0).astype(BF16)

    carry_ref[...] = jnp.zeros(carry_ref.shape, F32)
    acc_ref[...] = jnp.zeros(acc_ref.shape, F32)

    def step(j, diag_block):
        start = pl.multiple_of(j * tk, tk)
        rows = slice(None) if diag_block is None else slice(diag_block * tk, tq)
        for g in range(heads):
            kb = _head(k_ref, g, pl.ds(start, tk))
            vb = _head(v_ref, g, pl.ds(start, tk))
            qk = lax.dot_general(_head(q_ref, g, rows), kb, NT_DIMS, preferred_element_type=F32)
            z = qk * scale
            nz = qk * (-scale)
            lb = jnp.minimum(nz, 0.0) - jnp.log2(1.0 + jnp.exp2(jnp.minimum(z, nz)))
            if diag_block is not None:
                row = lax.broadcasted_iota(jnp.int32, z.shape, 0)
                col = lax.broadcasted_iota(jnp.int32, z.shape, 1)
                strict = col < row
                lb = jnp.where(strict, lb, 0.0)
            later = jnp.dot(lb.astype(BF16), tri_ref[...], preferred_element_type=F32)
            e = z + lb + later + _lane_tile(carry_ref[g, rows], tk)
            if diag_block is not None:
                e = jnp.where(strict, e, MASK_VALUE)
            w = jnp.exp2(e)
            acc_ref[g, rows] += jnp.dot(w.astype(BF16), vb, preferred_element_type=F32)
            carry_ref[g, rows] += jnp.sum(lb, axis=1, keepdims=True)

    for u in reversed(range(sub)):
        step(i * sub + u, u)

    def more(state):
        j, top = state
        return (j >= 0) & (top > EXP2_UNDERFLOW)

    def body(state):
        j, _ = state
        step(j, None)
        return j - 1, jnp.max(carry_ref[...])

    lax.while_loop(more, body, (i * sub - 1, jnp.float32(0.0)))
    for g in range(heads):
        o_ref[:, g * LANES:(g + 1) * LANES] = acc_ref[g].astype(o_ref.dtype)


def _stickbreak(proj, *, first_col, tq=512, tk=256, heads=4):
    s = proj.shape[0]
    width = B_HEADS * B_HEAD_DIM
    hw = heads * LANES
    groups = B_HEADS // heads
    c0 = first_col // hw
    kern = functools.partial(_stickbreak_kernel, tq=tq, tk=tk, heads=heads,
                             scale=B_HEAD_DIM ** -0.5 * math.log2(math.e))
    return pl.pallas_call(
        kern,
        grid=(groups, s // tq),
        in_specs=[pl.BlockSpec((tq, hw), lambda h, i: (i, c0 + h)),
                  pl.BlockSpec((s, hw), lambda h, i: (0, c0 + groups + h)),
                  pl.BlockSpec((s, hw), lambda h, i: (0, c0 + 2 * groups + h))],
        out_specs=pl.BlockSpec((tq, hw), lambda h, i: (i, h)),
        out_shape=jax.ShapeDtypeStruct((s, width), BF16),
        scratch_shapes=[pltpu.VMEM((tk, tk), BF16),
                        pltpu.VMEM((heads, tq, LANES), F32),
                        pltpu.VMEM((heads, tq, LANES), F32)],
        compiler_params=_params(("arbitrary", "arbitrary"),
                                ((tq, hw), BF16, 4), ((s, hw), BF16, 4),
                                ((tk, tk), BF16, 1), ((heads, tq, LANES), F32, 2)),
        name="stickbreak",
    )(proj, proj, proj)


def _swa_kernel(sinks_ref, q_ref, kp_ref, kc_ref, vp_ref, vc_ref, o_ref, *, blocks):
    hk = pl.program_id(0)
    n = pl.program_id(1)
    w = WINDOW
    half = LANES // 2

    def lane_halves(prev_ref, cur_ref):
        both = jnp.concatenate([prev_ref[...], cur_ref[...]], axis=0).astype(F32)
        low = lax.broadcasted_iota(jnp.int32, both.shape, 1) < half
        return jnp.where(low, both, 0.0).astype(BF16), jnp.where(low, 0.0, both).astype(BF16)

    k_lo, k_hi = lane_halves(kp_ref, kc_ref)
    v_lo, v_hi = lane_halves(vp_ref, vc_ref)

    qi = lax.broadcasted_iota(jnp.int32, (w, 2 * w), 0)
    ki = lax.broadcasted_iota(jnp.int32, (w, 2 * w), 1)
    band = (ki > qi) & (ki <= qi + w)
    first_real = jnp.where(n > 0, 0, w)
    band_first = band & (ki >= first_real)
    lane_o = lax.broadcasted_iota(jnp.int32, (w, LANES), 1)
    log2e = math.log2(math.e)

    for b in range(blocks):
        keys = slice(b * w, (b + 2) * w)
        kbd = jnp.concatenate([k_lo[keys], k_hi[keys]], axis=0)
        vbd = jnp.concatenate([v_lo[keys], v_hi[keys]], axis=0)
        mask = band_first if b == 0 else band
        pairs = C_GROUP // 2
        qs = jnp.concatenate([q_ref[b * w:(b + 1) * w, p * LANES:(p + 1) * LANES]
                              for p in range(pairs)], axis=0)
        s = lax.dot_general(qs, kbd, NT_DIMS, preferred_element_type=F32)
        probs, scales = [], []
        for p in range(pairs):
            row_p, inv_den = [], []
            for t in range(2):
                st = jnp.where(mask, s[p * w:(p + 1) * w, t * 2 * w:(t + 1) * 2 * w], MASK_VALUE)
                sink = sinks_ref[hk * C_GROUP + 2 * p + t] * log2e
                m = jnp.maximum(jnp.max(st, axis=1, keepdims=True), sink)
                pt = jnp.exp2(st - m)
                den = jnp.sum(pt, axis=1, keepdims=True) + jnp.exp2(sink - m)
                row_p.append(pt.astype(BF16))
                inv_den.append(1.0 / den)
            probs.append(jnp.concatenate(row_p, axis=1))
            scales.append(jnp.where(lane_o < half, inv_den[0], inv_den[1]))
        o = jnp.dot(jnp.concatenate(probs, axis=0), vbd, preferred_element_type=F32)
        outs = [o[p * w:(p + 1) * w] * scales[p] for p in range(pairs)]
        o_ref[b * w:(b + 1) * w, :] = jnp.concatenate(outs, axis=1).astype(o_ref.dtype)


def _swa(proj, sinks, *, blocks=4):
    s = proj.shape[0]
    w = WINDOW
    tq = blocks * w
    qw = C_GROUP * C_HEAD_DIM
    k0 = C_Q_HEADS * C_HEAD_DIM // LANES
    v0 = k0 + C_KV_HEADS
    prev = lambda n: jnp.maximum(n * blocks - 1, 0)
    return pl.pallas_call(
        functools.partial(_swa_kernel, blocks=blocks),
        grid=(C_KV_HEADS, s // tq),
        in_specs=[pl.BlockSpec(memory_space=pltpu.SMEM),
                  pl.BlockSpec((tq, qw), lambda h, n: (n, h)),
                  pl.BlockSpec((w, LANES), lambda h, n: (prev(n), k0 + h)),
                  pl.BlockSpec((tq, LANES), lambda h, n: (n, k0 + h)),
                  pl.BlockSpec((w, LANES), lambda h, n: (prev(n), v0 + h)),
                  pl.BlockSpec((tq, LANES), lambda h, n: (n, v0 + h))],
        out_specs=pl.BlockSpec((tq, qw), lambda h, n: (n, h)),
        out_shape=jax.ShapeDtypeStruct((s, C_Q_HEADS * C_HEAD_DIM), BF16),
        compiler_params=_params(("arbitrary", "arbitrary"),
                                ((tq, qw), BF16, 4), ((tq + w, LANES), BF16, 4)),
        name="swa_sink",
    )(sinks, proj, proj, proj, proj, proj)


def _out_proj_kernel(*refs, n_in):
    a_refs = refs[:n_in]
    w_ref, x_ref, o_ref, wb_ref = refs[n_in:]

    @pl.when(pl.program_id(0) == 0)
    def _():
        wb_ref[...] = w_ref[...].astype(BF16)

    acc = x_ref[...]
    row = 0
    for a_ref in a_refs:
        ka = a_ref.shape[1]
        acc = acc + jnp.dot(a_ref[...], wb_ref[row:row + ka, :], preferred_element_type=F32)
        row += ka
    o_ref[...] = acc


def _out_proj(acts, w, layer, x, *, tm=512):
    s, d = x.shape
    n_in = len(acts)
    kw = w.shape[1]
    a_specs = [pl.BlockSpec((tm, a.shape[1]), lambda i: (i, 0)) for a in acts]
    w_spec = pl.BlockSpec((None, kw, d), lambda i: (layer, 0, 0), pipeline_mode=pl.Buffered(1))
    return pl.pallas_call(
        functools.partial(_out_proj_kernel, n_in=n_in),
        grid=(s // tm,),
        in_specs=a_specs + [w_spec, pl.BlockSpec((tm, d), lambda i: (i, 0))],
        out_specs=pl.BlockSpec((tm, d), lambda i: (i, 0)),
        out_shape=jax.ShapeDtypeStruct((s, d), F32),
        scratch_shapes=[pltpu.VMEM((kw, d), BF16)],
        compiler_params=_params(("arbitrary",),
                                ((tm, kw), BF16, 2), ((kw, d), F32, 1), ((kw, d), BF16, 1),
                                ((tm, d), F32, 4)),
        name="out_proj",
    )(*acts, w, x)


def _mlp_kernel(x_ref, g_ref, wu_ref, wd_ref, gf_ref, o_ref, xn_ref, *, final_norm):
    f = pl.program_id(1)

    def up_down(xn):
        hid = jnp.dot(xn, wu_ref[...].astype(BF16), preferred_element_type=F32)
        hid = jnp.square(jnp.maximum(hid, 0.0)).astype(BF16)
        return jnp.dot(hid, wd_ref[...].astype(BF16), preferred_element_type=F32)

    @pl.when(f == 0)
    def _():
        x = x_ref[...]
        xn = _rms(x, g_ref[...]).astype(BF16)
        xn_ref[...] = xn
        o_ref[...] = x + up_down(xn)

    @pl.when(f > 0)
    def _():
        o_ref[...] += up_down(xn_ref[...])

    if final_norm:
        @pl.when(f == pl.num_programs(1) - 1)
        def _():
            o_ref[...] = _rms(o_ref[...], gf_ref[...])


def _mlp(x, g, w_up, w_down, layer, g_final, *, final_norm, tm=1024, tf=512):
    s, d = x.shape
    dff = w_up.shape[2]
    return pl.pallas_call(
        functools.partial(_mlp_kernel, final_norm=final_norm),
        grid=(s // tm, dff // tf),
        in_specs=[pl.BlockSpec((tm, d), lambda i, f: (i, 0)),
                  pl.BlockSpec((1, d), lambda i, f: (0, 0)),
                  pl.BlockSpec((None, d, tf), lambda i, f: (layer, 0, f)),
                  pl.BlockSpec((None, tf, d), lambda i, f: (layer, f, 0)),
                  pl.BlockSpec((1, d), lambda i, f: (0, 0))],
        out_specs=pl.BlockSpec((tm, d), lambda i, f: (i, 0)),
        out_shape=jax.ShapeDtypeStruct((s, d), F32),
        scratch_shapes=[pltpu.VMEM((tm, d), BF16)],
        compiler_params=_params(("arbitrary", "arbitrary"),
                                ((tm, d), F32, 4), ((d, tf), F32, 4), ((tm, d), BF16, 1)),
        name="mlp",
    )(x, g.reshape(1, d), w_up, w_down, g_final.reshape(1, d))


def kernel(x, positions, norm_mix, norm_mlp, norm_final, w_in_ab, w_out_ab, lambda_q1, lambda_k1,
           lambda_q2, lambda_k2, diff_subln, w_in_c, w_out_c, sinks, w_up, w_down):
    b, s, d = x.shape
    assert b == 1
    depth = norm_mix.shape[0]
    h = x.reshape(s, d)
    cos, sin = _rope_tables(positions)

    a_width = A_HEADS * LANES
    c_q = C_Q_HEADS * C_HEAD_DIM
    c_kv = C_KV_HEADS * C_HEAD_DIM
    for layer in range(depth):
        j = layer // 2
        if layer % 2 == 0:
            proj = _norm_proj(h, norm_mix[layer], w_in_ab, j, cos, sin,
                              n_main=w_in_ab.shape[2], scaled_cols=a_width, rope_cols=2 * a_width,
                              scale=ROPE_DIM ** -0.5 * math.log2(math.e))
            lam_init = 0.8 - 0.6 * math.exp(-0.3 * layer)
            lam_params = jnp.stack([lambda_q1[j], lambda_k1[j], lambda_q2[j], lambda_k2[j]])
            oa = _diff_attn(proj, lam_params, diff_subln[j], lam_init)
            ob = _stickbreak(proj, first_col=3 * a_width)
            h = _out_proj([oa, ob], w_out_ab, j, h)
        else:
            proj = _norm_proj(h, norm_mix[layer], w_in_c, j, cos, sin, tail_rope_cols=c_kv,
                              n_main=c_q, scaled_cols=c_q, rope_cols=c_q,
                              scale=C_HEAD_DIM ** -0.5 * math.log2(math.e))
            oc = _swa(proj, sinks[j])
            h = _out_proj([oc], w_out_c, j, h)
        h = _mlp(h, norm_mlp[layer], w_up, w_down, layer,
                 norm_final, final_norm=(layer == depth - 1))
    return h.reshape(b, s, d)
```

```python
import functools
import math

import jax
import jax.numpy as jnp
import numpy as np
from jax import lax
from jax.experimental import pallas as pl
from jax.experimental.pallas import tpu as pltpu

F32 = jnp.float32
BF16 = jnp.bfloat16

EPS = 1e-6
ROPE_THETA = 10000.0
LANES = 128
ROPE_DIM = 64
A_HEADS = 8
B_HEADS = 8
B_HEAD_DIM = 128
C_Q_HEADS = 32
C_KV_HEADS = 4
C_GROUP = C_Q_HEADS // C_KV_HEADS
C_HEAD_DIM = 64
WINDOW = 128
MASK_VALUE = -1e30
EXP2_UNDERFLOW = -160.0
FINITE_BOUND = 1e30
MIB = 1024 * 1024

NT_DIMS = (((1,), (1,)), ((), ()))


VMEM_CAPACITY_MIB = 64
VMEM_RESERVED_MIB = 4


def _params(semantics, *buffers):
    need = sum(math.prod(shape) * jnp.dtype(dtype).itemsize * copies
               for shape, dtype, copies in buffers)
    limit = (VMEM_CAPACITY_MIB - VMEM_RESERVED_MIB) * MIB
    assert need <= limit, (need, limit)
    return pltpu.CompilerParams(dimension_semantics=semantics, vmem_limit_bytes=limit)


def _rms(x, g):
    ms = jnp.mean(x * x, axis=-1, keepdims=True)
    return x * lax.rsqrt(ms + EPS) * g


def _lane_tile(a, width):
    reps = width // LANES
    return a if reps == 1 else jnp.concatenate([a] * reps, axis=1)


def _head(ref, g, rows=slice(None)):
    return ref[rows, g * LANES:(g + 1) * LANES]


def _rope_table_kernel(pos_ref, invf_ref, cos_ref, sin_ref):
    ang = pos_ref[...].astype(F32) * invf_ref[...]
    lane = lax.broadcasted_iota(jnp.int32, ang.shape, 1)
    sign = jnp.where((lane & (ROPE_DIM // 2)) == 0, -1.0, 1.0)
    cos_ref[...] = jnp.cos(ang)
    sin_ref[...] = jnp.sin(ang) * sign


def _rope_tables(positions):
    s = positions.shape[-1]
    tm = 1024
    inv_freq = ROPE_THETA ** (-jnp.arange(0, ROPE_DIM, 2, dtype=F32) / ROPE_DIM)
    invf = jnp.tile(inv_freq, LANES // (ROPE_DIM // 2)).reshape(1, LANES)
    pos = positions.reshape(s, 1)
    return pl.pallas_call(
        _rope_table_kernel,
        grid=(s // tm,),
        in_specs=[pl.BlockSpec((tm, 1), lambda i: (i, 0)),
                  pl.BlockSpec((1, LANES), lambda i: (0, 0))],
        out_specs=[pl.BlockSpec((tm, LANES), lambda i: (i, 0))] * 2,
        out_shape=[jax.ShapeDtypeStruct((s, LANES), F32)] * 2,
        compiler_params=_params(("arbitrary",), ((tm, LANES), F32, 6)),
        name="rope_tables",
    )(pos, invf)


def _rope(x, cos, ss):
    half = ROPE_DIM // 2
    lane = lax.broadcasted_iota(jnp.int32, cos.shape, 1)
    low = (lane & half) == 0
    outs = []
    for c in range(x.shape[1] // LANES):
        xc = x[:, c * LANES:(c + 1) * LANES]
        partner = jnp.where(low, pltpu.roll(xc, LANES - half, 1), pltpu.roll(xc, half, 1))
        outs.append(xc * cos + partner * ss)
    return outs[0] if len(outs) == 1 else jnp.concatenate(outs, axis=1)


PROJ_CHUNK = 512


def _norm_proj_kernel(*refs, tile_runs, scale, tail_rope_cols):
    x_ref, g_ref, w_ref = refs[:3]
    tail_ref, dup_ref = refs[3:5] if tail_rope_cols is not None else (None, None)
    cos_ref, sin_ref, o_ref, xn_ref = refs[-4:]
    j = pl.program_id(1)

    def chunk(xn, c, kind):
        cols = slice(c * PROJ_CHUNK, (c + 1) * PROJ_CHUNK)
        acc = jnp.dot(xn, w_ref[:, cols].astype(BF16), preferred_element_type=F32)
        if kind != "plain":
            acc = _rope(acc, cos_ref[...], sin_ref[...])
        if kind == "scaled":
            acc = acc * scale
        o_ref[:, cols] = acc.astype(o_ref.dtype)

    @pl.when(j == 0)
    def _():
        xn = _rms(x_ref[...], g_ref[...]).astype(BF16)
        xn_ref[...] = xn
        for c, kind in enumerate(tile_runs[0][2]):
            chunk(xn, c, kind)

    for first, end, kinds in tile_runs:
        first = max(first, 1)
        if first < end:
            @pl.when((j >= first) & (j < end))
            def _(kinds=kinds):
                for c, kind in enumerate(kinds):
                    chunk(xn_ref[...], c, kind)

    if tail_rope_cols is not None:
        @pl.when(j == tile_runs[-1][1])
        def _():
            acc = jnp.dot(xn_ref[...], tail_ref[...].astype(BF16), preferred_element_type=F32)
            roped = _rope(acc[:, :tail_rope_cols], cos_ref[...], sin_ref[...])
            narrow = jnp.concatenate([roped, acc[:, tail_rope_cols:]], axis=1).astype(BF16)
            o_ref[...] = jnp.dot(narrow, dup_ref[...],
                                 preferred_element_type=F32).astype(o_ref.dtype)


def _tile_runs(n_tiles, tn, scaled_cols, rope_cols):
    def kind(col):
        return "scaled" if col < scaled_cols else "rope" if col < rope_cols else "plain"
    patterns = [tuple(kind(t * tn + c * PROJ_CHUNK) for c in range(tn // PROJ_CHUNK))
                for t in range(n_tiles)]
    runs, first = [], 0
    for t in range(1, n_tiles + 1):
        if t == n_tiles or patterns[t] != patterns[first]:
            runs.append((first, t, patterns[first]))
            first = t
    return tuple(runs)


def _head_dup_matrix(n_in, head_dim):
    out_col = np.arange(2 * n_in)
    src = (out_col // (2 * head_dim)) * head_dim + out_col % head_dim
    return jnp.asarray(np.arange(n_in)[:, None] == src[None, :], dtype=BF16)


def _norm_proj(x, g, w, layer, cos, sin, *, n_main, scaled_cols, rope_cols, scale,
               tail_rope_cols=None, tm=1024, tn=1024):
    s, d = x.shape
    main_tiles = n_main // tn
    has_tail = tail_rope_cols is not None
    n_tiles = main_tiles + has_tail
    kern = functools.partial(_norm_proj_kernel, scale=scale, tail_rope_cols=tail_rope_cols,
                             tile_runs=_tile_runs(main_tiles, tn, scaled_cols, rope_cols))
    w_specs = [pl.BlockSpec((None, d, tn), lambda i, j: (layer, 0, jnp.minimum(j, main_tiles - 1)))]
    weights = [w]
    buffers = [((tm, d), F32, 2), ((d, tn), F32, 2), ((tm, LANES), F32, 4), ((tm, tn), BF16, 2),
               ((tm, d), BF16, 1)]
    if has_tail:
        half = tn // 2
        once = pl.Buffered(1)
        w_specs += [pl.BlockSpec((None, d, half), lambda i, j: (layer, 0, n_main // half),
                                 pipeline_mode=once),
                    pl.BlockSpec((half, tn), lambda i, j: (0, 0), pipeline_mode=once)]
        weights += [w, _head_dup_matrix(half, C_HEAD_DIM)]
        buffers += [((d, half), F32, 1), ((half, tn), BF16, 1)]
    return pl.pallas_call(
        kern,
        grid=(s // tm, n_tiles),
        in_specs=[pl.BlockSpec((tm, d), lambda i, j: (i, 0)),
                  pl.BlockSpec((1, d), lambda i, j: (0, 0))] + w_specs +
                 [pl.BlockSpec((tm, LANES), lambda i, j: (i, 0)),
                  pl.BlockSpec((tm, LANES), lambda i, j: (i, 0))],
        out_specs=pl.BlockSpec((tm, tn), lambda i, j: (i, j)),
        out_shape=jax.ShapeDtypeStruct((s, n_tiles * tn), BF16),
        scratch_shapes=[pltpu.VMEM((tm, d), BF16)],
        compiler_params=_params(("arbitrary", "arbitrary"), *buffers),
        name="norm_proj",
    )(x, g.reshape(1, d), *weights, cos, sin)


def _diff_attn_kernel(lam_ref, subln_ref, q_ref, k_ref, v_ref, o_ref,
                      qs_ref, m_ref, l_ref, acc_ref, *, tq, tk, heads, lam_init):
    i = pl.program_id(1)
    rows = 2 * tq
    half = LANES // 2
    hq = tq // 2
    sub = tq // tk

    for g in range(heads):
        q = _head(q_ref, g).astype(F32)
        lane = lax.broadcasted_iota(jnp.int32, q.shape, 1)
        q1 = jnp.where(lane < half, q, 0.0).astype(BF16)
        q2 = jnp.where(lane >= half, q, 0.0).astype(BF16)
        for part, src in enumerate((q1[0:hq], q2[0:hq], q1[hq:tq], q2[hq:tq])):
            qs_ref[g, part * hq:(part + 1) * hq, :] = src

    def reset():
        m_ref[...] = jnp.full(m_ref.shape, MASK_VALUE, F32)
        l_ref[...] = jnp.zeros(l_ref.shape, F32)
        acc_ref[...] = jnp.zeros(acc_ref.shape, F32)

    def step(start, width, diag_half):
        first = 0 if diag_half is None else diag_half * tq
        rs = slice(first, rows)
        for g in range(heads):
            kb = _head(k_ref, g, pl.ds(start, width))
            vb = _head(v_ref, g, pl.ds(start, width))
            s = lax.dot_general(qs_ref[g, rs], kb, NT_DIMS, preferred_element_type=F32)
            if diag_half is not None:
                r = lax.broadcasted_iota(jnp.int32, s.shape, 0) + first
                qrow = (r & (hq - 1)) + jnp.where(r >= tq, hq, 0)
                col = lax.broadcasted_iota(jnp.int32, s.shape, 1) + diag_half * hq
                s = jnp.where(col <= qrow, s, MASK_VALUE)
            m_prev = m_ref[g, rs]
            m_new = jnp.maximum(m_prev, jnp.max(s, axis=1, keepdims=True))
            alpha = jnp.exp2(m_prev - m_new)
            p = jnp.exp2(s - _lane_tile(m_new, width))
            l_ref[g, rs] = alpha * l_ref[g, rs] + jnp.sum(p, axis=1, keepdims=True)
            acc_ref[g, rs] = alpha * acc_ref[g, rs] + jnp.dot(p.astype(BF16), vb,
                                                              preferred_element_type=F32)
            m_ref[g, rs] = m_new

    def frozen_step(j):
        start = pl.multiple_of(j * tk, tk)
        for g in range(heads):
            kb = _head(k_ref, g, pl.ds(start, tk))
            vb = _head(v_ref, g, pl.ds(start, tk))
            s = lax.dot_general(qs_ref[g], kb, NT_DIMS, preferred_element_type=F32)
            p = jnp.exp2(s - _lane_tile(m_ref[g], tk))
            l_ref[g] += jnp.sum(p, axis=1, keepdims=True)
            acc_ref[g] += jnp.dot(p.astype(BF16), vb, preferred_element_type=F32)

    def diagonal():
        for u in range(2):
            step(pl.multiple_of(i * tq + u * hq, hq), hq, u)

    def finish():
        lp = lam_ref[...]
        lam = (jnp.exp(jnp.sum(lp[0:1] * lp[1:2], axis=1, keepdims=True))
               - jnp.exp(jnp.sum(lp[2:3] * lp[3:4], axis=1, keepdims=True)) + lam_init)
        for g in range(heads):
            o = acc_ref[g] / l_ref[g]
            for u in range(2):
                d = o[2 * u * hq:(2 * u + 1) * hq] - lam * o[(2 * u + 1) * hq:(2 * u + 2) * hq]
                y = _rms(d, subln_ref[...]) * (1.0 - lam_init)
                o_ref[u * hq:(u + 1) * hq, g * LANES:(g + 1) * LANES] = y.astype(o_ref.dtype)

    n_full = i * sub
    reset()
    diagonal()

    def pair(t, carry):
        frozen_step(2 * t)
        frozen_step(2 * t + 1)
        return carry

    lax.fori_loop(0, n_full // 2, pair, 0)

    @pl.when(n_full % 2 == 1)
    def _():
        frozen_step(n_full - 1)

    finite = jnp.minimum(jnp.min(jnp.where(l_ref[...] < FINITE_BOUND, 1.0, 0.0)),
                         jnp.min(jnp.where(jnp.abs(acc_ref[...]) < FINITE_BOUND, 1.0, 0.0)))

    @pl.when(finite > 0.5)
    def _():
        finish()

    @pl.when(finite <= 0.5)
    def _():
        reset()

        def body(j, carry):
            step(pl.multiple_of(j * tk, tk), tk, None)
            return carry

        lax.fori_loop(0, n_full, body, 0)
        diagonal()
        finish()


def _diff_attn(proj, lam_params, subln, lam_init, *, tq=512, tk=512, heads=4):
    s = proj.shape[0]
    width = A_HEADS * LANES
    hw = heads * LANES
    groups = A_HEADS // heads
    half_rows = tq // 2
    assert tq % tk == 0 and half_rows & (half_rows - 1) == 0, (tq, tk)
    kern = functools.partial(_diff_attn_kernel, tq=tq, tk=tk, heads=heads, lam_init=lam_init)
    return pl.pallas_call(
        kern,
        grid=(groups, s // tq),
        in_specs=[pl.BlockSpec((4, ROPE_DIM), lambda h, i: (0, 0)),
                  pl.BlockSpec((1, LANES), lambda h, i: (0, 0)),
                  pl.BlockSpec((tq, hw), lambda h, i: (i, h)),
                  pl.BlockSpec((s, hw), lambda h, i: (0, groups + h),
                               pipeline_mode=pl.Buffered(1)),
                  pl.BlockSpec((s, hw), lambda h, i: (0, 2 * groups + h),
                               pipeline_mode=pl.Buffered(1))],
        out_specs=pl.BlockSpec((tq, hw), lambda h, i: (i, h)),
        out_shape=jax.ShapeDtypeStruct((s, width), BF16),
        scratch_shapes=[pltpu.VMEM((heads, 2 * tq, LANES), BF16)]
                       + [pltpu.VMEM((heads, 2 * tq, LANES), F32)] * 3,
        compiler_params=_params(("arbitrary", "arbitrary"),
                                ((tq, hw), BF16, 4), ((s, hw), BF16, 2),
                                ((heads, 2 * tq, LANES), BF16, 1), ((heads, 2 * tq, LANES), F32, 3)),
        name="diff_attn",
    )(lam_params, subln.reshape(1, LANES), proj, proj, proj)


def _stickbreak_kernel(q_ref, k_ref, v_ref, o_ref, tri_ref, carry_ref, acc_ref, *,
                       tq, tk, heads, scale):
    h = pl.program_id(0)
    i = pl.program_id(1)
    sub = tq // tk

    @pl.when((h == 0) & (i == 0))
    def _():
        r = lax.broadcasted_iota(jnp.int32, tri_ref.shape, 0)
        c = lax.broadcasted_iota(jnp.int32, tri_ref.shape, 1)
        tri_ref[...] = jnp.where(r > c, 1.0, 0.0).astype(BF16)

    carry_ref[...] = jnp.zeros(carry_ref.shape, F32)
    acc_ref[...] = jnp.zeros(acc_ref.shape, F32)

    def step(j, diag_block):
        start = pl.multiple_of(j * tk, tk)
        rows = slice(None) if diag_block is None else slice(diag_block * tk, tq)
        for g in range(heads):
            kb = _head(k_ref, g, pl.ds(start, tk))
            vb = _head(v_ref, g, pl.ds(start, tk))
            qk = lax.dot_general(_head(q_ref, g, rows), kb, NT_DIMS, preferred_element_type=F32)
            z = qk * scale
            nz = qk * (-scale)
            lb = jnp.minimum(nz, 0.0) - jnp.log2(1.0 + jnp.exp2(jnp.minimum(z, nz)))
            if diag_block is not None:
                row = lax.broadcasted_iota(jnp.int32, z.shape, 0)
                col = lax.broadcasted_iota(jnp.int32, z.shape, 1)
                strict = col < row
                lb = jnp.where(strict, lb, 0.0)
            later = jnp.dot(lb.astype(BF16), tri_ref[...], preferred_element_type=F32)
            e = z + lb + later + _lane_tile(carry_ref[g, rows], tk)
            if diag_block is not None:
                e = jnp.where(strict, e, MASK_VALUE)
            w = jnp.exp2(e)
            acc_ref[g, rows] += jnp.dot(w.astype(BF16), vb, preferred_element_type=F32)
            carry_ref[g, rows] += jnp.sum(lb, axis=1, keepdims=True)

    for u in reversed(range(sub)):
        step(i * sub + u, u)

    def more(state):
        j, top = state
        return (j >= 0) & (top > EXP2_UNDERFLOW)

    def body(state):
        j, _ = state
        step(j, None)
        return j - 1, jnp.max(carry_ref[...])

    lax.while_loop(more, body, (i * sub - 1, jnp.float32(0.0)))
    for g in range(heads):
        o_ref[:, g * LANES:(g + 1) * LANES] = acc_ref[g].astype(o_ref.dtype)


def _stickbreak(proj, *, first_col, tq=512, tk=256, heads=4):
    s = proj.shape[0]
    width = B_HEADS * B_HEAD_DIM
    hw = heads * LANES
    groups = B_HEADS // heads
    c0 = first_col // hw
    kern = functools.partial(_stickbreak_kernel, tq=tq, tk=tk, heads=heads,
                             scale=B_HEAD_DIM ** -0.5 * math.log2(math.e))
    return pl.pallas_call(
        kern,
        grid=(groups, s // tq),
        in_specs=[pl.BlockSpec((tq, hw), lambda h, i: (i, c0 + h)),
                  pl.BlockSpec((s, hw), lambda h, i: (0, c0 + groups + h)),
                  pl.BlockSpec((s, hw), lambda h, i: (0, c0 + 2 * groups + h))],
        out_specs=pl.BlockSpec((tq, hw), lambda h, i: (i, h)),
        out_shape=jax.ShapeDtypeStruct((s, width), BF16),
        scratch_shapes=[pltpu.VMEM((tk, tk), BF16),
                        pltpu.VMEM((heads, tq, LANES), F32),
                        pltpu.VMEM((heads, tq, LANES), F32)],
        compiler_params=_params(("arbitrary", "arbitrary"),
                                ((tq, hw), BF16, 4), ((s, hw), BF16, 4),
                                ((tk, tk), BF16, 1), ((heads, tq, LANES), F32, 2)),
        name="stickbreak",
    )(proj, proj, proj)


def _swa_kernel(sinks_ref, q_ref, kp_ref, kc_ref, vp_ref, vc_ref, o_ref, *, blocks):
    hk = pl.program_id(0)
    n = pl.program_id(1)
    w = WINDOW
    half = LANES // 2

    def lane_halves(prev_ref, cur_ref):
        both = jnp.concatenate([prev_ref[...], cur_ref[...]], axis=0).astype(F32)
        low = lax.broadcasted_iota(jnp.int32, both.shape, 1) < half
        return jnp.where(low, both, 0.0).astype(BF16), jnp.where(low, 0.0, both).astype(BF16)

    k_lo, k_hi = lane_halves(kp_ref, kc_ref)
    v_lo, v_hi = lane_halves(vp_ref, vc_ref)

    qi = lax.broadcasted_iota(jnp.int32, (w, 2 * w), 0)
    ki = lax.broadcasted_iota(jnp.int32, (w, 2 * w), 1)
    band = (ki > qi) & (ki <= qi + w)
    first_real = jnp.where(n > 0, 0, w)
    band_first = band & (ki >= first_real)
    lane_o = lax.broadcasted_iota(jnp.int32, (w, LANES), 1)
    log2e = math.log2(math.e)

    for b in range(blocks):
        keys = slice(b * w, (b + 2) * w)
        kbd = jnp.concatenate([k_lo[keys], k_hi[keys]], axis=0)
        vbd = jnp.concatenate([v_lo[keys], v_hi[keys]], axis=0)
        mask = band_first if b == 0 else band
        pairs = C_GROUP // 2
        qs = jnp.concatenate([q_ref[b * w:(b + 1) * w, p * LANES:(p + 1) * LANES]
                              for p in range(pairs)], axis=0)
        s = lax.dot_general(qs, kbd, NT_DIMS, preferred_element_type=F32)
        probs, scales = [], []
        for p in range(pairs):
            row_p, inv_den = [], []
            for t in range(2):
                st = jnp.where(mask, s[p * w:(p + 1) * w, t * 2 * w:(t + 1) * 2 * w], MASK_VALUE)
                sink = sinks_ref[hk * C_GROUP + 2 * p + t] * log2e
                m = jnp.maximum(jnp.max(st, axis=1, keepdims=True), sink)
                pt = jnp.exp2(st - m)
                den = jnp.sum(pt, axis=1, keepdims=True) + jnp.exp2(sink - m)
                row_p.append(pt.astype(BF16))
                inv_den.append(1.0 / den)
            probs.append(jnp.concatenate(row_p, axis=1))
            scales.append(jnp.where(lane_o < half, inv_den[0], inv_den[1]))
        o = jnp.dot(jnp.concatenate(probs, axis=0), vbd, preferred_element_type=F32)
        outs = [o[p * w:(p + 1) * w] * scales[p] for p in range(pairs)]
        o_ref[b * w:(b + 1) * w, :] = jnp.concatenate(outs, axis=1).astype(o_ref.dtype)


def _swa(proj, sinks, *, blocks=4):
    s = proj.shape[0]
    w = WINDOW
    tq = blocks * w
    qw = C_GROUP * C_HEAD_DIM
    k0 = C_Q_HEADS * C_HEAD_DIM // LANES
    v0 = k0 + C_KV_HEADS
    prev = lambda n: jnp.maximum(n * blocks - 1, 0)
    return pl.pallas_call(
        functools.partial(_swa_kernel, blocks=blocks),
        grid=(C_KV_HEADS, s // tq),
        in_specs=[pl.BlockSpec(memory_space=pltpu.SMEM),
                  pl.BlockSpec((tq, qw), lambda h, n: (n, h)),
                  pl.BlockSpec((w, LANES), lambda h, n: (prev(n), k0 + h)),
                  pl.BlockSpec((tq, LANES), lambda h, n: (n, k0 + h)),
                  pl.BlockSpec((w, LANES), lambda h, n: (prev(n), v0 + h)),
                  pl.BlockSpec((tq, LANES), lambda h, n: (n, v0 + h))],
        out_specs=pl.BlockSpec((tq, qw), lambda h, n: (n, h)),
        out_shape=jax.ShapeDtypeStruct((s, C_Q_HEADS * C_HEAD_DIM), BF16),
        compiler_params=_params(("arbitrary", "arbitrary"),
                                ((tq, qw), BF16, 4), ((tq + w, LANES), BF16, 4)),
        name="swa_sink",
    )(sinks, proj, proj, proj, proj, proj)


def _out_proj_kernel(*refs, n_in):
    a_refs = refs[:n_in]
    w_ref, x_ref, o_ref, wb_ref = refs[n_in:]

    @pl.when(pl.program_id(0) == 0)
    def _():
        wb_ref[...] = w_ref[...].astype(BF16)

    acc = x_ref[...]
    row = 0
    for a_ref in a_refs:
        ka = a_ref.shape[1]
        acc = acc + jnp.dot(a_ref[...], wb_ref[row:row + ka, :], preferred_element_type=F32)
        row += ka
    o_ref[...] = acc


def _out_proj(acts, w, layer, x, *, tm=512):
    s, d = x.shape
    n_in = len(acts)
    kw = w.shape[1]
    a_specs = [pl.BlockSpec((tm, a.shape[1]), lambda i: (i, 0)) for a in acts]
    w_spec = pl.BlockSpec((None, kw, d), lambda i: (layer, 0, 0), pipeline_mode=pl.Buffered(1))
    return pl.pallas_call(
        functools.partial(_out_proj_kernel, n_in=n_in),
        grid=(s // tm,),
        in_specs=a_specs + [w_spec, pl.BlockSpec((tm, d), lambda i: (i, 0))],
        out_specs=pl.BlockSpec((tm, d), lambda i: (i, 0)),
        out_shape=jax.ShapeDtypeStruct((s, d), F32),
        scratch_shapes=[pltpu.VMEM((kw, d), BF16)],
        compiler_params=_params(("arbitrary",),
                                ((tm, kw), BF16, 2), ((kw, d), F32, 1), ((kw, d), BF16, 1),
                                ((tm, d), F32, 4)),
        name="out_proj",
    )(*acts, w, x)


def _mlp_kernel(x_ref, g_ref, wu_ref, wd_ref, gf_ref, o_ref, xn_ref, *, final_norm):
    f = pl.program_id(1)

    def up_down(xn):
        hid = jnp.dot(xn, wu_ref[...].astype(BF16), preferred_element_type=F32)
        hid = jnp.square(jnp.maximum(hid, 0.0)).astype(BF16)
        return jnp.dot(hid, wd_ref[...].astype(BF16), preferred_element_type=F32)

    @pl.when(f == 0)
    def _():
        x = x_ref[...]
        xn = _rms(x, g_ref[...]).astype(BF16)
        xn_ref[...] = xn
        o_ref[...] = x + up_down(xn)

    @pl.when(f > 0)
    def _():
        o_ref[...] += up_down(xn_ref[...])

    if final_norm:
        @pl.when(f == pl.num_programs(1) - 1)
        def _():
            o_ref[...] = _rms(o_ref[...], gf_ref[...])


def _mlp(x, g, w_up, w_down, layer, g_final, *, final_norm, tm=1024, tf=512):
    s, d = x.shape
    dff = w_up.shape[2]
    return pl.pallas_call(
        functools.partial(_mlp_kernel, final_norm=final_norm),
        grid=(s // tm, dff // tf),
        in_specs=[pl.BlockSpec((tm, d), lambda i, f: (i, 0)),
                  pl.BlockSpec((1, d), lambda i, f: (0, 0)),
                  pl.BlockSpec((None, d, tf), lambda i, f: (layer, 0, f)),
                  pl.BlockSpec((None, tf, d), lambda i, f: (layer, f, 0)),
                  pl.BlockSpec((1, d), lambda i, f: (0, 0))],
        out_specs=pl.BlockSpec((tm, d), lambda i, f: (i, 0)),
        out_shape=jax.ShapeDtypeStruct((s, d), F32),
        scratch_shapes=[pltpu.VMEM((tm, d), BF16)],
        compiler_params=_params(("arbitrary", "arbitrary"),
                                ((tm, d), F32, 4), ((d, tf), F32, 4), ((tm, d), BF16, 1)),
        name="mlp",
    )(x, g.reshape(1, d), w_up, w_down, g_final.reshape(1, d))


def kernel(x, positions, norm_mix, norm_mlp, norm_final, w_in_ab, w_out_ab, lambda_q1, lambda_k1,
           lambda_q2, lambda_k2, diff_subln, w_in_c, w_out_c, sinks, w_up, w_down):
    b, s, d = x.shape
    assert b == 1
    depth = norm_mix.shape[0]
    h = x.reshape(s, d)
    cos, sin = _rope_tables(positions)

    a_width = A_HEADS * LANES
    c_q = C_Q_HEADS * C_HEAD_DIM
    c_kv = C_KV_HEADS * C_HEAD_DIM
    for layer in range(depth):
        j = layer // 2
        if layer % 2 == 0:
            proj = _norm_proj(h, norm_mix[layer], w_in_ab, j, cos, sin,
                              n_main=w_in_ab.shape[2], scaled_cols=a_width, rope_cols=2 * a_width,
                              scale=ROPE_DIM ** -0.5 * math.log2(math.e))
            lam_init = 0.8 - 0.6 * math.exp(-0.3 * layer)
            lam_params = jnp.stack([lambda_q1[j], lambda_k1[j], lambda_q2[j], lambda_k2[j]])
            oa = _diff_attn(proj, lam_params, diff_subln[j], lam_init)
            ob = _stickbreak(proj, first_col=3 * a_width)
            h = _out_proj([oa, ob], w_out_ab, j, h)
        else:
            proj = _norm_proj(h, norm_mix[layer], w_in_c, j, cos, sin, tail_rope_cols=c_kv,
                              n_main=c_q, scaled_cols=c_q, rope_cols=c_q,
                              scale=C_HEAD_DIM ** -0.5 * math.log2(math.e))
            oc = _swa(proj, sinks[j])
            h = _out_proj([oc], w_out_c, j, h)
        h = _mlp(h, norm_mlp[layer], w_up, w_down, layer,
                 norm_final, final_norm=(layer == depth - 1))
    return h.reshape(b, s, d)
```

```python
import functools
import math

import jax
import jax.numpy as jnp
import numpy as np
from jax import lax
from jax.experimental import pallas as pl
from jax.experimental.pallas import tpu as pltpu

F32 = jnp.float32
BF16 = jnp.bfloat16

EPS = 1e-6
ROPE_THETA = 10000.0
LANES = 128
ROPE_DIM = 64
A_HEADS = 8
B_HEADS = 8
B_HEAD_DIM = 128
C_Q_HEADS = 32
C_KV_HEADS = 4
C_GROUP = C_Q_HEADS // C_KV_HEADS
C_HEAD_DIM = 64
WINDOW = 128
MASK_VALUE = -1e30
EXP2_UNDERFLOW = -160.0
FINITE_BOUND = 1e30
MIB = 1024 * 1024

NT_DIMS = (((1,), (1,)), ((), ()))


VMEM_CAPACITY_MIB = 64
VMEM_RESERVED_MIB = 4


def _params(semantics, *buffers):
    need = sum(math.prod(shape) * jnp.dtype(dtype).itemsize * copies
               for shape, dtype, copies in buffers)
    limit = (VMEM_CAPACITY_MIB - VMEM_RESERVED_MIB) * MIB
    assert need <= limit, (need, limit)
    return pltpu.CompilerParams(dimension_semantics=semantics, vmem_limit_bytes=limit)


def _rms(x, g):
    ms = jnp.mean(x * x, axis=-1, keepdims=True)
    return x * lax.rsqrt(ms + EPS) * g


def _lane_tile(a, width):
    reps = width // LANES
    return a if reps == 1 else jnp.concatenate([a] * reps, axis=1)


def _head(ref, g, rows=slice(None)):
    return ref[rows, g * LANES:(g + 1) * LANES]


def _rope_table_kernel(pos_ref, invf_ref, cos_ref, sin_ref):
    ang = pos_ref[...].astype(F32) * invf_ref[...]
    lane = lax.broadcasted_iota(jnp.int32, ang.shape, 1)
    sign = jnp.where((lane & (ROPE_DIM // 2)) == 0, -1.0, 1.0)
    cos_ref[...] = jnp.cos(ang)
    sin_ref[...] = jnp.sin(ang) * sign


def _rope_tables(positions):
    s = positions.shape[-1]
    tm = 1024
    inv_freq = ROPE_THETA ** (-jnp.arange(0, ROPE_DIM, 2, dtype=F32) / ROPE_DIM)
    invf = jnp.tile(inv_freq, LANES // (ROPE_DIM // 2)).reshape(1, LANES)
    pos = positions.reshape(s, 1)
    return pl.pallas_call(
        _rope_table_kernel,
        grid=(s // tm,),
        in_specs=[pl.BlockSpec((tm, 1), lambda i: (i, 0)),
                  pl.BlockSpec((1, LANES), lambda i: (0, 0))],
        out_specs=[pl.BlockSpec((tm, LANES), lambda i: (i, 0))] * 2,
        out_shape=[jax.ShapeDtypeStruct((s, LANES), F32)] * 2,
        compiler_params=_params(("arbitrary",), ((tm, LANES), F32, 6)),
        name="rope_tables",
    )(pos, invf)


def _rope(x, cos, ss):
    half = ROPE_DIM // 2
    lane = lax.broadcasted_iota(jnp.int32, cos.shape, 1)
    low = (lane & half) == 0
    outs = []
    for c in range(x.shape[1] // LANES):
        xc = x[:, c * LANES:(c + 1) * LANES]
        partner = jnp.where(low, pltpu.roll(xc, LANES - half, 1), pltpu.roll(xc, half, 1))
        outs.append(xc * cos + partner * ss)
    return outs[0] if len(outs) == 1 else jnp.concatenate(outs, axis=1)


PROJ_CHUNK = 512


def _norm_proj_kernel(*refs, tile_runs, scale, tail_rope_cols):
    x_ref, g_ref = refs[:2]
    w_refs = refs[2:4]
    tail_ref, dup_ref = refs[4:6] if tail_rope_cols is not None else (None, None)
    cos_ref, sin_ref, o_ref, xn_ref = refs[-4:]
    j = pl.program_id(1)

    def chunk(xn, c, kind):
        cols = slice(c * PROJ_CHUNK, (c + 1) * PROJ_CHUNK)
        acc = jnp.dot(xn, w_refs[c][...].astype(BF16), preferred_element_type=F32)
        if kind != "plain":
            acc = _rope(acc, cos_ref[...], sin_ref[...])
        if kind == "scaled":
            acc = acc * scale
        o_ref[:, cols] = acc.astype(o_ref.dtype)

    @pl.when(j == 0)
    def _():
        xn = _rms(x_ref[...], g_ref[...]).astype(BF16)
        xn_ref[...] = xn
        for c, kind in enumerate(tile_runs[0][2]):
            chunk(xn, c, kind)

    for first, end, kinds in tile_runs:
        first = max(first, 1)
        if first < end:
            @pl.when((j >= first) & (j < end))
            def _(kinds=kinds):
                for c, kind in enumerate(kinds):
                    chunk(xn_ref[...], c, kind)

    if tail_rope_cols is not None:
        @pl.when(j == tile_runs[-1][1])
        def _():
            acc = jnp.dot(xn_ref[...], tail_ref[...].astype(BF16), preferred_element_type=F32)
            roped = _rope(acc[:, :tail_rope_cols], cos_ref[...], sin_ref[...])
            narrow = jnp.concatenate([roped, acc[:, tail_rope_cols:]], axis=1).astype(BF16)
            o_ref[...] = jnp.dot(narrow, dup_ref[...],
                                 preferred_element_type=F32).astype(o_ref.dtype)


def _step_chunks(step, paired):
    if isinstance(step, int):
        return (step, paired + step) if step < paired else (2 * step, 2 * step + 1)
    first = step < paired
    return jnp.where(first, step, 2 * step), jnp.where(first, paired + step, 2 * step + 1)


def _proj_block(chunk, paired):
    if chunk < paired:
        return 2 * chunk
    if chunk < 2 * paired:
        return 2 * (chunk - paired) + 1
    return chunk


def _group_blocks(first_chunk, paired):
    base = _proj_block(first_chunk, paired)
    return base, _proj_block(first_chunk + 1, paired) - base


def _tile_runs(n_steps, paired, scaled_cols, rope_cols):
    def kind(chunk):
        col = chunk * PROJ_CHUNK
        return "scaled" if col < scaled_cols else "rope" if col < rope_cols else "plain"
    patterns = [tuple(kind(int(c)) for c in _step_chunks(t, paired)) for t in range(n_steps)]
    runs, first = [], 0
    for t in range(1, n_steps + 1):
        if t == n_steps or patterns[t] != patterns[first]:
            runs.append((first, t, patterns[first]))
            first = t
    return tuple(runs)


def _head_dup_matrix(n_in, head_dim):
    out_col = np.arange(2 * n_in)
    src = (out_col // (2 * head_dim)) * head_dim + out_col % head_dim
    return jnp.asarray(np.arange(n_in)[:, None] == src[None, :], dtype=BF16)


def _norm_proj(x, g, w, layer, cos, sin, *, n_main, scaled_cols, rope_cols, scale, paired=0,
               tail_rope_cols=None, tm=1024):
    s, d = x.shape
    tn = 2 * PROJ_CHUNK
    n_chunks = n_main // PROJ_CHUNK
    main_tiles = n_chunks // 2
    has_tail = tail_rope_cols is not None
    n_tiles = main_tiles + has_tail
    kern = functools.partial(_norm_proj_kernel, scale=scale, tail_rope_cols=tail_rope_cols,
                             tile_runs=_tile_runs(main_tiles, paired, scaled_cols, rope_cols))

    def w_spec(which):
        return pl.BlockSpec((None, d, PROJ_CHUNK), lambda i, j: (
            layer, 0, jnp.minimum(_step_chunks(j, paired)[which], n_chunks - 2 + which)))

    w_specs = [w_spec(0), w_spec(1)]
    weights = [w, w]
    buffers = [((tm, d), F32, 2), ((d, tn), F32, 2), ((tm, LANES), F32, 4), ((tm, tn), BF16, 2),
               ((tm, d), BF16, 1)]
    if has_tail:
        half = tn // 2
        once = pl.Buffered(1)
        w_specs += [pl.BlockSpec((None, d, half), lambda i, j: (layer, 0, n_main // half),
                                 pipeline_mode=once),
                    pl.BlockSpec((half, tn), lambda i, j: (0, 0), pipeline_mode=once)]
        weights += [w, _head_dup_matrix(half, C_HEAD_DIM)]
        buffers += [((d, half), F32, 1), ((half, tn), BF16, 1)]
    return pl.pallas_call(
        kern,
        grid=(s // tm, n_tiles),
        in_specs=[pl.BlockSpec((tm, d), lambda i, j: (i, 0)),
                  pl.BlockSpec((1, d), lambda i, j: (0, 0))] + w_specs +
                 [pl.BlockSpec((tm, LANES), lambda i, j: (i, 0)),
                  pl.BlockSpec((tm, LANES), lambda i, j: (i, 0))],
        out_specs=pl.BlockSpec((tm, tn), lambda i, j: (i, j)),
        out_shape=jax.ShapeDtypeStruct((s, n_tiles * tn), BF16),
        scratch_shapes=[pltpu.VMEM((tm, d), BF16)],
        compiler_params=_params(("arbitrary", "arbitrary"), *buffers),
        name="norm_proj",
    )(x, g.reshape(1, d), *weights, cos, sin)


def _diff_attn_kernel(lam_ref, subln_ref, q_ref, k_ref, v_ref, o_ref,
                      qs_ref, m_ref, l_ref, acc_ref, *, tq, tk, heads, lam_init):
    i = pl.program_id(1)
    rows = 2 * tq
    half = LANES // 2
    hq = tq // 2
    sub = tq // tk

    for g in range(heads):
        q = _head(q_ref, g).astype(F32)
        lane = lax.broadcasted_iota(jnp.int32, q.shape, 1)
        q1 = jnp.where(lane < half, q, 0.0).astype(BF16)
        q2 = jnp.where(lane >= half, q, 0.0).astype(BF16)
        for part, src in enumerate((q1[0:hq], q2[0:hq], q1[hq:tq], q2[hq:tq])):
            qs_ref[g, part * hq:(part + 1) * hq, :] = src

    def reset():
        m_ref[...] = jnp.full(m_ref.shape, MASK_VALUE, F32)
        l_ref[...] = jnp.zeros(l_ref.shape, F32)
        acc_ref[...] = jnp.zeros(acc_ref.shape, F32)

    def step(start, width, diag_half):
        first = 0 if diag_half is None else diag_half * tq
        rs = slice(first, rows)
        for g in range(heads):
            kb = _head(k_ref, g, pl.ds(start, width))
            vb = _head(v_ref, g, pl.ds(start, width))
            s = lax.dot_general(qs_ref[g, rs], kb, NT_DIMS, preferred_element_type=F32)
            if diag_half is not None:
                r = lax.broadcasted_iota(jnp.int32, s.shape, 0) + first
                qrow = (r & (hq - 1)) + jnp.where(r >= tq, hq, 0)
                col = lax.broadcasted_iota(jnp.int32, s.shape, 1) + diag_half * hq
                s = jnp.where(col <= qrow, s, MASK_VALUE)
            m_prev = m_ref[g, rs]
            m_new = jnp.maximum(m_prev, jnp.max(s, axis=1, keepdims=True))
            alpha = jnp.exp2(m_prev - m_new)
            p = jnp.exp2(s - _lane_tile(m_new, width))
            l_ref[g, rs] = alpha * l_ref[g, rs] + jnp.sum(p, axis=1, keepdims=True)
            acc_ref[g, rs] = alpha * acc_ref[g, rs] + jnp.dot(p.astype(BF16), vb,
                                                              preferred_element_type=F32)
            m_ref[g, rs] = m_new

    def frozen_step(j):
        start = pl.multiple_of(j * tk, tk)
        for g in range(heads):
            kb = _head(k_ref, g, pl.ds(start, tk))
            vb = _head(v_ref, g, pl.ds(start, tk))
            s = lax.dot_general(qs_ref[g], kb, NT_DIMS, preferred_element_type=F32)
            p = jnp.exp2(s - _lane_tile(m_ref[g], tk))
            l_ref[g] += jnp.sum(p, axis=1, keepdims=True)
            acc_ref[g] += jnp.dot(p.astype(BF16), vb, preferred_element_type=F32)

    def diagonal():
        for u in range(2):
            step(pl.multiple_of(i * tq + u * hq, hq), hq, u)

    def finish():
        lp = lam_ref[...]
        lam = (jnp.exp(jnp.sum(lp[0:1] * lp[1:2], axis=1, keepdims=True))
               - jnp.exp(jnp.sum(lp[2:3] * lp[3:4], axis=1, keepdims=True)) + lam_init)
        for g in range(heads):
            o = acc_ref[g] / l_ref[g]
            for u in range(2):
                d = o[2 * u * hq:(2 * u + 1) * hq] - lam * o[(2 * u + 1) * hq:(2 * u + 2) * hq]
                y = _rms(d, subln_ref[...]) * (1.0 - lam_init)
                o_ref[u * hq:(u + 1) * hq, g * LANES:(g + 1) * LANES] = y.astype(o_ref.dtype)

    n_full = i * sub
    reset()
    diagonal()

    def pair(t, carry):
        frozen_step(2 * t)
        frozen_step(2 * t + 1)
        return carry

    lax.fori_loop(0, n_full // 2, pair, 0)

    @pl.when(n_full % 2 == 1)
    def _():
        frozen_step(n_full - 1)

    finite = jnp.minimum(jnp.min(jnp.where(l_ref[...] < FINITE_BOUND, 1.0, 0.0)),
                         jnp.min(jnp.where(jnp.abs(acc_ref[...]) < FINITE_BOUND, 1.0, 0.0)))

    @pl.when(finite > 0.5)
    def _():
        finish()

    @pl.when(finite <= 0.5)
    def _():
        reset()

        def body(j, carry):
            step(pl.multiple_of(j * tk, tk), tk, None)
            return carry

        lax.fori_loop(0, n_full, body, 0)
        diagonal()
        finish()


def _diff_attn(proj, lam_params, subln, lam_init, *, q_at, k_at, v_at, tq=512, tk=512, heads=4):
    s = proj.shape[0]
    width = A_HEADS * LANES
    hw = heads * LANES
    groups = A_HEADS // heads
    half_rows = tq // 2
    assert tq % tk == 0 and half_rows & (half_rows - 1) == 0, (tq, tk)
    assert hw == PROJ_CHUNK and groups == 2
    kern = functools.partial(_diff_attn_kernel, tq=tq, tk=tk, heads=heads, lam_init=lam_init)
    return pl.pallas_call(
        kern,
        grid=(groups, s // tq),
        in_specs=[pl.BlockSpec((4, ROPE_DIM), lambda h, i: (0, 0)),
                  pl.BlockSpec((1, LANES), lambda h, i: (0, 0)),
                  pl.BlockSpec((tq, hw), lambda h, i: (i, q_at[0] + q_at[1] * h)),
                  pl.BlockSpec((s, hw), lambda h, i: (0, k_at[0] + k_at[1] * h),
                               pipeline_mode=pl.Buffered(1)),
                  pl.BlockSpec((s, hw), lambda h, i: (0, v_at[0] + v_at[1] * h),
                               pipeline_mode=pl.Buffered(1))],
        out_specs=pl.BlockSpec((tq, hw), lambda h, i: (i, h)),
        out_shape=jax.ShapeDtypeStruct((s, width), BF16),
        scratch_shapes=[pltpu.VMEM((heads, 2 * tq, LANES), BF16)]
                       + [pltpu.VMEM((heads, 2 * tq, LANES), F32)] * 3,
        compiler_params=_params(("arbitrary", "arbitrary"),
                                ((tq, hw), BF16, 4), ((s, hw), BF16, 2),
                                ((heads, 2 * tq, LANES), BF16, 1), ((heads, 2 * tq, LANES), F32, 3)),
        name="diff_attn",
    )(lam_params, subln.reshape(1, LANES), proj, proj, proj)


def _stickbreak_kernel(q_ref, k_ref, v_ref, o_ref, tri_ref, carry_ref, acc_ref, *,
                       tq, tk, heads, scale):
    h = pl.program_id(0)
    i = pl.program_id(1)
    sub = tq // tk

    @pl.when((h == 0) & (i == 0))
    def _():
        r = lax.broadcasted_iota(jnp.int32, tri_ref.shape, 0)
        c = lax.broadcasted_iota(jnp.int32, tri_ref.shape, 1)
        tri_ref[...] = jnp.where(r > c, 1.0, 0.0).astype(BF16)

    carry_ref[...] = jnp.zeros(carry_ref.shape, F32)
    acc_ref[...] = jnp.zeros(acc_ref.shape, F32)

    def step(j, diag_block):
        start = pl.multiple_of(j * tk, tk)
        rows = slice(None) if diag_block is None else slice(diag_block * tk, tq)
        for g in range(heads):
            kb = _head(k_ref, g, pl.ds(start, tk))
            vb = _head(v_ref, g, pl.ds(start, tk))
            qk = lax.dot_general(_head(q_ref, g, rows), kb, NT_DIMS, preferred_element_type=F32)
            z = qk * scale
            nz = qk * (-scale)
            lb = jnp.minimum(nz, 0.0) - jnp.log2(1.0 + jnp.exp2(jnp.minimum(z, nz)))
            if diag_block is not None:
                row = lax.broadcasted_iota(jnp.int32, z.shape, 0)
                col = lax.broadcasted_iota(jnp.int32, z.shape, 1)
                strict = col < row
                lb = jnp.where(strict, lb, 0.0)
            later = jnp.dot(lb.astype(BF16), tri_ref[...], preferred_element_type=F32)
            e = z + lb + later + _lane_tile(carry_ref[g, rows], tk)
            if diag_block is not None:
                e = jnp.where(strict, e, MASK_VALUE)
            w = jnp.exp2(e)
            acc_ref[g, rows] += jnp.dot(w.astype(BF16), vb, preferred_element_type=F32)
            carry_ref[g, rows] += jnp.sum(lb, axis=1, keepdims=True)

    for u in reversed(range(sub)):
        step(i * sub + u, u)

    def more(state):
        j, top = state
        return (j >= 0) & (top > EXP2_UNDERFLOW)

    def body(state):
        j, _ = state
        step(j, None)
        return j - 1, jnp.max(carry_ref[...])

    lax.while_loop(more, body, (i * sub - 1, jnp.float32(0.0)))
    for g in range(heads):
        o_ref[:, g * LANES:(g + 1) * LANES] = acc_ref[g].astype(o_ref.dtype)


def _stickbreak(proj, *, q_at, k_at, v_at, tq=512, tk=256, heads=4):
    s = proj.shape[0]
    width = B_HEADS * B_HEAD_DIM
    hw = heads * LANES
    groups = B_HEADS // heads
    assert hw == PROJ_CHUNK and groups == 2
    kern = functools.partial(_stickbreak_kernel, tq=tq, tk=tk, heads=heads,
                             scale=B_HEAD_DIM ** -0.5 * math.log2(math.e))
    return pl.pallas_call(
        kern,
        grid=(groups, s // tq),
        in_specs=[pl.BlockSpec((tq, hw), lambda h, i: (i, q_at[0] + q_at[1] * h)),
                  pl.BlockSpec((s, hw), lambda h, i: (0, k_at[0] + k_at[1] * h)),
                  pl.BlockSpec((s, hw), lambda h, i: (0, v_at[0] + v_at[1] * h))],
        out_specs=pl.BlockSpec((tq, hw), lambda h, i: (i, h)),
        out_shape=jax.ShapeDtypeStruct((s, width), BF16),
        scratch_shapes=[pltpu.VMEM((tk, tk), BF16),
                        pltpu.VMEM((heads, tq, LANES), F32),
                        pltpu.VMEM((heads, tq, LANES), F32)],
        compiler_params=_params(("arbitrary", "arbitrary"),
                                ((tq, hw), BF16, 4), ((s, hw), BF16, 4),
                                ((tk, tk), BF16, 1), ((heads, tq, LANES), F32, 2)),
        name="stickbreak",
    )(proj, proj, proj)


def _swa_kernel(sinks_ref, q_ref, kp_ref, kc_ref, vp_ref, vc_ref, o_ref, *, blocks):
    hk = pl.program_id(0)
    n = pl.program_id(1)
    w = WINDOW
    half = LANES // 2

    def lane_halves(prev_ref, cur_ref):
        both = jnp.concatenate([prev_ref[...], cur_ref[...]], axis=0).astype(F32)
        low = lax.broadcasted_iota(jnp.int32, both.shape, 1) < half
        return jnp.where(low, both, 0.0).astype(BF16), jnp.where(low, 0.0, both).astype(BF16)

    k_lo, k_hi = lane_halves(kp_ref, kc_ref)
    v_lo, v_hi = lane_halves(vp_ref, vc_ref)

    qi = lax.broadcasted_iota(jnp.int32, (w, 2 * w), 0)
    ki = lax.broadcasted_iota(jnp.int32, (w, 2 * w), 1)
    band = (ki > qi) & (ki <= qi + w)
    first_real = jnp.where(n > 0, 0, w)
    band_first = band & (ki >= first_real)
    lane_o = lax.broadcasted_iota(jnp.int32, (w, LANES), 1)
    log2e = math.log2(math.e)

    for b in range(blocks):
        keys = slice(b * w, (b + 2) * w)
        kbd = jnp.concatenate([k_lo[keys], k_hi[keys]], axis=0)
        vbd = jnp.concatenate([v_lo[keys], v_hi[keys]], axis=0)
        mask = band_first if b == 0 else band
        pairs = C_GROUP // 2
        qs = jnp.concatenate([q_ref[b * w:(b + 1) * w, p * LANES:(p + 1) * LANES]
                              for p in range(pairs)], axis=0)
        s = lax.dot_general(qs, kbd, NT_DIMS, preferred_element_type=F32)
        probs, scales = [], []
        for p in range(pairs):
            row_p, inv_den = [], []
            for t in range(2):
                st = jnp.where(mask, s[p * w:(p + 1) * w, t * 2 * w:(t + 1) * 2 * w], MASK_VALUE)
                sink = sinks_ref[hk * C_GROUP + 2 * p + t] * log2e
                m = jnp.maximum(jnp.max(st, axis=1, keepdims=True), sink)
                pt = jnp.exp2(st - m)
                den = jnp.sum(pt, axis=1, keepdims=True) + jnp.exp2(sink - m)
                row_p.append(pt.astype(BF16))
                inv_den.append(1.0 / den)
            probs.append(jnp.concatenate(row_p, axis=1))
            scales.append(jnp.where(lane_o < half, inv_den[0], inv_den[1]))
        o = jnp.dot(jnp.concatenate(probs, axis=0), vbd, preferred_element_type=F32)
        outs = [o[p * w:(p + 1) * w] * scales[p] for p in range(pairs)]
        o_ref[b * w:(b + 1) * w, :] = jnp.concatenate(outs, axis=1).astype(o_ref.dtype)


def _swa(proj, sinks, *, blocks=4):
    s = proj.shape[0]
    w = WINDOW
    tq = blocks * w
    qw = C_GROUP * C_HEAD_DIM
    k0 = C_Q_HEADS * C_HEAD_DIM // LANES
    v0 = k0 + C_KV_HEADS
    prev = lambda n: jnp.maximum(n * blocks - 1, 0)
    return pl.pallas_call(
        functools.partial(_swa_kernel, blocks=blocks),
        grid=(C_KV_HEADS, s // tq),
        in_specs=[pl.BlockSpec(memory_space=pltpu.SMEM),
                  pl.BlockSpec((tq, qw), lambda h, n: (n, h)),
                  pl.BlockSpec((w, LANES), lambda h, n: (prev(n), k0 + h)),
                  pl.BlockSpec((tq, LANES), lambda h, n: (n, k0 + h)),
                  pl.BlockSpec((w, LANES), lambda h, n: (prev(n), v0 + h)),
                  pl.BlockSpec((tq, LANES), lambda h, n: (n, v0 + h))],
        out_specs=pl.BlockSpec((tq, qw), lambda h, n: (n, h)),
        out_shape=jax.ShapeDtypeStruct((s, C_Q_HEADS * C_HEAD_DIM), BF16),
        compiler_params=_params(("arbitrary", "arbitrary"),
                                ((tq, qw), BF16, 4), ((tq + w, LANES), BF16, 4)),
        name="swa_sink",
    )(sinks, proj, proj, proj, proj, proj)


def _out_proj_kernel(*refs, n_in):
    a_refs = refs[:n_in]
    w_ref, x_ref, o_ref, wb_ref = refs[n_in:]

    @pl.when(pl.program_id(0) == 0)
    def _():
        wb_ref[...] = w_ref[...].astype(BF16)

    acc = x_ref[...]
    row = 0
    for a_ref in a_refs:
        ka = a_ref.shape[1]
        acc = acc + jnp.dot(a_ref[...], wb_ref[row:row + ka, :], preferred_element_type=F32)
        row += ka
    o_ref[...] = acc


def _out_proj(acts, w, layer, x, *, tm=512):
    s, d = x.shape
    n_in = len(acts)
    kw = w.shape[1]
    a_specs = [pl.BlockSpec((tm, a.shape[1]), lambda i: (i, 0)) for a in acts]
    w_spec = pl.BlockSpec((None, kw, d), lambda i: (layer, 0, 0), pipeline_mode=pl.Buffered(1))
    return pl.pallas_call(
        functools.partial(_out_proj_kernel, n_in=n_in),
        grid=(s // tm,),
        in_specs=a_specs + [w_spec, pl.BlockSpec((tm, d), lambda i: (i, 0))],
        out_specs=pl.BlockSpec((tm, d), lambda i: (i, 0)),
        out_shape=jax.ShapeDtypeStruct((s, d), F32),
        scratch_shapes=[pltpu.VMEM((kw, d), BF16)],
        compiler_params=_params(("arbitrary",),
                                ((tm, kw), BF16, 2), ((kw, d), F32, 1), ((kw, d), BF16, 1),
                                ((tm, d), F32, 4)),
        name="out_proj",
    )(*acts, w, x)


def _mlp_kernel(x_ref, g_ref, wu_ref, wd_ref, gf_ref, o_ref, xn_ref, *, final_norm):
    f = pl.program_id(1)

    def up_down(xn):
        hid = jnp.dot(xn, wu_ref[...].astype(BF16), preferred_element_type=F32)
        hid = jnp.square(jnp.maximum(hid, 0.0)).astype(BF16)
        return jnp.dot(hid, wd_ref[...].astype(BF16), preferred_element_type=F32)

    @pl.when(f == 0)
    def _():
        x = x_ref[...]
        xn = _rms(x, g_ref[...]).astype(BF16)
        xn_ref[...] = xn
        o_ref[...] = x + up_down(xn)

    @pl.when(f > 0)
    def _():
        o_ref[...] += up_down(xn_ref[...])

    if final_norm:
        @pl.when(f == pl.num_programs(1) - 1)
        def _():
            o_ref[...] = _rms(o_ref[...], gf_ref[...])


def _mlp(x, g, w_up, w_down, layer, g_final, *, final_norm, tm=1024, tf=512):
    s, d = x.shape
    dff = w_up.shape[2]
    return pl.pallas_call(
        functools.partial(_mlp_kernel, final_norm=final_norm),
        grid=(s // tm, dff // tf),
        in_specs=[pl.BlockSpec((tm, d), lambda i, f: (i, 0)),
                  pl.BlockSpec((1, d), lambda i, f: (0, 0)),
                  pl.BlockSpec((None, d, tf), lambda i, f: (layer, 0, f)),
                  pl.BlockSpec((None, tf, d), lambda i, f: (layer, f, 0)),
                  pl.BlockSpec((1, d), lambda i, f: (0, 0))],
        out_specs=pl.BlockSpec((tm, d), lambda i, f: (i, 0)),
        out_shape=jax.ShapeDtypeStruct((s, d), F32),
        scratch_shapes=[pltpu.VMEM((tm, d), BF16)],
        compiler_params=_params(("arbitrary", "arbitrary"),
                                ((tm, d), F32, 4), ((d, tf), F32, 4), ((tm, d), BF16, 1)),
        name="mlp",
    )(x, g.reshape(1, d), w_up, w_down, g_final.reshape(1, d))


def kernel(x, positions, norm_mix, norm_mlp, norm_final, w_in_ab, w_out_ab, lambda_q1, lambda_k1,
           lambda_q2, lambda_k2, diff_subln, w_in_c, w_out_c, sinks, w_up, w_down):
    b, s, d = x.shape
    assert b == 1
    depth = norm_mix.shape[0]
    h = x.reshape(s, d)
    cos, sin = _rope_tables(positions)

    a_width = A_HEADS * LANES
    c_q = C_Q_HEADS * C_HEAD_DIM
    c_kv = C_KV_HEADS * C_HEAD_DIM
    for layer in range(depth):
        j = layer // 2
        if layer % 2 == 0:
            paired = 2 * a_width // PROJ_CHUNK
            proj = _norm_proj(h, norm_mix[layer], w_in_ab, j, cos, sin, paired=paired,
                              n_main=w_in_ab.shape[2], scaled_cols=a_width, rope_cols=2 * a_width,
                              scale=ROPE_DIM ** -0.5 * math.log2(math.e))
            at = functools.partial(_group_blocks, paired=paired)
            lam_init = 0.8 - 0.6 * math.exp(-0.3 * layer)
            lam_params = jnp.stack([lambda_q1[j], lambda_k1[j], lambda_q2[j], lambda_k2[j]])
            oa = _diff_attn(proj, lam_params, diff_subln[j], lam_init,
                            q_at=at(0), k_at=at(2), v_at=at(4))
            ob = _stickbreak(proj, q_at=at(6), k_at=at(8), v_at=at(10))
            h = _out_proj([oa, ob], w_out_ab, j, h)
        else:
            proj = _norm_proj(h, norm_mix[layer], w_in_c, j, cos, sin, tail_rope_cols=c_kv,
                              n_main=c_q, scaled_cols=c_q, rope_cols=c_q,
                              scale=C_HEAD_DIM ** -0.5 * math.log2(math.e))
            oc = _swa(proj, sinks[j])
            h = _out_proj([oc], w_out_c, j, h)
        h = _mlp(h, norm_mlp[layer], w_up, w_down, layer,
                 norm_final, final_norm=(layer == depth - 1))
    return h.reshape(b, s, d)
```

```python
import functools
import math

import jax
import jax.numpy as jnp
import numpy as np
from jax import lax
from jax.experimental import pallas as pl
from jax.experimental.pallas import tpu as pltpu

F32 = jnp.float32
BF16 = jnp.bfloat16

EPS = 1e-6
ROPE_THETA = 10000.0
LANES = 128
ROPE_DIM = 64
A_HEADS = 8
B_HEADS = 8
B_HEAD_DIM = 128
C_Q_HEADS = 32
C_KV_HEADS = 4
C_GROUP = C_Q_HEADS // C_KV_HEADS
C_HEAD_DIM = 64
WINDOW = 128
MASK_VALUE = -1e30
EXP2_UNDERFLOW = -160.0
FINITE_BOUND = 1e30
MIB = 1024 * 1024

NT_DIMS = (((1,), (1,)), ((), ()))


VMEM_CAPACITY_MIB = 64
VMEM_RESERVED_MIB = 4


def _params(semantics, *buffers):
    need = sum(math.prod(shape) * jnp.dtype(dtype).itemsize * copies
               for shape, dtype, copies in buffers)
    limit = (VMEM_CAPACITY_MIB - VMEM_RESERVED_MIB) * MIB
    assert need <= limit, (need, limit)
    return pltpu.CompilerParams(dimension_semantics=semantics, vmem_limit_bytes=limit)


def _rms(x, g):
    ms = jnp.mean(x * x, axis=-1, keepdims=True)
    return x * lax.rsqrt(ms + EPS) * g


def _lane_tile(a, width):
    reps = width // LANES
    return a if reps == 1 else jnp.concatenate([a] * reps, axis=1)


def _head(ref, g, rows=slice(None)):
    return ref[rows, g * LANES:(g + 1) * LANES]


def _rope_table_kernel(pos_ref, invf_ref, cos_ref, sin_ref):
    ang = pos_ref[...].astype(F32) * invf_ref[...]
    lane = lax.broadcasted_iota(jnp.int32, ang.shape, 1)
    sign = jnp.where((lane & (ROPE_DIM // 2)) == 0, -1.0, 1.0)
    cos_ref[...] = jnp.cos(ang)
    sin_ref[...] = jnp.sin(ang) * sign


def _rope_tables(positions):
    s = positions.shape[-1]
    tm = 1024
    inv_freq = ROPE_THETA ** (-jnp.arange(0, ROPE_DIM, 2, dtype=F32) / ROPE_DIM)
    invf = jnp.tile(inv_freq, LANES // (ROPE_DIM // 2)).reshape(1, LANES)
    pos = positions.reshape(s, 1)
    return pl.pallas_call(
        _rope_table_kernel,
        grid=(s // tm,),
        in_specs=[pl.BlockSpec((tm, 1), lambda i: (i, 0)),
                  pl.BlockSpec((1, LANES), lambda i: (0, 0))],
        out_specs=[pl.BlockSpec((tm, LANES), lambda i: (i, 0))] * 2,
        out_shape=[jax.ShapeDtypeStruct((s, LANES), F32)] * 2,
        compiler_params=_params(("arbitrary",), ((tm, LANES), F32, 6)),
        name="rope_tables",
    )(pos, invf)


def _rope(x, cos, ss):
    half = ROPE_DIM // 2
    lane = lax.broadcasted_iota(jnp.int32, cos.shape, 1)
    low = (lane & half) == 0
    outs = []
    for c in range(x.shape[1] // LANES):
        xc = x[:, c * LANES:(c + 1) * LANES]
        partner = jnp.where(low, pltpu.roll(xc, LANES - half, 1), pltpu.roll(xc, half, 1))
        outs.append(xc * cos + partner * ss)
    return outs[0] if len(outs) == 1 else jnp.concatenate(outs, axis=1)


PROJ_CHUNK = 512


def _norm_proj_kernel(*refs, tile_runs, scale, tail_rope_cols):
    x_ref, g_ref = refs[:2]
    w_refs = refs[2:4]
    tail_ref, dup_ref = refs[4:6] if tail_rope_cols is not None else (None, None)
    cos_ref, sin_ref, o_ref, xn_ref = refs[-4:]
    j = pl.program_id(1)

    def chunk(xn, c, kind):
        cols = slice(c * PROJ_CHUNK, (c + 1) * PROJ_CHUNK)
        acc = jnp.dot(xn, w_refs[c][...].astype(BF16), preferred_element_type=F32)
        if kind != "plain":
            acc = _rope(acc, cos_ref[...], sin_ref[...])
        if kind == "scaled":
            acc = acc * scale
        o_ref[:, cols] = acc.astype(o_ref.dtype)

    @pl.when(j == 0)
    def _():
        xn = _rms(x_ref[...], g_ref[...]).astype(BF16)
        xn_ref[...] = xn
        for c, kind in enumerate(tile_runs[0][2]):
            chunk(xn, c, kind)

    for first, end, kinds in tile_runs:
        first = max(first, 1)
        if first < end:
            @pl.when((j >= first) & (j < end))
            def _(kinds=kinds):
                for c, kind in enumerate(kinds):
                    chunk(xn_ref[...], c, kind)

    if tail_rope_cols is not None:
        @pl.when(j == tile_runs[-1][1])
        def _():
            acc = jnp.dot(xn_ref[...], tail_ref[...].astype(BF16), preferred_element_type=F32)
            roped = _rope(acc[:, :tail_rope_cols], cos_ref[...], sin_ref[...])
            narrow = jnp.concatenate([roped, acc[:, tail_rope_cols:]], axis=1).astype(BF16)
            o_ref[...] = jnp.dot(narrow, dup_ref[...],
                                 preferred_element_type=F32).astype(o_ref.dtype)


def _step_chunks(step, paired):
    if isinstance(step, int):
        return (step, paired + step) if step < paired else (2 * step, 2 * step + 1)
    first = step < paired
    return jnp.where(first, step, 2 * step), jnp.where(first, paired + step, 2 * step + 1)


def _proj_block(chunk, paired):
    if chunk < paired:
        return 2 * chunk
    if chunk < 2 * paired:
        return 2 * (chunk - paired) + 1
    return chunk


def _group_blocks(first_chunk, paired):
    base = _proj_block(first_chunk, paired)
    return base, _proj_block(first_chunk + 1, paired) - base


def _tile_runs(n_steps, paired, scaled_cols, rope_cols):
    def kind(chunk):
        col = chunk * PROJ_CHUNK
        return "scaled" if col < scaled_cols else "rope" if col < rope_cols else "plain"
    patterns = [tuple(kind(int(c)) for c in _step_chunks(t, paired)) for t in range(n_steps)]
    runs, first = [], 0
    for t in range(1, n_steps + 1):
        if t == n_steps or patterns[t] != patterns[first]:
            runs.append((first, t, patterns[first]))
            first = t
    return tuple(runs)


def _head_dup_matrix(n_in, head_dim):
    out_col = np.arange(2 * n_in)
    src = (out_col // (2 * head_dim)) * head_dim + out_col % head_dim
    return jnp.asarray(np.arange(n_in)[:, None] == src[None, :], dtype=BF16)


def _norm_proj(x, g, w, layer, cos, sin, *, n_main, scaled_cols, rope_cols, scale, paired=0,
               tail_rope_cols=None, tm=1024):
    s, d = x.shape
    tn = 2 * PROJ_CHUNK
    n_chunks = n_main // PROJ_CHUNK
    main_tiles = n_chunks // 2
    has_tail = tail_rope_cols is not None
    n_tiles = main_tiles + has_tail
    kern = functools.partial(_norm_proj_kernel, scale=scale, tail_rope_cols=tail_rope_cols,
                             tile_runs=_tile_runs(main_tiles, paired, scaled_cols, rope_cols))

    def w_spec(which):
        return pl.BlockSpec((None, d, PROJ_CHUNK), lambda i, j: (
            layer, 0, jnp.minimum(_step_chunks(j, paired)[which], n_chunks - 2 + which)))

    w_specs = [w_spec(0), w_spec(1)]
    weights = [w, w]
    buffers = [((tm, d), F32, 2), ((d, tn), F32, 2), ((tm, LANES), F32, 4), ((tm, tn), BF16, 2),
               ((tm, d), BF16, 1)]
    if has_tail:
        half = tn // 2
        once = pl.Buffered(1)
        w_specs += [pl.BlockSpec((None, d, half), lambda i, j: (layer, 0, n_main // half),
                                 pipeline_mode=once),
                    pl.BlockSpec((half, tn), lambda i, j: (0, 0), pipeline_mode=once)]
        weights += [w, _head_dup_matrix(half, C_HEAD_DIM)]
        buffers += [((d, half), F32, 1), ((half, tn), BF16, 1)]
    return pl.pallas_call(
        kern,
        grid=(s // tm, n_tiles),
        in_specs=[pl.BlockSpec((tm, d), lambda i, j: (i, 0)),
                  pl.BlockSpec((1, d), lambda i, j: (0, 0))] + w_specs +
                 [pl.BlockSpec((tm, LANES), lambda i, j: (i, 0)),
                  pl.BlockSpec((tm, LANES), lambda i, j: (i, 0))],
        out_specs=pl.BlockSpec((tm, tn), lambda i, j: (i, j)),
        out_shape=jax.ShapeDtypeStruct((s, n_tiles * tn), BF16),
        scratch_shapes=[pltpu.VMEM((tm, d), BF16)],
        compiler_params=_params(("arbitrary", "arbitrary"), *buffers),
        name="norm_proj",
    )(x, g.reshape(1, d), *weights, cos, sin)


def _diff_attn_kernel(lam_ref, subln_ref, q_ref, k_ref, v_ref, o_ref,
                      qs_ref, m_ref, l_ref, acc_ref, *, tq, tk, heads, lam_init):
    i = pl.program_id(1)
    rows = 2 * tq
    half = LANES // 2
    hq = tq // 2
    sub = tq // tk

    for g in range(heads):
        q = _head(q_ref, g).astype(F32)
        lane = lax.broadcasted_iota(jnp.int32, q.shape, 1)
        q1 = jnp.where(lane < half, q, 0.0).astype(BF16)
        q2 = jnp.where(lane >= half, q, 0.0).astype(BF16)
        for part, src in enumerate((q1[0:hq], q2[0:hq], q1[hq:tq], q2[hq:tq])):
            qs_ref[g, part * hq:(part + 1) * hq, :] = src

    def reset():
        m_ref[...] = jnp.full(m_ref.shape, MASK_VALUE, F32)
        l_ref[...] = jnp.zeros(l_ref.shape, F32)
        acc_ref[...] = jnp.zeros(acc_ref.shape, F32)

    def step(start, width, diag_half, fresh=False):
        first = 0 if diag_half is None else diag_half * tq
        rs = slice(first, rows)
        for g in range(heads):
            kb = _head(k_ref, g, pl.ds(start, width))
            vb = _head(v_ref, g, pl.ds(start, width))
            s = lax.dot_general(qs_ref[g, rs], kb, NT_DIMS, preferred_element_type=F32)
            if diag_half is not None:
                r = lax.broadcasted_iota(jnp.int32, s.shape, 0) + first
                qrow = (r & (hq - 1)) + jnp.where(r >= tq, hq, 0)
                col = lax.broadcasted_iota(jnp.int32, s.shape, 1) + diag_half * hq
                s = jnp.where(col <= qrow, s, MASK_VALUE)
            if fresh:
                m_new = jnp.broadcast_to(jnp.max(s, axis=1, keepdims=True), (rows - first, LANES))
                p = jnp.exp2(s - _lane_tile(m_new, width))
                l_ref[g, rs] = jnp.broadcast_to(jnp.sum(p, axis=1, keepdims=True),
                                                (rows - first, LANES))
                acc_ref[g, rs] = jnp.dot(p.astype(BF16), vb, preferred_element_type=F32)
                m_ref[g, rs] = m_new
                continue
            m_prev = m_ref[g, rs]
            m_new = jnp.maximum(m_prev, jnp.max(s, axis=1, keepdims=True))
            alpha = jnp.exp2(m_prev - m_new)
            p = jnp.exp2(s - _lane_tile(m_new, width))
            l_ref[g, rs] = alpha * l_ref[g, rs] + jnp.sum(p, axis=1, keepdims=True)
            acc_ref[g, rs] = alpha * acc_ref[g, rs] + jnp.dot(p.astype(BF16), vb,
                                                              preferred_element_type=F32)
            m_ref[g, rs] = m_new

    def frozen_step(j):
        start = pl.multiple_of(j * tk, tk)
        for g in range(heads):
            kb = _head(k_ref, g, pl.ds(start, tk))
            vb = _head(v_ref, g, pl.ds(start, tk))
            s = lax.dot_general(qs_ref[g], kb, NT_DIMS, preferred_element_type=F32)
            p = jnp.exp2(s - _lane_tile(m_ref[g], tk))
            l_ref[g] += jnp.sum(p, axis=1, keepdims=True)
            acc_ref[g] += jnp.dot(p.astype(BF16), vb, preferred_element_type=F32)

    def diagonal(fresh):
        for u in range(2):
            step(pl.multiple_of(i * tq + u * hq, hq), hq, u, fresh=fresh and u == 0)

    def finish():
        lp = lam_ref[...]
        lam = (jnp.exp(jnp.sum(lp[0:1] * lp[1:2], axis=1, keepdims=True))
               - jnp.exp(jnp.sum(lp[2:3] * lp[3:4], axis=1, keepdims=True)) + lam_init)
        for g in range(heads):
            o = acc_ref[g] / l_ref[g]
            for u in range(2):
                d = o[2 * u * hq:(2 * u + 1) * hq] - lam * o[(2 * u + 1) * hq:(2 * u + 2) * hq]
                y = _rms(d, subln_ref[...]) * (1.0 - lam_init)
                o_ref[u * hq:(u + 1) * hq, g * LANES:(g + 1) * LANES] = y.astype(o_ref.dtype)

    n_full = i * sub
    diagonal(fresh=True)

    def pair(t, carry):
        frozen_step(2 * t)
        frozen_step(2 * t + 1)
        return carry

    lax.fori_loop(0, n_full // 2, pair, 0)

    @pl.when(n_full % 2 == 1)
    def _():
        frozen_step(n_full - 1)

    finite = jnp.minimum(jnp.min(jnp.where(l_ref[...] < FINITE_BOUND, 1.0, 0.0)),
                         jnp.min(jnp.where(jnp.abs(acc_ref[...]) < FINITE_BOUND, 1.0, 0.0)))

    @pl.when(finite > 0.5)
    def _():
        finish()

    @pl.when(finite <= 0.5)
    def _():
        reset()

        def body(j, carry):
            step(pl.multiple_of(j * tk, tk), tk, None)
            return carry

        lax.fori_loop(0, n_full, body, 0)
        diagonal(fresh=False)
        finish()


def _diff_attn(proj, lam_params, subln, lam_init, *, q_at, k_at, v_at, tq=512, tk=512, heads=4):
    s = proj.shape[0]
    width = A_HEADS * LANES
    hw = heads * LANES
    groups = A_HEADS // heads
    half_rows = tq // 2
    assert tq % tk == 0 and half_rows & (half_rows - 1) == 0, (tq, tk)
    assert hw == PROJ_CHUNK and groups == 2
    kern = functools.partial(_diff_attn_kernel, tq=tq, tk=tk, heads=heads, lam_init=lam_init)
    return pl.pallas_call(
        kern,
        grid=(groups, s // tq),
        in_specs=[pl.BlockSpec((4, ROPE_DIM), lambda h, i: (0, 0)),
                  pl.BlockSpec((1, LANES), lambda h, i: (0, 0)),
                  pl.BlockSpec((tq, hw), lambda h, i: (i, q_at[0] + q_at[1] * h)),
                  pl.BlockSpec((s, hw), lambda h, i: (0, k_at[0] + k_at[1] * h),
                               pipeline_mode=pl.Buffered(1)),
                  pl.BlockSpec((s, hw), lambda h, i: (0, v_at[0] + v_at[1] * h),
                               pipeline_mode=pl.Buffered(1))],
        out_specs=pl.BlockSpec((tq, hw), lambda h, i: (i, h)),
        out_shape=jax.ShapeDtypeStruct((s, width), BF16),
        scratch_shapes=[pltpu.VMEM((heads, 2 * tq, LANES), BF16)]
                       + [pltpu.VMEM((heads, 2 * tq, LANES), F32)] * 3,
        compiler_params=_params(("arbitrary", "arbitrary"),
                                ((tq, hw), BF16, 4), ((s, hw), BF16, 2),
                                ((heads, 2 * tq, LANES), BF16, 1), ((heads, 2 * tq, LANES), F32, 3)),
        name="diff_attn",
    )(lam_params, subln.reshape(1, LANES), proj, proj, proj)


def _stickbreak_kernel(q_ref, k_ref, v_ref, o_ref, tri_ref, carry_ref, acc_ref, *,
                       tq, tk, heads, scale):
    h = pl.program_id(0)
    i = pl.program_id(1)
    sub = tq // tk

    @pl.when((h == 0) & (i == 0))
    def _():
        r = lax.broadcasted_iota(jnp.int32, tri_ref.shape, 0)
        c = lax.broadcasted_iota(jnp.int32, tri_ref.shape, 1)
        tri_ref[...] = jnp.where(r > c, 1.0, 0.0).astype(BF16)

    carry_ref[...] = jnp.zeros(carry_ref.shape, F32)
    acc_ref[...] = jnp.zeros(acc_ref.shape, F32)

    def step(j, diag_block):
        start = pl.multiple_of(j * tk, tk)
        rows = slice(None) if diag_block is None else slice(diag_block * tk, tq)
        for g in range(heads):
            kb = _head(k_ref, g, pl.ds(start, tk))
            vb = _head(v_ref, g, pl.ds(start, tk))
            qk = lax.dot_general(_head(q_ref, g, rows), kb, NT_DIMS, preferred_element_type=F32)
            z = qk * scale
            nz = qk * (-scale)
            lb = jnp.minimum(nz, 0.0) - jnp.log2(1.0 + jnp.exp2(jnp.minimum(z, nz)))
            if diag_block is not None:
                row = lax.broadcasted_iota(jnp.int32, z.shape, 0)
                col = lax.broadcasted_iota(jnp.int32, z.shape, 1)
                strict = col < row
                lb = jnp.where(strict, lb, 0.0)
            later = jnp.dot(lb.astype(BF16), tri_ref[...], preferred_element_type=F32)
            e = z + lb + later + _lane_tile(carry_ref[g, rows], tk)
            if diag_block is not None:
                e = jnp.where(strict, e, MASK_VALUE)
            w = jnp.exp2(e)
            acc_ref[g, rows] += jnp.dot(w.astype(BF16), vb, preferred_element_type=F32)
            carry_ref[g, rows] += jnp.sum(lb, axis=1, keepdims=True)

    for u in reversed(range(sub)):
        step(i * sub + u, u)

    def more(state):
        j, top = state
        return (j >= 0) & (top > EXP2_UNDERFLOW)

    def body(state):
        j, _ = state
        step(j, None)
        return j - 1, jnp.max(carry_ref[...])

    lax.while_loop(more, body, (i * sub - 1, jnp.float32(0.0)))
    for g in range(heads):
        o_ref[:, g * LANES:(g + 1) * LANES] = acc_ref[g].astype(o_ref.dtype)


def _stickbreak(proj, *, q_at, k_at, v_at, tq=512, tk=256, heads=4):
    s = proj.shape[0]
    width = B_HEADS * B_HEAD_DIM
    hw = heads * LANES
    groups = B_HEADS // heads
    assert hw == PROJ_CHUNK and groups == 2
    kern = functools.partial(_stickbreak_kernel, tq=tq, tk=tk, heads=heads,
                             scale=B_HEAD_DIM ** -0.5 * math.log2(math.e))
    return pl.pallas_call(
        kern,
        grid=(groups, s // tq),
        in_specs=[pl.BlockSpec((tq, hw), lambda h, i: (i, q_at[0] + q_at[1] * h)),
                  pl.BlockSpec((s, hw), lambda h, i: (0, k_at[0] + k_at[1] * h)),
                  pl.BlockSpec((s, hw), lambda h, i: (0, v_at[0] + v_at[1] * h))],
        out_specs=pl.BlockSpec((tq, hw), lambda h, i: (i, h)),
        out_shape=jax.ShapeDtypeStruct((s, width), BF16),
        scratch_shapes=[pltpu.VMEM((tk, tk), BF16),
                        pltpu.VMEM((heads, tq, LANES), F32),
                        pltpu.VMEM((heads, tq, LANES), F32)],
        compiler_params=_params(("arbitrary", "arbitrary"),
                                ((tq, hw), BF16, 4), ((s, hw), BF16, 4),
                                ((tk, tk), BF16, 1), ((heads, tq, LANES), F32, 2)),
        name="stickbreak",
    )(proj, proj, proj)


def _swa_kernel(sinks_ref, q_ref, kp_ref, kc_ref, vp_ref, vc_ref, o_ref, *, blocks):
    hk = pl.program_id(0)
    n = pl.program_id(1)
    w = WINDOW
    half = LANES // 2

    def lane_halves(prev_ref, cur_ref):
        both = jnp.concatenate([prev_ref[...], cur_ref[...]], axis=0).astype(F32)
        low = lax.broadcasted_iota(jnp.int32, both.shape, 1) < half
        return jnp.where(low, both, 0.0).astype(BF16), jnp.where(low, 0.0, both).astype(BF16)

    k_lo, k_hi = lane_halves(kp_ref, kc_ref)
    v_lo, v_hi = lane_halves(vp_ref, vc_ref)

    qi = lax.broadcasted_iota(jnp.int32, (w, 2 * w), 0)
    ki = lax.broadcasted_iota(jnp.int32, (w, 2 * w), 1)
    band = (ki > qi) & (ki <= qi + w)
    first_real = jnp.where(n > 0, 0, w)
    band_first = band & (ki >= first_real)
    lane_o = lax.broadcasted_iota(jnp.int32, (w, LANES), 1)
    log2e = math.log2(math.e)

    for b in range(blocks):
        keys = slice(b * w, (b + 2) * w)
        kbd = jnp.concatenate([k_lo[keys], k_hi[keys]], axis=0)
        vbd = jnp.concatenate([v_lo[keys], v_hi[keys]], axis=0)
        mask = band_first if b == 0 else band
        pairs = C_GROUP // 2
        qs = jnp.concatenate([q_ref[b * w:(b + 1) * w, p * LANES:(p + 1) * LANES]
                              for p in range(pairs)], axis=0)
        s = lax.dot_general(qs, kbd, NT_DIMS, preferred_element_type=F32)
        probs, scales = [], []
        for p in range(pairs):
            row_p, inv_den = [], []
            for t in range(2):
                st = jnp.where(mask, s[p * w:(p + 1) * w, t * 2 * w:(t + 1) * 2 * w], MASK_VALUE)
                sink = sinks_ref[hk * C_GROUP + 2 * p + t] * log2e
                m = jnp.maximum(jnp.max(st, axis=1, keepdims=True), sink)
                pt = jnp.exp2(st - m)
                den = jnp.sum(pt, axis=1, keepdims=True) + jnp.exp2(sink - m)
                row_p.append(pt.astype(BF16))
                inv_den.append(1.0 / den)
            probs.append(jnp.concatenate(row_p, axis=1))
            scales.append(jnp.where(lane_o < half, inv_den[0], inv_den[1]))
        o = jnp.dot(jnp.concatenate(probs, axis=0), vbd, preferred_element_type=F32)
        outs = [o[p * w:(p + 1) * w] * scales[p] for p in range(pairs)]
        o_ref[b * w:(b + 1) * w, :] = jnp.concatenate(outs, axis=1).astype(o_ref.dtype)


def _swa(proj, sinks, *, blocks=4):
    s = proj.shape[0]
    w = WINDOW
    tq = blocks * w
    qw = C_GROUP * C_HEAD_DIM
    k0 = C_Q_HEADS * C_HEAD_DIM // LANES
    v0 = k0 + C_KV_HEADS
    prev = lambda n: jnp.maximum(n * blocks - 1, 0)
    return pl.pallas_call(
        functools.partial(_swa_kernel, blocks=blocks),
        grid=(C_KV_HEADS, s // tq),
        in_specs=[pl.BlockSpec(memory_space=pltpu.SMEM),
                  pl.BlockSpec((tq, qw), lambda h, n: (n, h)),
                  pl.BlockSpec((w, LANES), lambda h, n: (prev(n), k0 + h)),
                  pl.BlockSpec((tq, LANES), lambda h, n: (n, k0 + h)),
                  pl.BlockSpec((w, LANES), lambda h, n: (prev(n), v0 + h)),
                  pl.BlockSpec((tq, LANES), lambda h, n: (n, v0 + h))],
        out_specs=pl.BlockSpec((tq, qw), lambda h, n: (n, h)),
        out_shape=jax.ShapeDtypeStruct((s, C_Q_HEADS * C_HEAD_DIM), BF16),
        compiler_params=_params(("arbitrary", "arbitrary"),
                                ((tq, qw), BF16, 4), ((tq + w, LANES), BF16, 4)),
        name="swa_sink",
    )(sinks, proj, proj, proj, proj, proj)


def _out_proj_kernel(*refs, n_in):
    a_refs = refs[:n_in]
    w_ref, x_ref, o_ref, wb_ref = refs[n_in:]

    @pl.when(pl.program_id(0) == 0)
    def _():
        wb_ref[...] = w_ref[...].astype(BF16)

    acc = x_ref[...]
    row = 0
    for a_ref in a_refs:
        ka = a_ref.shape[1]
        acc = acc + jnp.dot(a_ref[...], wb_ref[row:row + ka, :], preferred_element_type=F32)
        row += ka
    o_ref[...] = acc


def _out_proj(acts, w, layer, x, *, tm=512):
    s, d = x.shape
    n_in = len(acts)
    kw = w.shape[1]
    a_specs = [pl.BlockSpec((tm, a.shape[1]), lambda i: (i, 0)) for a in acts]
    w_spec = pl.BlockSpec((None, kw, d), lambda i: (layer, 0, 0), pipeline_mode=pl.Buffered(1))
    return pl.pallas_call(
        functools.partial(_out_proj_kernel, n_in=n_in),
        grid=(s // tm,),
        in_specs=a_specs + [w_spec, pl.BlockSpec((tm, d), lambda i: (i, 0))],
        out_specs=pl.BlockSpec((tm, d), lambda i: (i, 0)),
        out_shape=jax.ShapeDtypeStruct((s, d), F32),
        scratch_shapes=[pltpu.VMEM((kw, d), BF16)],
        compiler_params=_params(("arbitrary",),
                                ((tm, kw), BF16, 2), ((kw, d), F32, 1), ((kw, d), BF16, 1),
                                ((tm, d), F32, 4)),
        name="out_proj",
    )(*acts, w, x)


def _mlp_kernel(x_ref, g_ref, wu_ref, wd_ref, gf_ref, o_ref, xn_ref, *, final_norm):
    f = pl.program_id(1)

    def up_down(xn):
        hid = jnp.dot(xn, wu_ref[...].astype(BF16), preferred_element_type=F32)
        hid = jnp.square(jnp.maximum(hid, 0.0)).astype(BF16)
        return jnp.dot(hid, wd_ref[...].astype(BF16), preferred_element_type=F32)

    @pl.when(f == 0)
    def _():
        x = x_ref[...]
        xn = _rms(x, g_ref[...]).astype(BF16)
        xn_ref[...] = xn
        o_ref[...] = x + up_down(xn)

    @pl.when(f > 0)
    def _():
        o_ref[...] += up_down(xn_ref[...])

    if final_norm:
        @pl.when(f == pl.num_programs(1) - 1)
        def _():
            o_ref[...] = _rms(o_ref[...], gf_ref[...])


def _mlp(x, g, w_up, w_down, layer, g_final, *, final_norm, tm=1024, tf=512):
    s, d = x.shape
    dff = w_up.shape[2]
    return pl.pallas_call(
        functools.partial(_mlp_kernel, final_norm=final_norm),
        grid=(s // tm, dff // tf),
        in_specs=[pl.BlockSpec((tm, d), lambda i, f: (i, 0)),
                  pl.BlockSpec((1, d), lambda i, f: (0, 0)),
                  pl.BlockSpec((None, d, tf), lambda i, f: (layer, 0, f)),
                  pl.BlockSpec((None, tf, d), lambda i, f: (layer, f, 0)),
                  pl.BlockSpec((1, d), lambda i, f: (0, 0))],
        out_specs=pl.BlockSpec((tm, d), lambda i, f: (i, 0)),
        out_shape=jax.ShapeDtypeStruct((s, d), F32),
        scratch_shapes=[pltpu.VMEM((tm, d), BF16)],
        compiler_params=_params(("arbitrary", "arbitrary"),
                                ((tm, d), F32, 4), ((d, tf), F32, 4), ((tm, d), BF16, 1)),
        name="mlp",
    )(x, g.reshape(1, d), w_up, w_down, g_final.reshape(1, d))


def kernel(x, positions, norm_mix, norm_mlp, norm_final, w_in_ab, w_out_ab, lambda_q1, lambda_k1,
           lambda_q2, lambda_k2, diff_subln, w_in_c, w_out_c, sinks, w_up, w_down):
    b, s, d = x.shape
    assert b == 1
    depth = norm_mix.shape[0]
    h = x.reshape(s, d)
    cos, sin = _rope_tables(positions)

    a_width = A_HEADS * LANES
    c_q = C_Q_HEADS * C_HEAD_DIM
    c_kv = C_KV_HEADS * C_HEAD_DIM
    for layer in range(depth):
        j = layer // 2
        if layer % 2 == 0:
            paired = 2 * a_width // PROJ_CHUNK
            proj = _norm_proj(h, norm_mix[layer], w_in_ab, j, cos, sin, paired=paired,
                              n_main=w_in_ab.shape[2], scaled_cols=a_width, rope_cols=2 * a_width,
                              scale=ROPE_DIM ** -0.5 * math.log2(math.e))
            at = functools.partial(_group_blocks, paired=paired)
            lam_init = 0.8 - 0.6 * math.exp(-0.3 * layer)
            lam_params = jnp.stack([lambda_q1[j], lambda_k1[j], lambda_q2[j], lambda_k2[j]])
            oa = _diff_attn(proj, lam_params, diff_subln[j], lam_init,
                            q_at=at(0), k_at=at(2), v_at=at(4))
            ob = _stickbreak(proj, q_at=at(6), k_at=at(8), v_at=at(10))
            h = _out_proj([oa, ob], w_out_ab, j, h)
        else:
            proj = _norm_proj(h, norm_mix[layer], w_in_c, j, cos, sin, tail_rope_cols=c_kv,
                              n_main=c_q, scaled_cols=c_q, rope_cols=c_q,
                              scale=C_HEAD_DIM ** -0.5 * math.log2(math.e))
            oc = _swa(proj, sinks[j])
            h = _out_proj([oc], w_out_c, j, h)
        h = _mlp(h, norm_mlp[layer], w_up, w_down, layer,
                 norm_final, final_norm=(layer == depth - 1))
    return h.reshape(b, s, d)
```

```python
import functools
import math

import jax
import jax.numpy as jnp
import numpy as np
from jax import lax
from jax.experimental import pallas as pl
from jax.experimental.pallas import tpu as pltpu

F32 = jnp.float32
BF16 = jnp.bfloat16

EPS = 1e-6
ROPE_THETA = 10000.0
LANES = 128
ROPE_DIM = 64
A_HEADS = 8
B_HEADS = 8
B_HEAD_DIM = 128
C_Q_HEADS = 32
C_KV_HEADS = 4
C_GROUP = C_Q_HEADS // C_KV_HEADS
C_HEAD_DIM = 64
WINDOW = 128
MASK_VALUE = -1e30
EXP2_UNDERFLOW = -160.0
FINITE_BOUND = 1e30
MIB = 1024 * 1024

NT_DIMS = (((1,), (1,)), ((), ()))


VMEM_CAPACITY_MIB = 64
VMEM_RESERVED_MIB = 4


def _params(semantics, *buffers):
    need = sum(math.prod(shape) * jnp.dtype(dtype).itemsize * copies
               for shape, dtype, copies in buffers)
    limit = (VMEM_CAPACITY_MIB - VMEM_RESERVED_MIB) * MIB
    assert need <= limit, (need, limit)
    return pltpu.CompilerParams(dimension_semantics=semantics, vmem_limit_bytes=limit)


def _rms(x, g):
    ms = jnp.mean(x * x, axis=-1, keepdims=True)
    return x * lax.rsqrt(ms + EPS) * g


def _lane_tile(a, width):
    reps = width // LANES
    return a if reps == 1 else jnp.concatenate([a] * reps, axis=1)


def _head(ref, g, rows=slice(None)):
    return ref[rows, g * LANES:(g + 1) * LANES]


def _rope_table_kernel(pos_ref, invf_ref, cos_ref, sin_ref):
    ang = pos_ref[...].astype(F32) * invf_ref[...]
    lane = lax.broadcasted_iota(jnp.int32, ang.shape, 1)
    sign = jnp.where((lane & (ROPE_DIM // 2)) == 0, -1.0, 1.0)
    cos_ref[...] = jnp.cos(ang)
    sin_ref[...] = jnp.sin(ang) * sign


def _rope_tables(positions):
    s = positions.shape[-1]
    tm = 1024
    inv_freq = ROPE_THETA ** (-jnp.arange(0, ROPE_DIM, 2, dtype=F32) / ROPE_DIM)
    invf = jnp.tile(inv_freq, LANES // (ROPE_DIM // 2)).reshape(1, LANES)
    pos = positions.reshape(s, 1)
    return pl.pallas_call(
        _rope_table_kernel,
        grid=(s // tm,),
        in_specs=[pl.BlockSpec((tm, 1), lambda i: (i, 0)),
                  pl.BlockSpec((1, LANES), lambda i: (0, 0))],
        out_specs=[pl.BlockSpec((tm, LANES), lambda i: (i, 0))] * 2,
        out_shape=[jax.ShapeDtypeStruct((s, LANES), F32)] * 2,
        compiler_params=_params(("arbitrary",), ((tm, LANES), F32, 6)),
        name="rope_tables",
    )(pos, invf)


def _rope(x, cos, ss):
    half = ROPE_DIM // 2
    lane = lax.broadcasted_iota(jnp.int32, cos.shape, 1)
    low = (lane & half) == 0
    outs = []
    for c in range(x.shape[1] // LANES):
        xc = x[:, c * LANES:(c + 1) * LANES]
        partner = jnp.where(low, pltpu.roll(xc, LANES - half, 1), pltpu.roll(xc, half, 1))
        outs.append(xc * cos + partner * ss)
    return outs[0] if len(outs) == 1 else jnp.concatenate(outs, axis=1)


PROJ_CHUNK = 512


def _norm_proj_kernel(*refs, tile_runs, scale, tail_rope_cols):
    x_ref, g_ref = refs[:2]
    w_refs = refs[2:4]
    tail_ref, dup_ref = refs[4:6] if tail_rope_cols is not None else (None, None)
    cos_ref, sin_ref, o_ref, xn_ref = refs[-4:]
    j = pl.program_id(1)

    def chunk(xn, c, kind):
        cols = slice(c * PROJ_CHUNK, (c + 1) * PROJ_CHUNK)
        acc = jnp.dot(xn, w_refs[c][...].astype(BF16), preferred_element_type=F32)
        if kind != "plain":
            acc = _rope(acc, cos_ref[...], sin_ref[...])
        if kind == "scaled":
            acc = acc * scale
        o_ref[:, cols] = acc.astype(o_ref.dtype)

    @pl.when(j == 0)
    def _():
        xn = _rms(x_ref[...], g_ref[...]).astype(BF16)
        xn_ref[...] = xn
        for c, kind in enumerate(tile_runs[0][2]):
            chunk(xn, c, kind)

    for first, end, kinds in tile_runs:
        first = max(first, 1)
        if first < end:
            @pl.when((j >= first) & (j < end))
            def _(kinds=kinds):
                for c, kind in enumerate(kinds):
                    chunk(xn_ref[...], c, kind)

    if tail_rope_cols is not None:
        @pl.when(j == tile_runs[-1][1])
        def _():
            acc = jnp.dot(xn_ref[...], tail_ref[...].astype(BF16), preferred_element_type=F32)
            roped = _rope(acc[:, :tail_rope_cols], cos_ref[...], sin_ref[...])
            narrow = jnp.concatenate([roped, acc[:, tail_rope_cols:]], axis=1).astype(BF16)
            o_ref[...] = jnp.dot(narrow, dup_ref[...],
                                 preferred_element_type=F32).astype(o_ref.dtype)


def _step_chunks(step, paired):
    if isinstance(step, int):
        return (step, paired + step) if step < paired else (2 * step, 2 * step + 1)
    first = step < paired
    return jnp.where(first, step, 2 * step), jnp.where(first, paired + step, 2 * step + 1)


def _proj_block(chunk, paired):
    if chunk < paired:
        return 2 * chunk
    if chunk < 2 * paired:
        return 2 * (chunk - paired) + 1
    return chunk


def _group_blocks(first_chunk, paired):
    base = _proj_block(first_chunk, paired)
    return base, _proj_block(first_chunk + 1, paired) - base


def _tile_runs(n_steps, paired, scaled_cols, rope_cols):
    def kind(chunk):
        col = chunk * PROJ_CHUNK
        return "scaled" if col < scaled_cols else "rope" if col < rope_cols else "plain"
    patterns = [tuple(kind(int(c)) for c in _step_chunks(t, paired)) for t in range(n_steps)]
    runs, first = [], 0
    for t in range(1, n_steps + 1):
        if t == n_steps or patterns[t] != patterns[first]:
            runs.append((first, t, patterns[first]))
            first = t
    return tuple(runs)


def _head_dup_matrix(n_in, head_dim):
    out_col = np.arange(2 * n_in)
    src = (out_col // (2 * head_dim)) * head_dim + out_col % head_dim
    return jnp.asarray(np.arange(n_in)[:, None] == src[None, :], dtype=BF16)


def _norm_proj(x, g, w, layer, cos, sin, *, n_main, scaled_cols, rope_cols, scale, paired=0,
               tail_rope_cols=None, tm=1024):
    s, d = x.shape
    tn = 2 * PROJ_CHUNK
    n_chunks = n_main // PROJ_CHUNK
    main_tiles = n_chunks // 2
    has_tail = tail_rope_cols is not None
    n_tiles = main_tiles + has_tail
    kern = functools.partial(_norm_proj_kernel, scale=scale, tail_rope_cols=tail_rope_cols,
                             tile_runs=_tile_runs(main_tiles, paired, scaled_cols, rope_cols))

    def w_spec(which):
        return pl.BlockSpec((None, d, PROJ_CHUNK), lambda i, j: (
            layer, 0, jnp.minimum(_step_chunks(j, paired)[which], n_chunks - 2 + which)))

    w_specs = [w_spec(0), w_spec(1)]
    weights = [w, w]
    buffers = [((tm, d), F32, 2), ((d, tn), F32, 2), ((tm, LANES), F32, 4), ((tm, tn), BF16, 2),
               ((tm, d), BF16, 1)]
    if has_tail:
        half = tn // 2
        once = pl.Buffered(1)
        w_specs += [pl.BlockSpec((None, d, half), lambda i, j: (layer, 0, n_main // half),
                                 pipeline_mode=once),
                    pl.BlockSpec((half, tn), lambda i, j: (0, 0), pipeline_mode=once)]
        weights += [w, _head_dup_matrix(half, C_HEAD_DIM)]
        buffers += [((d, half), F32, 1), ((half, tn), BF16, 1)]
    return pl.pallas_call(
        kern,
        grid=(s // tm, n_tiles),
        in_specs=[pl.BlockSpec((tm, d), lambda i, j: (i, 0)),
                  pl.BlockSpec((1, d), lambda i, j: (0, 0))] + w_specs +
                 [pl.BlockSpec((tm, LANES), lambda i, j: (i, 0)),
                  pl.BlockSpec((tm, LANES), lambda i, j: (i, 0))],
        out_specs=pl.BlockSpec((tm, tn), lambda i, j: (i, j)),
        out_shape=jax.ShapeDtypeStruct((s, n_tiles * tn), BF16),
        scratch_shapes=[pltpu.VMEM((tm, d), BF16)],
        compiler_params=_params(("arbitrary", "arbitrary"), *buffers),
        name="norm_proj",
    )(x, g.reshape(1, d), *weights, cos, sin)


def _diff_attn_kernel(lam_ref, subln_ref, q_ref, k_ref, v_ref, o_ref,
                      qs_ref, m_ref, l_ref, acc_ref, *, tq, tk, heads, lam_init):
    i = pl.program_id(1)
    rows = 2 * tq
    half = LANES // 2
    hq = tq // 2
    sub = tq // tk

    for g in range(heads):
        q = _head(q_ref, g).astype(F32)
        lane = lax.broadcasted_iota(jnp.int32, q.shape, 1)
        q1 = jnp.where(lane < half, q, 0.0).astype(BF16)
        q2 = jnp.where(lane >= half, q, 0.0).astype(BF16)
        for part, src in enumerate((q1[0:hq], q2[0:hq], q1[hq:tq], q2[hq:tq])):
            qs_ref[g, part * hq:(part + 1) * hq, :] = src

    def reset():
        m_ref[...] = jnp.full(m_ref.shape, MASK_VALUE, F32)
        l_ref[...] = jnp.zeros(l_ref.shape, F32)
        acc_ref[...] = jnp.zeros(acc_ref.shape, F32)

    def step(start, width, diag_half, fresh=False):
        first = 0 if diag_half is None else diag_half * tq
        rs = slice(first, rows)
        for g in range(heads):
            kb = _head(k_ref, g, pl.ds(start, width))
            vb = _head(v_ref, g, pl.ds(start, width))
            s = lax.dot_general(qs_ref[g, rs], kb, NT_DIMS, preferred_element_type=F32)
            if diag_half is not None:
                r = lax.broadcasted_iota(jnp.int32, s.shape, 0) + first
                qrow = (r & (hq - 1)) + jnp.where(r >= tq, hq, 0)
                col = lax.broadcasted_iota(jnp.int32, s.shape, 1) + diag_half * hq
                s = jnp.where(col <= qrow, s, MASK_VALUE)
            if fresh:
                m_new = jnp.broadcast_to(jnp.max(s, axis=1, keepdims=True), (rows - first, LANES))
                p = jnp.exp2(s - _lane_tile(m_new, width))
                l_ref[g, rs] = jnp.broadcast_to(jnp.sum(p, axis=1, keepdims=True),
                                                (rows - first, LANES))
                acc_ref[g, rs] = jnp.dot(p.astype(BF16), vb, preferred_element_type=F32)
                m_ref[g, rs] = m_new
                continue
            m_prev = m_ref[g, rs]
            m_new = jnp.maximum(m_prev, jnp.max(s, axis=1, keepdims=True))
            alpha = jnp.exp2(m_prev - m_new)
            p = jnp.exp2(s - _lane_tile(m_new, width))
            l_ref[g, rs] = alpha * l_ref[g, rs] + jnp.sum(p, axis=1, keepdims=True)
            acc_ref[g, rs] = alpha * acc_ref[g, rs] + jnp.dot(p.astype(BF16), vb,
                                                              preferred_element_type=F32)
            m_ref[g, rs] = m_new

    def frozen_step(j):
        start = pl.multiple_of(j * tk, tk)
        for g in range(heads):
            kb = _head(k_ref, g, pl.ds(start, tk))
            vb = _head(v_ref, g, pl.ds(start, tk))
            s = lax.dot_general(qs_ref[g], kb, NT_DIMS, preferred_element_type=F32)
            p = jnp.exp2(s - _lane_tile(m_ref[g], tk))
            l_ref[g] += jnp.sum(p, axis=1, keepdims=True)
            acc_ref[g] += jnp.dot(p.astype(BF16), vb, preferred_element_type=F32)

    def diagonal(fresh):
        for u in range(2):
            step(pl.multiple_of(i * tq + u * hq, hq), hq, u, fresh=fresh and u == 0)

    def finish():
        lp = lam_ref[...]
        lam = (jnp.exp(jnp.sum(lp[0:1] * lp[1:2], axis=1, keepdims=True))
               - jnp.exp(jnp.sum(lp[2:3] * lp[3:4], axis=1, keepdims=True)) + lam_init)
        for g in range(heads):
            o = acc_ref[g] / l_ref[g]
            for u in range(2):
                d = o[2 * u * hq:(2 * u + 1) * hq] - lam * o[(2 * u + 1) * hq:(2 * u + 2) * hq]
                y = _rms(d, subln_ref[...]) * (1.0 - lam_init)
                o_ref[u * hq:(u + 1) * hq, g * LANES:(g + 1) * LANES] = y.astype(o_ref.dtype)

    n_full = i * sub
    diagonal(fresh=True)

    def pair(t, carry):
        frozen_step(2 * t)
        frozen_step(2 * t + 1)
        return carry

    lax.fori_loop(0, n_full // 2, pair, 0)

    @pl.when(n_full % 2 == 1)
    def _():
        frozen_step(n_full - 1)

    finite = jnp.minimum(jnp.min(jnp.where(l_ref[...] < FINITE_BOUND, 1.0, 0.0)),
                         jnp.min(jnp.where(jnp.abs(acc_ref[...]) < FINITE_BOUND, 1.0, 0.0)))

    @pl.when(finite > 0.5)
    def _():
        finish()

    @pl.when(finite <= 0.5)
    def _():
        reset()

        def body(j, carry):
            step(pl.multiple_of(j * tk, tk), tk, None)
            return carry

        lax.fori_loop(0, n_full, body, 0)
        diagonal(fresh=False)
        finish()


def _diff_attn(proj, lam_params, subln, lam_init, *, q_at, k_at, v_at, tq=512, tk=512, heads=4):
    s = proj.shape[0]
    width = A_HEADS * LANES
    hw = heads * LANES
    groups = A_HEADS // heads
    half_rows = tq // 2
    assert tq % tk == 0 and half_rows & (half_rows - 1) == 0, (tq, tk)
    assert hw == PROJ_CHUNK and groups == 2
    kern = functools.partial(_diff_attn_kernel, tq=tq, tk=tk, heads=heads, lam_init=lam_init)
    return pl.pallas_call(
        kern,
        grid=(groups, s // tq),
        in_specs=[pl.BlockSpec((4, ROPE_DIM), lambda h, i: (0, 0)),
                  pl.BlockSpec((1, LANES), lambda h, i: (0, 0)),
                  pl.BlockSpec((tq, hw), lambda h, i: (i, q_at[0] + q_at[1] * h)),
                  pl.BlockSpec((s, hw), lambda h, i: (0, k_at[0] + k_at[1] * h)),
                  pl.BlockSpec((s, hw), lambda h, i: (0, v_at[0] + v_at[1] * h))],
        out_specs=pl.BlockSpec((tq, hw), lambda h, i: (i, h)),
        out_shape=jax.ShapeDtypeStruct((s, width), BF16),
        scratch_shapes=[pltpu.VMEM((heads, 2 * tq, LANES), BF16)]
                       + [pltpu.VMEM((heads, 2 * tq, LANES), F32)] * 3,
        compiler_params=_params(("arbitrary", "arbitrary"),
                                ((tq, hw), BF16, 4), ((s, hw), BF16, 4),
                                ((heads, 2 * tq, LANES), BF16, 1), ((heads, 2 * tq, LANES), F32, 3)),
        name="diff_attn",
    )(lam_params, subln.reshape(1, LANES), proj, proj, proj)


def _stickbreak_kernel(q_ref, k_ref, v_ref, o_ref, tri_ref, carry_ref, acc_ref, *,
                       tq, tk, heads, scale):
    h = pl.program_id(0)
    i = pl.program_id(1)
    sub = tq // tk

    @pl.when((h == 0) & (i == 0))
    def _():
        r = lax.broadcasted_iota(jnp.int32, tri_ref.shape, 0)
        c = lax.broadcasted_iota(jnp.int32, tri_ref.shape, 1)
        tri_ref[...] = jnp.where(r > c, 1.0, 0.0).astype(BF16)

    carry_ref[...] = jnp.zeros(carry_ref.shape, F32)
    acc_ref[...] = jnp.zeros(acc_ref.shape, F32)

    def step(j, diag_block):
        start = pl.multiple_of(j * tk, tk)
        rows = slice(None) if diag_block is None else slice(diag_block * tk, tq)
        for g in range(heads):
            kb = _head(k_ref, g, pl.ds(start, tk))
            vb = _head(v_ref, g, pl.ds(start, tk))
            qk = lax.dot_general(_head(q_ref, g, rows), kb, NT_DIMS, preferred_element_type=F32)
            z = qk * scale
            nz = qk * (-scale)
            lb = jnp.minimum(nz, 0.0) - jnp.log2(1.0 + jnp.exp2(jnp.minimum(z, nz)))
            if diag_block is not None:
                row = lax.broadcasted_iota(jnp.int32, z.shape, 0)
                col = lax.broadcasted_iota(jnp.int32, z.shape, 1)
                strict = col < row
                lb = jnp.where(strict, lb, 0.0)
            later = jnp.dot(lb.astype(BF16), tri_ref[...], preferred_element_type=F32)
            e = z + lb + later + _lane_tile(carry_ref[g, rows], tk)
            if diag_block is not None:
                e = jnp.where(strict, e, MASK_VALUE)
            w = jnp.exp2(e)
            acc_ref[g, rows] += jnp.dot(w.astype(BF16), vb, preferred_element_type=F32)
            carry_ref[g, rows] += jnp.sum(lb, axis=1, keepdims=True)

    for u in reversed(range(sub)):
        step(i * sub + u, u)

    def more(state):
        j, top = state
        return (j >= 0) & (top > EXP2_UNDERFLOW)

    def body(state):
        j, _ = state
        step(j, None)
        return j - 1, jnp.max(carry_ref[...])

    lax.while_loop(more, body, (i * sub - 1, jnp.float32(0.0)))
    for g in range(heads):
        o_ref[:, g * LANES:(g + 1) * LANES] = acc_ref[g].astype(o_ref.dtype)


def _stickbreak(proj, *, q_at, k_at, v_at, tq=512, tk=256, heads=4):
    s = proj.shape[0]
    width = B_HEADS * B_HEAD_DIM
    hw = heads * LANES
    groups = B_HEADS // heads
    assert hw == PROJ_CHUNK and groups == 2
    kern = functools.partial(_stickbreak_kernel, tq=tq, tk=tk, heads=heads,
                             scale=B_HEAD_DIM ** -0.5 * math.log2(math.e))
    return pl.pallas_call(
        kern,
        grid=(groups, s // tq),
        in_specs=[pl.BlockSpec((tq, hw), lambda h, i: (i, q_at[0] + q_at[1] * h)),
                  pl.BlockSpec((s, hw), lambda h, i: (0, k_at[0] + k_at[1] * h)),
                  pl.BlockSpec((s, hw), lambda h, i: (0, v_at[0] + v_at[1] * h))],
        out_specs=pl.BlockSpec((tq, hw), lambda h, i: (i, h)),
        out_shape=jax.ShapeDtypeStruct((s, width), BF16),
        scratch_shapes=[pltpu.VMEM((tk, tk), BF16),
                        pltpu.VMEM((heads, tq, LANES), F32),
                        pltpu.VMEM((heads, tq, LANES), F32)],
        compiler_params=_params(("arbitrary", "arbitrary"),
                                ((tq, hw), BF16, 4), ((s, hw), BF16, 4),
                                ((tk, tk), BF16, 1), ((heads, tq, LANES), F32, 2)),
        name="stickbreak",
    )(proj, proj, proj)


def _swa_kernel(sinks_ref, q_ref, kp_ref, kc_ref, vp_ref, vc_ref, o_ref, *, blocks):
    hk = pl.program_id(0)
    n = pl.program_id(1)
    w = WINDOW
    half = LANES // 2

    def lane_halves(prev_ref, cur_ref):
        both = jnp.concatenate([prev_ref[...], cur_ref[...]], axis=0).astype(F32)
        low = lax.broadcasted_iota(jnp.int32, both.shape, 1) < half
        return jnp.where(low, both, 0.0).astype(BF16), jnp.where(low, 0.0, both).astype(BF16)

    k_lo, k_hi = lane_halves(kp_ref, kc_ref)
    v_lo, v_hi = lane_halves(vp_ref, vc_ref)

    qi = lax.broadcasted_iota(jnp.int32, (w, 2 * w), 0)
    ki = lax.broadcasted_iota(jnp.int32, (w, 2 * w), 1)
    band = (ki > qi) & (ki <= qi + w)
    first_real = jnp.where(n > 0, 0, w)
    band_first = band & (ki >= first_real)
    lane_o = lax.broadcasted_iota(jnp.int32, (w, LANES), 1)
    log2e = math.log2(math.e)

    for b in range(blocks):
        keys = slice(b * w, (b + 2) * w)
        kbd = jnp.concatenate([k_lo[keys], k_hi[keys]], axis=0)
        vbd = jnp.concatenate([v_lo[keys], v_hi[keys]], axis=0)
        mask = band_first if b == 0 else band
        pairs = C_GROUP // 2
        qs = jnp.concatenate([q_ref[b * w:(b + 1) * w, p * LANES:(p + 1) * LANES]
                              for p in range(pairs)], axis=0)
        s = lax.dot_general(qs, kbd, NT_DIMS, preferred_element_type=F32)
        probs, scales = [], []
        for p in range(pairs):
            row_p, inv_den = [], []
            for t in range(2):
                st = jnp.where(mask, s[p * w:(p + 1) * w, t * 2 * w:(t + 1) * 2 * w], MASK_VALUE)
                sink = sinks_ref[hk * C_GROUP + 2 * p + t] * log2e
                m = jnp.maximum(jnp.max(st, axis=1, keepdims=True), sink)
                pt = jnp.exp2(st - m)
                den = jnp.sum(pt, axis=1, keepdims=True) + jnp.exp2(sink - m)
                row_p.append(pt.astype(BF16))
                inv_den.append(1.0 / den)
            probs.append(jnp.concatenate(row_p, axis=1))
            scales.append(jnp.where(lane_o < half, inv_den[0], inv_den[1]))
        o = jnp.dot(jnp.concatenate(probs, axis=0), vbd, preferred_element_type=F32)
        outs = [o[p * w:(p + 1) * w] * scales[p] for p in range(pairs)]
        o_ref[b * w:(b + 1) * w, :] = jnp.concatenate(outs, axis=1).astype(o_ref.dtype)


def _swa(proj, sinks, *, blocks=4):
    s = proj.shape[0]
    w = WINDOW
    tq = blocks * w
    qw = C_GROUP * C_HEAD_DIM
    k0 = C_Q_HEADS * C_HEAD_DIM // LANES
    v0 = k0 + C_KV_HEADS
    prev = lambda n: jnp.maximum(n * blocks - 1, 0)
    return pl.pallas_call(
        functools.partial(_swa_kernel, blocks=blocks),
        grid=(C_KV_HEADS, s // tq),
        in_specs=[pl.BlockSpec(memory_space=pltpu.SMEM),
                  pl.BlockSpec((tq, qw), lambda h, n: (n, h)),
                  pl.BlockSpec((w, LANES), lambda h, n: (prev(n), k0 + h)),
                  pl.BlockSpec((tq, LANES), lambda h, n: (n, k0 + h)),
                  pl.BlockSpec((w, LANES), lambda h, n: (prev(n), v0 + h)),
                  pl.BlockSpec((tq, LANES), lambda h, n: (n, v0 + h))],
        out_specs=pl.BlockSpec((tq, qw), lambda h, n: (n, h)),
        out_shape=jax.ShapeDtypeStruct((s, C_Q_HEADS * C_HEAD_DIM), BF16),
        compiler_params=_params(("arbitrary", "arbitrary"),
                                ((tq, qw), BF16, 4), ((tq + w, LANES), BF16, 4)),
        name="swa_sink",
    )(sinks, proj, proj, proj, proj, proj)


def _out_proj_kernel(*refs, n_in):
    a_refs = refs[:n_in]
    w_ref, x_ref, o_ref, wb_ref = refs[n_in:]

    @pl.when(pl.program_id(0) == 0)
    def _():
        wb_ref[...] = w_ref[...].astype(BF16)

    acc = x_ref[...]
    row = 0
    for a_ref in a_refs:
        ka = a_ref.shape[1]
        acc = acc + jnp.dot(a_ref[...], wb_ref[row:row + ka, :], preferred_element_type=F32)
        row += ka
    o_ref[...] = acc


def _out_proj(acts, w, layer, x, *, tm=512):
    s, d = x.shape
    n_in = len(acts)
    kw = w.shape[1]
    a_specs = [pl.BlockSpec((tm, a.shape[1]), lambda i: (i, 0)) for a in acts]
    w_spec = pl.BlockSpec((None, kw, d), lambda i: (layer, 0, 0), pipeline_mode=pl.Buffered(1))
    return pl.pallas_call(
        functools.partial(_out_proj_kernel, n_in=n_in),
        grid=(s // tm,),
        in_specs=a_specs + [w_spec, pl.BlockSpec((tm, d), lambda i: (i, 0))],
        out_specs=pl.BlockSpec((tm, d), lambda i: (i, 0)),
        out_shape=jax.ShapeDtypeStruct((s, d), F32),
        scratch_shapes=[pltpu.VMEM((kw, d), BF16)],
        compiler_params=_params(("arbitrary",),
                                ((tm, kw), BF16, 2), ((kw, d), F32, 1), ((kw, d), BF16, 1),
                                ((tm, d), F32, 4)),
        name="out_proj",
    )(*acts, w, x)


def _mlp_kernel(x_ref, g_ref, wu_ref, wd_ref, gf_ref, o_ref, xn_ref, *, final_norm):
    f = pl.program_id(1)

    def up_down(xn):
        hid = jnp.dot(xn, wu_ref[...].astype(BF16), preferred_element_type=F32)
        hid = jnp.square(jnp.maximum(hid, 0.0)).astype(BF16)
        return jnp.dot(hid, wd_ref[...].astype(BF16), preferred_element_type=F32)

    @pl.when(f == 0)
    def _():
        x = x_ref[...]
        xn = _rms(x, g_ref[...]).astype(BF16)
        xn_ref[...] = xn
        o_ref[...] = x + up_down(xn)

    @pl.when(f > 0)
    def _():
        o_ref[...] += up_down(xn_ref[...])

    if final_norm:
        @pl.when(f == pl.num_programs(1) - 1)
        def _():
            o_ref[...] = _rms(o_ref[...], gf_ref[...])


def _mlp(x, g, w_up, w_down, layer, g_final, *, final_norm, tm=1024, tf=512):
    s, d = x.shape
    dff = w_up.shape[2]
    return pl.pallas_call(
        functools.partial(_mlp_kernel, final_norm=final_norm),
        grid=(s // tm, dff // tf),
        in_specs=[pl.BlockSpec((tm, d), lambda i, f: (i, 0)),
                  pl.BlockSpec((1, d), lambda i, f: (0, 0)),
                  pl.BlockSpec((None, d, tf), lambda i, f: (layer, 0, f)),
                  pl.BlockSpec((None, tf, d), lambda i, f: (layer, f, 0)),
                  pl.BlockSpec((1, d), lambda i, f: (0, 0))],
        out_specs=pl.BlockSpec((tm, d), lambda i, f: (i, 0)),
        out_shape=jax.ShapeDtypeStruct((s, d), F32),
        scratch_shapes=[pltpu.VMEM((tm, d), BF16)],
        compiler_params=_params(("arbitrary", "arbitrary"),
                                ((tm, d), F32, 4), ((d, tf), F32, 4), ((tm, d), BF16, 1)),
        name="mlp",
    )(x, g.reshape(1, d), w_up, w_down, g_final.reshape(1, d))


def kernel(x, positions, norm_mix, norm_mlp, norm_final, w_in_ab, w_out_ab, lambda_q1, lambda_k1,
           lambda_q2, lambda_k2, diff_subln, w_in_c, w_out_c, sinks, w_up, w_down):
    b, s, d = x.shape
    assert b == 1
    depth = norm_mix.shape[0]
    h = x.reshape(s, d)
    cos, sin = _rope_tables(positions)

    a_width = A_HEADS * LANES
    c_q = C_Q_HEADS * C_HEAD_DIM
    c_kv = C_KV_HEADS * C_HEAD_DIM
    for layer in range(depth):
        j = layer // 2
        if layer % 2 == 0:
            paired = 2 * a_width // PROJ_CHUNK
            proj = _norm_proj(h, norm_mix[layer], w_in_ab, j, cos, sin, paired=paired,
                              n_main=w_in_ab.shape[2], scaled_cols=a_width, rope_cols=2 * a_width,
                              scale=ROPE_DIM ** -0.5 * math.log2(math.e))
            at = functools.partial(_group_blocks, paired=paired)
            lam_init = 0.8 - 0.6 * math.exp(-0.3 * layer)
            lam_params = jnp.stack([lambda_q1[j], lambda_k1[j], lambda_q2[j], lambda_k2[j]])
            oa = _diff_attn(proj, lam_params, diff_subln[j], lam_init,
                            q_at=at(0), k_at=at(2), v_at=at(4))
            ob = _stickbreak(proj, q_at=at(6), k_at=at(8), v_at=at(10))
            h = _out_proj([oa, ob], w_out_ab, j, h)
        else:
            proj = _norm_proj(h, norm_mix[layer], w_in_c, j, cos, sin, tail_rope_cols=c_kv,
                              n_main=c_q, scaled_cols=c_q, rope_cols=c_q,
                              scale=C_HEAD_DIM ** -0.5 * math.log2(math.e))
            oc = _swa(proj, sinks[j])
            h = _out_proj([oc], w_out_c, j, h)
        h = _mlp(h, norm_mlp[layer], w_up, w_down, layer,
                 norm_final, final_norm=(layer == depth - 1))
    return h.reshape(b, s, d)
```
